```python
import math
import jax, jax.numpy as jnp
from jax import lax
import numpy as np

D_MODEL = 2048
BATCH = 1
SEQ = 8192
DEPTH = 2
DEC_BATCH = 32
DEC_SEQ = 4
PAST_LEN = 8192
PAGE_SIZE = 128

HEAD_DIM = 128
N_HEADS = D_MODEL // HEAD_DIM
D_MIX = N_HEADS * HEAD_DIM
H_SB = N_HEADS // 4
H_DIFF = N_HEADS // 4
H_NSA = N_HEADS - H_SB - H_DIFF
G_NSA = H_NSA // 4
DIFF_DIM = HEAD_DIM // 2
N_KV = H_SB + H_DIFF + 2 * G_NSA
N_IN = 3 * H_SB * HEAD_DIM + 3 * H_DIFF * HEAD_DIM + H_NSA * HEAD_DIM + 6 * G_NSA * HEAD_DIM + 3 * H_NSA
CMP_LEN = 32
CMP_STRIDE = 16
SLC_LEN = 64
SLC_TOPK = 16
WINDOW = 512
D_FF = -(-8 * D_MODEL // (3 * 256)) * 256
ROPE_THETA = 10000.0
EPS = 1e-6
BLOCK_Q = 128

kernel_name = 'hybrid_sb_diff_nsa_decoder_step'


def _rms(x, g):
    x32 = x.astype(jnp.float32)
    y = x32 * lax.rsqrt(jnp.mean(x32 * x32, axis=-1, keepdims=True) + EPS)
    return (y * g.astype(jnp.float32)).astype(x.dtype)


def _rope(x, pos):
    half = x.shape[-1] // 2
    inv = jnp.power(ROPE_THETA, -jnp.arange(half, dtype=jnp.float32) / half)
    ang = pos.astype(jnp.float32)[:, None] * inv[None, :]
    shape = (pos.shape[0],) + (1,) * (x.ndim - 3) + (half,)
    cos, sin = jnp.cos(ang).reshape(shape), jnp.sin(ang).reshape(shape)
    x32 = x.astype(jnp.float32)
    x1, x2 = x32[..., :half], x32[..., half:]
    return jnp.concatenate([x1 * cos - x2 * sin, x2 * cos + x1 * sin], axis=-1).astype(x.dtype)


def _masked_softmax(s, mask):
    s = jnp.where(mask, s.astype(jnp.float32), -jnp.inf)
    m = jnp.max(s, axis=-1, keepdims=True)
    e = jnp.exp(s - jnp.where(jnp.isfinite(m), m, 0.0))
    return e / jnp.maximum(jnp.sum(e, axis=-1, keepdims=True), 1e-30)


def _weighted_values(w, v_segs):
    out, off = None, 0
    for v in v_segs:
        n = v.shape[1]
        part = jnp.einsum('bhqk,bkhd->bqhd', w[..., off:off + n].astype(v.dtype), v)
        out = part if out is None else out + part
        off += n
    return out


def _pad_rows(x, mult):
    extra = (-x.shape[1]) % mult
    return jnp.pad(x, [(0, 0), (0, extra)] + [(0, 0)] * (x.ndim - 2))


def _project(h, w_in, pos):
    B, T, _ = h.shape
    proj = jnp.einsum('btd,de->bte', h, w_in)
    sizes = [H_SB * HEAD_DIM] * 3 + [H_DIFF * HEAD_DIM] * 3 + [H_NSA * HEAD_DIM, 6 * G_NSA * HEAD_DIM, 3 * H_NSA]
    qa, ka, va, qb, kb, vb, qc, kvc, gl = jnp.split(proj, np.cumsum(sizes)[:-1].tolist(), axis=-1)
    heads = lambda a, n: a.reshape(B, T, n, HEAD_DIM)
    kvc = kvc.reshape(B, T, 6, G_NSA, HEAD_DIM)
    return dict(
        qa=heads(qa, H_SB), ka=heads(ka, H_SB), va=heads(va, H_SB),
        qb=_rope(qb.reshape(B, T, H_DIFF, 2, DIFF_DIM), pos),
        kb=_rope(kb.reshape(B, T, H_DIFF, 2, DIFF_DIM), pos),
        vb=heads(vb, H_DIFF),
        qc=_rope(heads(qc, H_NSA), pos),
        kc=_rope(kvc[:, :, 0], pos), vc=kvc[:, :, 1],
        ks=_rope(kvc[:, :, 2], pos), vs=kvc[:, :, 3],
        kw=_rope(kvc[:, :, 4], pos), vw=kvc[:, :, 5],
        gates=jax.nn.sigmoid(gl.reshape(B, T, H_NSA, 3)))


def _cache_rows(p):
    B, T = p['ka'].shape[:2]
    k = jnp.concatenate([p['ka'], p['kb'].reshape(B, T, H_DIFF, HEAD_DIM), p['kc'], p['ks']], axis=2)
    v = jnp.concatenate([p['va'], p['vb'], p['vc'], p['vs']], axis=2)
    return jnp.stack([k, v], axis=2), jnp.stack([p['kw'], p['vw']], axis=2)


def _stick_breaking(q, k_segs, v_segs, q_pos, k_pos):
    scale = q.shape[-1] ** -0.5
    z = jnp.concatenate([jnp.einsum('bqhd,bkhd->bhqk', q, k) for k in k_segs], axis=-1).astype(jnp.float32) * scale
    kp = jnp.concatenate(k_pos)
    mask = kp[None, :] < q_pos[:, None]
    log_keep = jnp.where(mask, jax.nn.log_sigmoid(-z), 0.0)
    after = lax.cumsum(log_keep, axis=3, reverse=True) - log_keep
    w = jnp.where(mask, jnp.exp(jax.nn.log_sigmoid(z) + after), 0.0)
    return _weighted_values(w, v_segs)


def _diff_lambda(lam_vecs, layer):
    lam_init = 0.8 - 0.6 * math.exp(-0.3 * layer)
    lv = lam_vecs.astype(jnp.float32)
    lam = jnp.exp(jnp.sum(lv[0] * lv[1])) - jnp.exp(jnp.sum(lv[2] * lv[3])) + lam_init
    return lam, lam_init


def _diff_attn(q, k_segs, v_segs, q_pos, k_pos, lam):
    scale = q.shape[-1] ** -0.5
    s = jnp.concatenate([jnp.einsum('bqhcd,bkhcd->bchqk', q, k) for k in k_segs], axis=-1).astype(jnp.float32) * scale
    kp = jnp.concatenate(k_pos)
    p = _masked_softmax(s, kp[None, :] <= q_pos[:, None])
    return _weighted_values(p[:, 0] - lam * p[:, 1], v_segs)


def _compress(k, pe, w1, w2):
    B, T, G, d = k.shape
    n_sub = CMP_LEN // CMP_STRIDE
    nc = T // CMP_STRIDE - n_sub + 1
    ch = k.reshape(B, T // CMP_STRIDE, CMP_STRIDE, G, d)
    pe = pe.reshape(n_sub, CMP_STRIDE, 1, d)
    w1 = w1.reshape(n_sub, CMP_STRIDE, d, d)
    h = None
    for c in range(n_sub):
        part = jnp.einsum('bnlgd,lde->bnge', ch[:, c:c + nc] + pe[c], w1[c])
        h = part if h is None else h + part
    return jnp.einsum('bnge,ef->bngf', jax.nn.silu(h), w2)


def _block_importance(imp, n_blk):
    r = SLC_LEN // CMP_STRIDE
    c = CMP_LEN // CMP_STRIDE
    pp = jnp.pad(imp, [(0, 0)] * (imp.ndim - 1) + [(c - 1, r * n_blk - imp.shape[-1])])
    out = None
    for o in range(r + c - 1):
        part = pp[..., o:o + r * n_blk:r]
        out = part if out is None else out + part
    return out


def _nsa(q, q_pos, gates, k_cmp, v_cmp, k_slc, v_slc, k_win, v_win, win_pos):
    B, Tq, H, d = q.shape
    G = k_slc.shape[2]
    hpg = H // G
    scale = d ** -0.5
    qg = q.reshape(B, Tq, G, hpg, d)
    nc = k_cmp.shape[1]
    cmp_end = jnp.arange(nc) * CMP_STRIDE + CMP_LEN - 1
    s = jnp.einsum('bqghd,bngd->bghqn', qg, k_cmp) * scale
    p_cmp = _masked_softmax(s, cmp_end[None, :] <= q_pos[:, None])
    o_cmp = jnp.einsum('bghqn,bngd->bqghd', p_cmp.astype(v_cmp.dtype), v_cmp)
    n_blk = k_slc.shape[1] // SLC_LEN
    imp = _block_importance(jnp.sum(p_cmp, axis=2), n_blk)
    blk = jnp.arange(n_blk)[None, :]
    cur = (q_pos // SLC_LEN)[:, None]
    forced = (blk == 0) | (blk == cur) | (blk == cur - 1)
    score = jnp.where(forced, jnp.inf, jnp.where(blk <= cur, imp, -jnp.inf))
    _, idx = lax.top_k(score, min(SLC_TOPK, n_blk))
    n_sel = idx.shape[-1]
    bi = jnp.arange(B)[:, None, None, None]
    gi = jnp.arange(G)[None, :, None, None]
    ksb = k_slc.reshape(B, n_blk, SLC_LEN, G, d).transpose(0, 3, 1, 2, 4)
    vsb = v_slc.reshape(B, n_blk, SLC_LEN, G, d).transpose(0, 3, 1, 2, 4)
    k_sel, v_sel = ksb[bi, gi, idx], vsb[bi, gi, idx]
    tok_pos = idx[..., None] * SLC_LEN + jnp.arange(SLC_LEN)
    sel_mask = (tok_pos <= q_pos[:, None, None])[:, :, None].reshape(B, G, 1, Tq, n_sel * SLC_LEN)
    s = jnp.einsum('bqghd,bgqnld->bghqnl', qg, k_sel) * scale
    p = _masked_softmax(s.reshape(B, G, hpg, Tq, n_sel * SLC_LEN), sel_mask)
    o_slc = jnp.einsum('bghqnl,bgqnld->bqghd', p.reshape(B, G, hpg, Tq, n_sel, SLC_LEN).astype(v_sel.dtype), v_sel)
    dist = q_pos[:, None] - win_pos[None, :]
    s = jnp.einsum('bqghd,bkgd->bghqk', qg, k_win) * scale
    p = _masked_softmax(s, (dist >= 0) & (dist <= WINDOW) & (win_pos[None, :] >= 0))
    o_win = jnp.einsum('bghqk,bkgd->bqghd', p.astype(v_win.dtype), v_win)
    g = gates.reshape(B, Tq, G, hpg, 3)
    o = g[..., 0:1] * o_cmp + g[..., 1:2] * o_slc + g[..., 2:3] * o_win
    return o.reshape(B, Tq, H, d)


def _merge_heads(o_a, o_b, o_c, lw, lam_init):
    o_b = _rms(o_b, lw['diff_norm']) * (1.0 - lam_init)
    o = jnp.concatenate([o_a, o_b, o_c], axis=2)
    B, T = o.shape[:2]
    return jnp.einsum('bte,ed->btd', o.reshape(B, T, D_MIX), lw['w_out'])


def _mix_prompt(h, lw, layer):
    B, T, _ = h.shape
    pos = jnp.arange(T)
    p = _project(h, lw['w_in'], pos)
    lam, lam_init = _diff_lambda(lw['diff_lambda'], layer)
    k_cmp = _compress(p['kc'], lw['cmp_pe'][0], lw['cmp_w1'][0], lw['cmp_w2'][0])
    v_cmp = _compress(p['vc'], lw['cmp_pe'][1], lw['cmp_w1'][1], lw['cmp_w2'][1])
    pad = ((0, 0), (WINDOW, 0), (0, 0), (0, 0))
    kw_pad, vw_pad = jnp.pad(p['kw'], pad), jnp.pad(p['vw'], pad)
    span = WINDOW + BLOCK_Q

    def block(i):
        q0 = i * BLOCK_Q
        qpos = q0 + jnp.arange(BLOCK_Q)
        rows = lambda a: lax.dynamic_slice_in_dim(a, q0, BLOCK_Q, axis=1)
        o_a = _stick_breaking(rows(p['qa']), (p['ka'],), (p['va'],), qpos, (pos,))
        o_b = _diff_attn(rows(p['qb']), (p['kb'],), (p['vb'],), qpos, (pos,), lam)
        o_c = _nsa(rows(p['qc']), qpos, rows(p['gates']), k_cmp, v_cmp, p['ks'], p['vs'],
                   lax.dynamic_slice_in_dim(kw_pad, q0, span, axis=1),
                   lax.dynamic_slice_in_dim(vw_pad, q0, span, axis=1),
                   q0 - WINDOW + jnp.arange(span))
        return o_a, o_b, o_c

    unblock = lambda o: jnp.moveaxis(o, 0, 1).reshape((B, T) + o.shape[3:])
    o_a, o_b, o_c = (unblock(o) for o in lax.map(block, jnp.arange(T // BLOCK_Q)))
    y = _merge_heads(o_a, o_b, o_c, lw, lam_init)
    kv_rows, win_rows = _cache_rows(p)
    return y, (kv_rows, win_rows[:, T - min(WINDOW, T):])


def _mix_sample(h, lw, layer, cache_kv, state_win_kv, page_table):
    B, S, _ = h.shape
    past = page_table.shape[1] * PAGE_SIZE
    pos = past + jnp.arange(S)
    past_pos = jnp.arange(past)
    p = _project(h, lw['w_in'], pos)
    lam, lam_init = _diff_lambda(lw['diff_lambda'], layer)

    def gather(lo, hi):
        return cache_kv[layer, page_table, :, :, lo:hi].reshape(B, past, 2, hi - lo, HEAD_DIM)

    kv_a = gather(0, H_SB)
    kv_b = gather(H_SB, H_SB + H_DIFF)
    kv_c = gather(H_SB + H_DIFF, N_KV)
    o_a = _stick_breaking(p['qa'], (kv_a[:, :, 0], p['ka']), (kv_a[:, :, 1], p['va']), pos, (past_pos, pos))
    kb_past = kv_b[:, :, 0].reshape(B, past, H_DIFF, 2, DIFF_DIM)
    o_b = _diff_attn(p['qb'], (kb_past, p['kb']), (kv_b[:, :, 1], p['vb']), pos, (past_pos, pos), lam)
    full = lambda old, new: _pad_rows(jnp.concatenate([old, new], axis=1), SLC_LEN)
    k_cmp = _compress(full(kv_c[:, :, 0, :G_NSA], p['kc']), lw['cmp_pe'][0], lw['cmp_w1'][0], lw['cmp_w2'][0])
    v_cmp = _compress(full(kv_c[:, :, 1, :G_NSA], p['vc']), lw['cmp_pe'][1], lw['cmp_w1'][1], lw['cmp_w2'][1])
    k_slc = full(kv_c[:, :, 0, G_NSA:], p['ks'])
    v_slc = full(kv_c[:, :, 1, G_NSA:], p['vs'])
    kv_rows, win_rows = _cache_rows(p)
    buf = state_win_kv[layer]
    wb = buf.shape[1]
    win_all = jnp.concatenate([buf, win_rows], axis=1)
    o_c = _nsa(p['qc'], pos, p['gates'], k_cmp, v_cmp, k_slc, v_slc,
               win_all[:, :, 0], win_all[:, :, 1], past - wb + jnp.arange(wb + S))
    y = _merge_heads(o_a, o_b, o_c, lw, lam_init)
    return y, (kv_rows, win_all[:, S:])


def _swiglu(h, w_gate_up, w_down):
    a, b = jnp.split(jnp.einsum('btd,df->btf', h, w_gate_up), 2, axis=-1)
    return jnp.einsum('btf,fd->btd', jax.nn.silu(a) * b, w_down)


def _layer(x, c, lw, mixer):
    mod = jnp.einsum('bd,de->be', jax.nn.silu(c), lw['ada_w']) + lw['ada_b']
    sh_a, sc_a, g_a, sh_f, sc_f, g_f = jnp.split(mod[:, None, :], 6, axis=-1)
    nrm = lw['norms']
    y, state = mixer(_rms(x, nrm[0]) * (1.0 + sc_a) + sh_a)
    x = x + g_a * _rms(y, nrm[1])
    f = _swiglu(_rms(x, nrm[2]) * (1.0 + sc_f) + sh_f, lw['w_gate_up'], lw['w_down'])
    x = x + g_f * _rms(f, nrm[3])
    return x, state


def setup_inputs(seed: int = 0) -> dict:
    key = jax.random.key(seed)
    k = jax.random.split(key, 20)
    n_pages = PAST_LEN // PAGE_SIZE
    n_used = DEC_BATCH * n_pages
    n_pool = n_used + n_used // 4 + 1
    wb = min(WINDOW, PAST_LEN)
    normal = lambda kk, shape, scale: jax.random.normal(kk, shape, jnp.float32) * scale
    page_table = jax.random.permutation(k[4], n_pool)[:n_used].reshape(DEC_BATCH, n_pages).astype(jnp.int32)
    return {
        'x_prompt': normal(k[0], (BATCH, SEQ, D_MODEL), 1.0),
        'x_sample': normal(k[1], (DEC_BATCH, DEC_SEQ, D_MODEL), 1.0),
        'cache_kv': normal(k[2], (DEPTH, n_pool, PAGE_SIZE, 2, N_KV, HEAD_DIM), 1.0),
        'state_win_kv': normal(k[3], (DEPTH, DEC_BATCH, wb, 2, G_NSA, HEAD_DIM), 1.0),
        'page_table': page_table,
        'c_prompt': normal(k[5], (BATCH, D_MODEL), 1.0),
        'c_sample': normal(k[6], (DEC_BATCH, D_MODEL), 1.0),
        'ada_w': normal(k[7], (DEPTH, D_MODEL, 6 * D_MODEL), 0.5 * D_MODEL ** -0.5),
        'ada_b': normal(k[8], (DEPTH, 6 * D_MODEL), 0.02),
        'norms': 1.0 + normal(k[9], (DEPTH, 4, D_MODEL), 0.05),
        'w_in': normal(k[10], (DEPTH, D_MODEL, N_IN), D_MODEL ** -0.5),
        'w_out': normal(k[11], (DEPTH, D_MIX, D_MODEL), D_MIX ** -0.5),
        'diff_lambda': normal(k[12], (DEPTH, 4, DIFF_DIM), 0.1),
        'diff_norm': 1.0 + normal(k[13], (DEPTH, HEAD_DIM), 0.05),
        'cmp_pe': normal(k[14], (DEPTH, 2, CMP_LEN, HEAD_DIM), 0.1),
        'cmp_w1': normal(k[15], (DEPTH, 2, CMP_LEN * HEAD_DIM, HEAD_DIM), (CMP_LEN * HEAD_DIM) ** -0.5),
        'cmp_w2': normal(k[16], (DEPTH, 2, HEAD_DIM, HEAD_DIM), HEAD_DIM ** -0.5),
        'w_gate_up': normal(k[17], (DEPTH, D_MODEL, 2 * D_FF), D_MODEL ** -0.5),
        'w_down': normal(k[18], (DEPTH, D_FF, D_MODEL), D_FF ** -0.5),
    }


def reference(x_prompt, x_sample, cache_kv, state_win_kv, page_table, c_prompt, c_sample,
              ada_w, ada_b, norms, w_in, w_out, diff_lambda, diff_norm, cmp_pe, cmp_w1, cmp_w2,
              w_gate_up, w_down):
    y_prompt, y_sample = x_prompt, x_sample
    kv_p, kv_s, win_p, win_s = [], [], [], []
    for layer in range(DEPTH):
        lw = dict(ada_w=ada_w[layer], ada_b=ada_b[layer], norms=norms[layer], w_in=w_in[layer],
                  w_out=w_out[layer], diff_lambda=diff_lambda[layer], diff_norm=diff_norm[layer],
                  cmp_pe=cmp_pe[layer], cmp_w1=cmp_w1[layer], cmp_w2=cmp_w2[layer],
                  w_gate_up=w_gate_up[layer], w_down=w_down[layer])
        y_prompt, (kv, win) = _layer(y_prompt, c_prompt, lw, lambda h: _mix_prompt(h, lw, layer))
        kv_p.append(kv)
        win_p.append(win)
        y_sample, (kv, win) = _layer(y_sample, c_sample, lw,
                                     lambda h: _mix_sample(h, lw, layer, cache_kv, state_win_kv, page_table))
        kv_s.append(kv)
        win_s.append(win)
    kv_prompt = jnp.stack(kv_p)
    kv_sample = jnp.stack(kv_s)
    win_prompt = jnp.stack(win_p)
    win_sample = jnp.stack(win_s)
    return (y_prompt, y_sample, kv_prompt, kv_sample, win_prompt, win_sample)
```

```python
import functools
import math

import numpy as np
import jax
import jax.numpy as jnp
from jax import lax
from jax.experimental import pallas as pl
from jax.experimental.pallas import tpu as pltpu

F32 = jnp.float32
BF16 = jnp.bfloat16

HEAD_DIM = 128
H_SB = 4
H_DIFF = 4
H_NSA = 8
G_NSA = 2
HPG = H_NSA // G_NSA
N_KV = H_SB + H_DIFF + 2 * G_NSA
N_HEADS = H_SB + H_DIFF + H_NSA
DIFF_DIM = HEAD_DIM // 2
CMP_LEN = 32
CMP_STRIDE = 16
SLC_LEN = 64
SLC_TOPK = 16
WINDOW = 512
PAGE_SIZE = 128
ROPE_THETA = 10000.0
EPS = 1e-6
NEG = -1e30
LANES = 128
VMEM_LIMIT = 56 * 1024 * 1024

C_QA, C_KA, C_VA = 0, 512, 1024
C_QB, C_KB, C_VB = 1536, 2048, 2560
C_QC = 3072
C_KC, C_VC, C_KS, C_VS, C_KW, C_VW = 4096, 4352, 4608, 4864, 5120, 5376
C_GATE = 5632
N_IN = 5656
N_IN_PAD = 5760


def _cparams(sem):
    return pltpu.CompilerParams(dimension_semantics=sem, vmem_limit_bytes=VMEM_LIMIT)


def _silu(x):
    return x * jax.nn.sigmoid(x)


def _rms_rows(x, g):
    return x * lax.rsqrt(jnp.mean(x * x, axis=-1, keepdims=True) + EPS) * g


def _dot(a, b):
    return jnp.dot(a, b, preferred_element_type=F32)


def _dot_nt(a, b):
    return lax.dot_general(a, b, (((1,), (1,)), ((), ())), preferred_element_type=F32)


def _split3(x):
    hi = x.astype(BF16)
    r1 = x - hi.astype(F32)
    mid = r1.astype(BF16)
    lo = (r1 - mid.astype(F32)).astype(BF16)
    return hi, mid, lo


def _dot_f32ish(x, m01):
    hi, mid, lo = _split3(x)
    return _dot(hi, m01) + _dot(mid, m01) + _dot(lo, m01)


def _ada_kernel(c_ref, w_ref, b_ref, o_ref):
    a = _silu(c_ref[...]).astype(BF16)
    o_ref[...] = _dot(a, w_ref[...].astype(BF16)) + b_ref[...]


def _ada(c_all, ada_w, ada_b):
    depth, d, n = ada_w.shape
    rows = c_all.shape[0]
    tn = 1024
    return pl.pallas_call(
        _ada_kernel,
        grid=(depth, n // tn),
        in_specs=[pl.BlockSpec((rows, d), lambda l, j: (0, 0)),
                  pl.BlockSpec((None, d, tn), lambda l, j: (l, 0, j)),
                  pl.BlockSpec((None, 1, tn), lambda l, j: (l, 0, j))],
        out_specs=pl.BlockSpec((None, rows, tn), lambda l, j: (l, 0, j)),
        out_shape=jax.ShapeDtypeStruct((depth, rows, n), F32),
        compiler_params=_cparams(("arbitrary", "arbitrary")),
        name="ada_mod",
    )(c_all, ada_w, ada_b.reshape(depth, 1, n))


def _row_spec(arr, tm, grid_rank):
    c = arr.shape[1]
    if arr.shape[0] == 1:
        if grid_rank == 1:
            return pl.BlockSpec((1, c), lambda i: (0, 0))
        return pl.BlockSpec((1, c), lambda i, j: (0, 0))
    if grid_rank == 1:
        return pl.BlockSpec((tm, c), lambda i: (i, 0))
    return pl.BlockSpec((tm, c), lambda i, j: (i, 0))


def _proj_kernel(x_ref, g_ref, sc_ref, sh_ref, w_ref, o_ref, h_scr):
    @pl.when(pl.program_id(1) == 0)
    def _():
        y = _rms_rows(x_ref[...], g_ref[...])
        h_scr[...] = (y * (1.0 + sc_ref[...]) + sh_ref[...]).astype(BF16)

    o_ref[...] = _dot(h_scr[...], w_ref[...])


def _proj(x, g, sc, sh, w):
    t, d = x.shape
    n = w.shape[1]
    tm = min(512, t)
    tn = 1920
    return pl.pallas_call(
        _proj_kernel,
        grid=(t // tm, n // tn),
        in_specs=[pl.BlockSpec((tm, d), lambda i, j: (i, 0)),
                  pl.BlockSpec((1, d), lambda i, j: (0, 0)),
                  _row_spec(sc, tm, 2), _row_spec(sh, tm, 2),
                  pl.BlockSpec((d, tn), lambda i, j: (0, j))],
        out_specs=pl.BlockSpec((tm, tn), lambda i, j: (i, j)),
        out_shape=jax.ShapeDtypeStruct((t, n), F32),
        scratch_shapes=[pltpu.VMEM((tm, d), BF16)],
        compiler_params=_cparams(("arbitrary", "arbitrary")),
        name="in_proj",
    )(x, g, sc, sh, w)


def _postproj_kernel(p_ref, ca_ref, sa_ref, cb_ref, sb_ref,
                     qsb_ref, ksb_ref, vsb_ref, qdf_ref, kdf_ref, vdf_ref, qns_ref,
                     kc_ref, vc_ref, ks_ref, vs_ref, kw_ref, vw_ref, gate_ref, kv_ref, win_ref):
    ca, sa, cb, sb = ca_ref[...], sa_ref[...], cb_ref[...], sb_ref[...]
    lane = lax.broadcasted_iota(jnp.int32, ca.shape, 1)
    low_half = (lane % DIFF_DIM) < (DIFF_DIM // 2)
    first_chunk = lane < DIFF_DIM
    sm_scale = HEAD_DIM ** -0.5
    df_scale = DIFF_DIM ** -0.5

    def col(c0, h):
        return p_ref[:, c0 + h * HEAD_DIM:c0 + (h + 1) * HEAD_DIM]

    def rope_a(x):
        return x * ca + pltpu.roll(x, HEAD_DIM // 2, 1) * sa

    def rope_b(x):
        partner = jnp.where(low_half, pltpu.roll(x, HEAD_DIM - DIFF_DIM // 2, 1), pltpu.roll(x, DIFF_DIM // 2, 1))
        return x * cb + partner * sb

    def put_kv(is_v, head, val):
        c0 = (N_KV * is_v + head) * HEAD_DIM
        kv_ref[:, c0:c0 + HEAD_DIM] = val

    for h in range(H_SB):
        qsb_ref[h] = (col(C_QA, h) * sm_scale).astype(BF16)
        k, v = col(C_KA, h), col(C_VA, h)
        ksb_ref[h] = k.astype(BF16)
        vsb_ref[h] = v.astype(BF16)
        put_kv(0, h, k)
        put_kv(1, h, v)
    for h in range(H_DIFF):
        q = rope_b(col(C_QB, h)) * df_scale
        qdf_ref[2 * h] = jnp.where(first_chunk, q, 0.0).astype(BF16)
        qdf_ref[2 * h + 1] = jnp.where(first_chunk, 0.0, q).astype(BF16)
        k, v = rope_b(col(C_KB, h)), col(C_VB, h)
        kdf_ref[h] = k.astype(BF16)
        vdf_ref[h] = v.astype(BF16)
        put_kv(0, H_SB + h, k)
        put_kv(1, H_SB + h, v)
    for h in range(H_NSA):
        qns_ref[h] = (rope_a(col(C_QC, h)) * sm_scale).astype(BF16)
    for g in range(G_NSA):
        k, v = rope_a(col(C_KC, g)), col(C_VC, g)
        kc_ref[g] = k
        vc_ref[g] = v
        put_kv(0, H_SB + H_DIFF + g, k)
        put_kv(1, H_SB + H_DIFF + g, v)
        k, v = rope_a(col(C_KS, g)), col(C_VS, g)
        ks_ref[g] = k.astype(BF16)
        vs_ref[g] = v.astype(BF16)
        put_kv(0, H_SB + H_DIFF + G_NSA + g, k)
        put_kv(1, H_SB + H_DIFF + G_NSA + g, v)
        k, v = rope_a(col(C_KW, g)), col(C_VW, g)
        kw_ref[g] = k.astype(BF16)
        vw_ref[g] = v.astype(BF16)
        win_ref[:, g * HEAD_DIM:(g + 1) * HEAD_DIM] = k
        win_ref[:, (G_NSA + g) * HEAD_DIM:(G_NSA + g + 1) * HEAD_DIM] = v
    gate_ref[...] = jax.nn.sigmoid(p_ref[:, C_GATE:C_GATE + LANES])


def _postproj(proj, tabs):
    t = proj.shape[0]
    tm = min(256, t)
    hm = lambda n, dt: jax.ShapeDtypeStruct((n, t, HEAD_DIM), dt)
    hspec = lambda n: pl.BlockSpec((n, tm, HEAD_DIM), lambda i: (0, i, 0))
    tab_spec = pl.BlockSpec((tm, LANES), lambda i: (i, 0))
    out_shapes = [hm(H_SB, BF16), hm(H_SB, BF16), hm(H_SB, BF16),
                  hm(2 * H_DIFF, BF16), hm(H_DIFF, BF16), hm(H_DIFF, BF16), hm(H_NSA, BF16),
                  hm(G_NSA, F32), hm(G_NSA, F32), hm(G_NSA, BF16), hm(G_NSA, BF16),
                  hm(G_NSA, BF16), hm(G_NSA, BF16),
                  jax.ShapeDtypeStruct((t, LANES), F32),
                  jax.ShapeDtypeStruct((t, 2 * N_KV * HEAD_DIM), F32),
                  jax.ShapeDtypeStruct((t, 2 * G_NSA * HEAD_DIM), F32)]
    out_specs = [hspec(H_SB), hspec(H_SB), hspec(H_SB), hspec(2 * H_DIFF), hspec(H_DIFF), hspec(H_DIFF),
                 hspec(H_NSA), hspec(G_NSA), hspec(G_NSA), hspec(G_NSA), hspec(G_NSA), hspec(G_NSA), hspec(G_NSA),
                 pl.BlockSpec((tm, LANES), lambda i: (i, 0)),
                 pl.BlockSpec((tm, 2 * N_KV * HEAD_DIM), lambda i: (i, 0)),
                 pl.BlockSpec((tm, 2 * G_NSA * HEAD_DIM), lambda i: (i, 0))]
    return pl.pallas_call(
        _postproj_kernel,
        grid=(t // tm,),
        in_specs=[pl.BlockSpec((tm, N_IN_PAD), lambda i: (i, 0)), tab_spec, tab_spec, tab_spec, tab_spec],
        out_specs=out_specs,
        out_shape=out_shapes,
        compiler_params=_cparams(("arbitrary",)),
        name="rope_split",
    )(proj, *tabs)


def _rope_tables(pos):
    posf = pos.astype(F32)[:, None]

    def tab(half, reps):
        inv = jnp.power(ROPE_THETA, -jnp.arange(half, dtype=F32) / half)
        ang = posf * inv[None, :]
        c, s = jnp.cos(ang), jnp.sin(ang)
        return jnp.tile(jnp.concatenate([c, c], 1), (1, reps)), jnp.tile(jnp.concatenate([-s, s], 1), (1, reps))

    ca, sa = tab(HEAD_DIM // 2, 1)
    cb, sb = tab(DIFF_DIM // 2, 2)
    return ca, sa, cb, sb


def _pairs(nq, tq, tk, lo_fn, hi_fn, reverse):
    qi, kj, first, last = [], [], [], []
    for i in range(nq):
        ks = list(range(lo_fn(i), hi_fn(i) + 1))
        if reverse:
            ks = ks[::-1]
        for n, k in enumerate(ks):
            qi.append(i)
            kj.append(k)
            first.append(int(n == 0))
            last.append(int(n == len(ks) - 1))
    mk = lambda a: jnp.asarray(np.asarray(a, dtype=np.int32))
    return mk(qi), mk(kj), mk(first), mk(last)


def _sb_kernel(qi_ref, kj_ref, first_ref, last_ref, q_ref, k_ref, v_ref, u_ref, o_ref, carry_scr, acc_scr, *, tq, tk):
    p = pl.program_id(1)

    @pl.when(first_ref[p] == 1)
    def _():
        carry_scr[...] = jnp.zeros_like(carry_scr)
        acc_scr[...] = jnp.zeros_like(acc_scr)

    z = _dot_nt(q_ref[...], k_ref[...])
    qpos = qi_ref[p] * tq + lax.broadcasted_iota(jnp.int32, (tq, tk), 0)
    kpos = kj_ref[p] * tk + lax.broadcasted_iota(jnp.int32, (tq, tk), 1)
    mask = kpos < qpos
    l1p = jnp.log(1.0 + jnp.exp(-jnp.abs(z)))
    ls_pos = jnp.minimum(z, 0.0) - l1p
    lk = jnp.where(mask, ls_pos - z, 0.0)
    hi = lk.astype(BF16)
    lo = (lk - hi.astype(F32)).astype(BF16)
    u = u_ref[...]
    after = _dot(hi, u) + _dot(lo, u) + carry_scr[...]
    w = jnp.where(mask, jnp.exp(ls_pos + after), 0.0)
    acc_scr[...] += _dot(w.astype(BF16), v_ref[...])
    carry_scr[...] += jnp.sum(lk, axis=-1, keepdims=True)

    @pl.when(last_ref[p] == 1)
    def _():
        o_ref[...] = acc_scr[...]


def _suffix_matrix(n):
    j = np.arange(n)[:, None]
    s = np.arange(n)[None, :]
    return jnp.asarray((j > s).astype(np.float32), dtype=BF16)


def _sb_attention(q, k, v):
    h, t, d = q.shape
    tq = min(512, t)
    tk = min(256, t)
    sched = _pairs(t // tq, tq, tk, lambda i: 0, lambda i: ((i + 1) * tq - 2) // tk, reverse=True)
    npairs = sched[0].shape[0]
    grid_spec = pltpu.PrefetchScalarGridSpec(
        num_scalar_prefetch=4,
        grid=(h, npairs),
        in_specs=[pl.BlockSpec((None, tq, d), lambda hh, p, qi, kj, f, l: (hh, qi[p], 0)),
                  pl.BlockSpec((None, tk, d), lambda hh, p, qi, kj, f, l: (hh, kj[p], 0)),
                  pl.BlockSpec((None, tk, d), lambda hh, p, qi, kj, f, l: (hh, kj[p], 0)),
                  pl.BlockSpec((tk, tk), lambda hh, p, qi, kj, f, l: (0, 0))],
        out_specs=pl.BlockSpec((None, tq, d), lambda hh, p, qi, kj, f, l: (hh, qi[p], 0)),
        scratch_shapes=[pltpu.VMEM((tq, 1), F32), pltpu.VMEM((tq, d), F32)])
    return pl.pallas_call(
        functools.partial(_sb_kernel, tq=tq, tk=tk),
        grid_spec=grid_spec,
        out_shape=jax.ShapeDtypeStruct((h, t, d), F32),
        compiler_params=_cparams(("arbitrary", "arbitrary")),
        name="sb_attn",
    )(*sched, q, k, v, _suffix_matrix(tk))


def _flash_kernel(qi_ref, kj_ref, first_ref, last_ref, q_ref, k_ref, v_ref, *rest, mode, rep, tq, tk):
    if mode == "select":
        sel_ref, e_ref, o_ref, m_scr, l_scr, acc_scr = rest
    else:
        o_ref, m_scr, l_scr, acc_scr = rest
    p = pl.program_id(1)
    rows = rep * tq

    @pl.when(first_ref[p] == 1)
    def _():
        m_scr[...] = jnp.full_like(m_scr, NEG)
        l_scr[...] = jnp.zeros_like(l_scr)
        acc_scr[...] = jnp.zeros_like(acc_scr)

    q = q_ref[...].reshape(rows, HEAD_DIM)
    s = _dot_nt(q, k_ref[...])
    row = lax.broadcasted_iota(jnp.int32, (rows, tk), 0)
    qpos = qi_ref[p] * tq + (row & (tq - 1))
    kpos = kj_ref[p] * tk + lax.broadcasted_iota(jnp.int32, (rows, tk), 1)
    dist = qpos - kpos
    if mode == "window":
        mask = (dist >= 0) & (dist <= WINDOW)
    else:
        mask = dist >= 0
    if mode == "select":
        picked = _dot(sel_ref[...], e_ref[...])
        mask = mask & (jnp.concatenate([picked] * rep, axis=0) > 0.5)
    s = jnp.where(mask, s, NEG)
    m_old = m_scr[...]
    m_new = jnp.maximum(m_old, jnp.max(s, axis=-1, keepdims=True))
    alpha = jnp.exp(m_old - m_new)
    e = jnp.where(mask, jnp.exp(s - m_new), 0.0)
    l_scr[...] = alpha * l_scr[...] + jnp.sum(e, axis=-1, keepdims=True)
    acc_scr[...] = alpha * acc_scr[...] + _dot(e.astype(BF16), v_ref[...])
    m_scr[...] = m_new

    @pl.when(last_ref[p] == 1)
    def _():
        out = acc_scr[...] / jnp.maximum(l_scr[...], 1e-30)
        o_ref[...] = out.reshape(rep, tq, HEAD_DIM)


def _block_expander(t):
    b = np.arange(LANES)[:, None]
    key = np.arange(t)[None, :]
    return jnp.asarray((key // SLC_LEN == b).astype(np.float32), dtype=BF16)


def _flash(q, k, v, mode, sel=None):
    hq, t, d = q.shape
    hk = k.shape[0]
    rep = hq // hk
    tq = min(256, t)
    tk = min(512, t)
    if mode == "window":
        lo = lambda i: max(0, (i * tq - WINDOW) // tk)
    else:
        lo = lambda i: 0
    sched = _pairs(t // tq, tq, tk, lo, lambda i: ((i + 1) * tq - 1) // tk, reverse=False)
    npairs = sched[0].shape[0]
    in_specs = [pl.BlockSpec((rep, tq, d), lambda g, p, qi, kj, f, l: (g, qi[p], 0)),
                pl.BlockSpec((None, tk, d), lambda g, p, qi, kj, f, l: (g, kj[p], 0)),
                pl.BlockSpec((None, tk, d), lambda g, p, qi, kj, f, l: (g, kj[p], 0))]
    args = [q, k, v]
    if mode == "select":
        in_specs += [pl.BlockSpec((None, tq, LANES), lambda g, p, qi, kj, f, l: (g, qi[p], 0)),
                     pl.BlockSpec((LANES, tk), lambda g, p, qi, kj, f, l: (0, kj[p]))]
        args += [sel, _block_expander(t)]
    grid_spec = pltpu.PrefetchScalarGridSpec(
        num_scalar_prefetch=4,
        grid=(hk, npairs),
        in_specs=in_specs,
        out_specs=pl.BlockSpec((rep, tq, d), lambda g, p, qi, kj, f, l: (g, qi[p], 0)),
        scratch_shapes=[pltpu.VMEM((rep * tq, 1), F32), pltpu.VMEM((rep * tq, 1), F32),
                        pltpu.VMEM((rep * tq, d), F32)])
    return pl.pallas_call(
        functools.partial(_flash_kernel, mode=mode, rep=rep, tq=tq, tk=tk),
        grid_spec=grid_spec,
        out_shape=jax.ShapeDtypeStruct((hq, t, d), F32),
        compiler_params=_cparams(("arbitrary", "arbitrary")),
        name="attn_" + mode,
    )(*sched, *args)


def _compress_kernel(x_ref, pe_ref, w1_ref, w2_ref, o_ref):
    x = x_ref[...]
    n = x.shape[0]
    a = _dot((x + pe_ref[0:1, :]).astype(BF16), w1_ref[0].astype(BF16))
    b = _dot((x + pe_ref[1:2, :]).astype(BF16), w1_ref[1].astype(BF16))
    h = a + pltpu.roll(b, n - 1, 0)
    o_ref[...] = _dot(_silu(h).astype(BF16), w2_ref[...].astype(BF16)).astype(o_ref.dtype)


def _compress(xc, pe, w1, w2):
    _, g, n, wdt = xc.shape
    return pl.pallas_call(
        _compress_kernel,
        grid=(2, g),
        in_specs=[pl.BlockSpec((None, None, n, wdt), lambda a, b: (a, b, 0, 0)),
                  pl.BlockSpec((None, 2, wdt), lambda a, b: (a, 0, 0)),
                  pl.BlockSpec((None, 2, wdt, HEAD_DIM), lambda a, b: (a, 0, 0, 0)),
                  pl.BlockSpec((None, HEAD_DIM, HEAD_DIM), lambda a, b: (a, 0, 0))],
        out_specs=pl.BlockSpec((None, None, n, HEAD_DIM), lambda a, b: (a, b, 0, 0)),
        out_shape=jax.ShapeDtypeStruct((2, g, n, HEAD_DIM), BF16),
        compiler_params=_cparams(("arbitrary", "arbitrary")),
        name="nsa_compress",
    )(xc, pe, w1, w2)


def _importance_matrix(nc_pad, nblk_pad):
    r = SLC_LEN // CMP_STRIDE
    c = CMP_LEN // CMP_STRIDE
    i = np.arange(nc_pad)[:, None]
    b = np.arange(nblk_pad)[None, :]
    lo = r * b - (c - 1)
    return jnp.asarray(((i >= lo) & (i <= lo + r + c - 2)).astype(np.float32), dtype=BF16)


def _select_blocks(imp, cur, blk):
    forced = (blk == 0) | (blk == cur) | (blk == cur - 1)
    score = jnp.where(forced, 1e30, jnp.where(blk <= cur, imp, NEG))
    sel = jnp.zeros_like(score)
    blkf = blk.astype(F32)
    for _ in range(SLC_TOPK):
        mx = jnp.max(score, axis=-1, keepdims=True)
        idx = jnp.min(jnp.where(score == mx, blkf, 1e9), axis=-1, keepdims=True)
        pick = blkf == idx
        sel = jnp.where(pick, 1.0, sel)
        score = jnp.where(pick, -3e38, score)
    return sel


def _cmp_kernel(q_ref, kc_ref, vc_ref, imat_ref, o_ref, sel_ref, *, tq, nc):
    i = pl.program_id(1)
    rows = HPG * tq
    ncp = kc_ref.shape[0]
    q = q_ref[...].reshape(rows, HEAD_DIM)
    s = _dot_nt(q, kc_ref[...])
    row = lax.broadcasted_iota(jnp.int32, (rows, ncp), 0)
    qpos = i * tq + (row & (tq - 1))
    n = lax.broadcasted_iota(jnp.int32, (rows, ncp), 1)
    mask = (n * CMP_STRIDE + CMP_LEN - 1 <= qpos) & (n < nc)
    s = jnp.where(mask, s, NEG)
    m = jnp.max(s, axis=-1, keepdims=True)
    e = jnp.where(mask, jnp.exp(s - m), 0.0)
    p = e / jnp.maximum(jnp.sum(e, axis=-1, keepdims=True), 1e-30)
    o_ref[...] = _dot(p.astype(BF16), vc_ref[...]).reshape(HPG, tq, HEAD_DIM)
    psum = p[0:tq]
    for h in range(1, HPG):
        psum = psum + p[h * tq:(h + 1) * tq]
    imp = _dot_f32ish(psum, imat_ref[...])
    blk = lax.broadcasted_iota(jnp.int32, imp.shape, 1)
    cur = (i * tq + lax.broadcasted_iota(jnp.int32, imp.shape, 0)) // SLC_LEN
    sel_ref[...] = _select_blocks(imp, cur, blk).astype(BF16)


def _cmp_select(q, kcmp, vcmp, nc):
    _, t, d = q.shape
    g, ncp, _ = kcmp.shape
    tq = min(128, t)
    imat = _importance_matrix(ncp, LANES)
    return pl.pallas_call(
        functools.partial(_cmp_kernel, tq=tq, nc=nc),
        grid=(g, t // tq),
        in_specs=[pl.BlockSpec((HPG, tq, d), lambda a, i: (a, i, 0)),
                  pl.BlockSpec((None, ncp, d), lambda a, i: (a, 0, 0)),
                  pl.BlockSpec((None, ncp, d), lambda a, i: (a, 0, 0)),
                  pl.BlockSpec((ncp, LANES), lambda a, i: (0, 0))],
        out_specs=[pl.BlockSpec((HPG, tq, d), lambda a, i: (a, i, 0)),
                   pl.BlockSpec((None, tq, LANES), lambda a, i: (a, i, 0))],
        out_shape=[jax.ShapeDtypeStruct((H_NSA, t, d), F32), jax.ShapeDtypeStruct((g, t, LANES), BF16)],
        compiler_params=_cparams(("arbitrary", "arbitrary")),
        name="nsa_cmp_select",
    )(q, kcmp, vcmp, imat)


def _merge_kernel(osb_ref, odf_ref, ocm_ref, osl_ref, owi_ref, gate_ref, lam_ref, dn_ref, x_ref, ga_ref, nrm_ref,
                  w_ref, o_ref, cat_scr, *, lam_init):
    lv = lam_ref[...]
    lam = (jnp.exp(jnp.sum(lv[0:1] * lv[1:2], axis=-1, keepdims=True))
           - jnp.exp(jnp.sum(lv[2:3] * lv[3:4], axis=-1, keepdims=True)) + lam_init)
    for h in range(H_SB):
        cat_scr[:, h * HEAD_DIM:(h + 1) * HEAD_DIM] = osb_ref[h].astype(BF16)
    dn = dn_ref[...]
    for h in range(H_DIFF):
        ob = odf_ref[2 * h] - lam * odf_ref[2 * h + 1]
        ob = _rms_rows(ob, dn) * (1.0 - lam_init)
        c0 = (H_SB + h) * HEAD_DIM
        cat_scr[:, c0:c0 + HEAD_DIM] = ob.astype(BF16)
    gates = gate_ref[...]
    for h in range(H_NSA):
        oc = (gates[:, 3 * h:3 * h + 1] * ocm_ref[h] + gates[:, 3 * h + 1:3 * h + 2] * osl_ref[h]
              + gates[:, 3 * h + 2:3 * h + 3] * owi_ref[h])
        c0 = (H_SB + H_DIFF + h) * HEAD_DIM
        cat_scr[:, c0:c0 + HEAD_DIM] = oc.astype(BF16)
    y = _dot(cat_scr[...], w_ref[...])
    o_ref[...] = x_ref[...] + ga_ref[...] * _rms_rows(y, nrm_ref[...])


def _merge(osb, odf, ocm, osl, owi, gates, lam_vecs, diff_norm, x, ga, nrm, w, layer):
    t, d = x.shape
    tm = min(256, t)
    lam_init = 0.8 - 0.6 * math.exp(-0.3 * layer)
    hspec = lambda n: pl.BlockSpec((n, tm, HEAD_DIM), lambda i: (0, i, 0))
    return pl.pallas_call(
        functools.partial(_merge_kernel, lam_init=lam_init),
        grid=(t // tm,),
        in_specs=[hspec(H_SB), hspec(2 * H_DIFF), hspec(H_NSA), hspec(H_NSA), hspec(H_NSA),
                  pl.BlockSpec((tm, LANES), lambda i: (i, 0)),
                  pl.BlockSpec(lam_vecs.shape, lambda i: (0, 0)),
                  pl.BlockSpec((1, HEAD_DIM), lambda i: (0, 0)),
                  pl.BlockSpec((tm, d), lambda i: (i, 0)),
                  _row_spec(ga, tm, 1),
                  pl.BlockSpec((1, d), lambda i: (0, 0)),
                  pl.BlockSpec(w.shape, lambda i: (0, 0))],
        out_specs=pl.BlockSpec((tm, d), lambda i: (i, 0)),
        out_shape=jax.ShapeDtypeStruct((t, d), F32),
        scratch_shapes=[pltpu.VMEM((tm, N_HEADS * HEAD_DIM), BF16)],
        compiler_params=_cparams(("arbitrary",)),
        name="merge_out_proj",
    )(osb, odf, ocm, osl, owi, gates, lam_vecs, diff_norm, x, ga, nrm, w)


def _ffn_kernel(x_ref, g2_ref, sc_ref, sh_ref, wg_ref, wu_ref, wd_ref, gf_ref, g3_ref, o_ref, h_scr, acc_scr):
    f = pl.program_id(1)

    @pl.when(f == 0)
    def _():
        y = _rms_rows(x_ref[...], g2_ref[...])
        h_scr[...] = (y * (1.0 + sc_ref[...]) + sh_ref[...]).astype(BF16)
        acc_scr[...] = jnp.zeros_like(acc_scr)

    h = h_scr[...]
    a = _dot(h, wg_ref[...])
    b = _dot(h, wu_ref[...])
    acc_scr[...] += _dot((_silu(a) * b).astype(BF16), wd_ref[...])

    @pl.when(f == pl.num_programs(1) - 1)
    def _():
        o_ref[...] = x_ref[...] + gf_ref[...] * _rms_rows(acc_scr[...], g3_ref[...])


def _ffn(x, g2, sc, sh, wgu, wd, gf, g3):
    t, d = x.shape
    dff = wd.shape[0]
    tm = min(512, t)
    tf = 512 if dff % 512 == 0 else dff
    nf = dff // tf
    return pl.pallas_call(
        _ffn_kernel,
        grid=(t // tm, nf),
        in_specs=[pl.BlockSpec((tm, d), lambda i, f: (i, 0)),
                  pl.BlockSpec((1, d), lambda i, f: (0, 0)),
                  _row_spec(sc, tm, 2), _row_spec(sh, tm, 2),
                  pl.BlockSpec((d, tf), lambda i, f: (0, f)),
                  pl.BlockSpec((d, tf), lambda i, f: (0, f + nf)),
                  pl.BlockSpec((tf, d), lambda i, f: (f, 0)),
                  _row_spec(gf, tm, 2),
                  pl.BlockSpec((1, d), lambda i, f: (0, 0))],
        out_specs=pl.BlockSpec((tm, d), lambda i, f: (i, 0)),
        out_shape=jax.ShapeDtypeStruct((t, d), F32),
        scratch_shapes=[pltpu.VMEM((tm, d), BF16), pltpu.VMEM((tm, d), F32)],
        compiler_params=_cparams(("arbitrary", "arbitrary")),
        name="ffn_swiglu",
    )(x, g2, sc, sh, wgu, wgu, wd, gf, g3)


ROWS_PER_TOK = 2 * N_KV
DEC_ROWS = 32
BLK_LANES = 256


def _dec_kernel(pt_ref, *refs, pg, nps, past, ns):
    page_refs = refs[:pg]
    (new_ref, qsb_ref, qdf_ref, qsl_ref, u_ref, w1_ref,
     osb_ref, odf_ref, sacc_ref, sm_ref, sl_ref, cab_ref,
     carry_scr, asb_scr, mdf_scr, ldf_scr, adf_scr, sm_scr, sl_scr) = refs[pg:]
    j = pl.program_id(1)
    lane = lax.broadcasted_iota(jnp.int32, (DEC_ROWS, PAGE_SIZE), 1)
    row = lax.broadcasted_iota(jnp.int32, (DEC_ROWS, PAGE_SIZE), 0)
    qpos = past + (row & (ns - 1))
    blk_lane = lax.broadcasted_iota(jnp.int32, (DEC_ROWS, BLK_LANES), 1)
    half1 = lane >= SLC_LEN

    @pl.when(j == 0)
    def _():
        carry_scr[...] = jnp.zeros_like(carry_scr)
        asb_scr[...] = jnp.zeros_like(asb_scr)
        mdf_scr[...] = jnp.full_like(mdf_scr, NEG)
        ldf_scr[...] = jnp.zeros_like(ldf_scr)
        adf_scr[...] = jnp.zeros_like(adf_scr)
        sm_scr[...] = jnp.full_like(sm_scr, NEG)
        sl_scr[...] = jnp.zeros_like(sl_scr)

    def tok_rows(ref, r):
        return ref[pl.ds(r, PAGE_SIZE, stride=ROWS_PER_TOK), :]

    def attend(ref, page, slot, masked):
        kpos = page * PAGE_SIZE + lane
        z = jnp.concatenate([_dot_nt(qsb_ref[8 * h:8 * h + 8, :], tok_rows(ref, 2 * h).astype(BF16))
                             for h in range(H_SB)], axis=0)
        l1p = jnp.log(1.0 + jnp.exp(-jnp.abs(z)))
        ls_pos = jnp.minimum(z, 0.0) - l1p
        lk = ls_pos - z
        if masked:
            valid = kpos < qpos
            lk = jnp.where(valid, lk, 0.0)
        hi = lk.astype(BF16)
        lo = (lk - hi.astype(F32)).astype(BF16)
        u = u_ref[...]
        w = jnp.exp(ls_pos + _dot(hi, u) + _dot(lo, u) + carry_scr[...])
        if masked:
            w = jnp.where(valid, w, 0.0)
        wb = w.astype(BF16)
        asb_scr[...] += jnp.concatenate(
            [_dot(wb[8 * h:8 * h + 8, :], tok_rows(ref, 2 * h + 1).astype(BF16)) for h in range(H_SB)], axis=0)
        carry_scr[...] += jnp.sum(lk, axis=-1, keepdims=True)
        r0 = 2 * H_SB
        s = jnp.concatenate([_dot_nt(qdf_ref[8 * h:8 * h + 8, :], tok_rows(ref, r0 + 2 * h).astype(BF16))
                             for h in range(H_DIFF)], axis=0)
        if masked:
            valid = kpos <= qpos
            s = jnp.where(valid, s, NEG)
        m_old = mdf_scr[...]
        m_new = jnp.maximum(m_old, jnp.max(s, axis=-1, keepdims=True))
        alpha = jnp.exp(m_old - m_new)
        e = jnp.exp(s - m_new)
        if masked:
            e = jnp.where(valid, e, 0.0)
        eb = e.astype(BF16)
        ldf_scr[...] = alpha * ldf_scr[...] + jnp.sum(e, axis=-1, keepdims=True)
        adf_scr[...] = alpha * adf_scr[...] + jnp.concatenate(
            [_dot(eb[8 * h:8 * h + 8, :], tok_rows(ref, r0 + 2 * h + 1).astype(BF16)) for h in range(H_DIFF)], axis=0)
        mdf_scr[...] = m_new
        r0 = 2 * (H_SB + H_DIFF + G_NSA)
        s = jnp.concatenate([_dot_nt(qsl_ref[16 * g:16 * g + 16, :], tok_rows(ref, r0 + 2 * g).astype(BF16))
                             for g in range(G_NSA)], axis=0)
        if masked:
            valid = kpos <= qpos
            s = jnp.where(valid, s, NEG)
        m0 = jnp.max(jnp.where(half1, NEG, s), axis=-1, keepdims=True)
        m1 = jnp.max(jnp.where(half1, s, NEG), axis=-1, keepdims=True)
        e = jnp.exp(s - jnp.where(half1, m1, m0))
        if masked:
            e = jnp.where(valid, e, 0.0)
        e0 = jnp.where(half1, 0.0, e)
        e1 = e - e0
        l0 = jnp.sum(e0, axis=-1, keepdims=True)
        l1 = jnp.sum(e1, axis=-1, keepdims=True)
        e0b, e1b = e0.astype(BF16), e1.astype(BF16)
        acc0, acc1 = [], []
        for g in range(G_NSA):
            vs = tok_rows(ref, r0 + 2 * g + 1).astype(BF16)
            both = _dot(jnp.concatenate([e0b[16 * g:16 * g + 16, :], e1b[16 * g:16 * g + 16, :]], axis=0), vs)
            acc0.append(both[0:16])
            acc1.append(both[16:32])
        sacc_ref[2 * slot] = jnp.concatenate(acc0, axis=0)
        sacc_ref[2 * slot + 1] = jnp.concatenate(acc1, axis=0)
        b0 = 2 * page
        sm_scr[...] = jnp.where(blk_lane == b0, m0, jnp.where(blk_lane == b0 + 1, m1, sm_scr[...]))
        sl_scr[...] = jnp.where(blk_lane == b0, l0, jnp.where(blk_lane == b0 + 1, l1, sl_scr[...]))

    def compress(page_list, n_slots):
        r0 = 2 * (H_SB + H_DIFF)
        chunk_stride = CMP_STRIDE * ROWS_PER_TOK
        for kv in range(2):
            acc = None
            for l in range(CMP_STRIDE):
                x = jnp.concatenate(
                    [ref[pl.ds(l * ROWS_PER_TOK + r0 + 2 * g + kv, PAGE_SIZE // CMP_STRIDE, stride=chunk_stride), :]
                     for ref in page_list for g in range(G_NSA)], axis=0)
                part = _dot(x.astype(BF16), w1_ref[kv, l])
                acc = part if acc is None else acc + part
            for i in range(n_slots):
                if i < len(page_list):
                    cab_ref[i, kv] = acc[16 * i:16 * i + 16]
                else:
                    cab_ref[i, kv] = jnp.zeros((16, 2 * HEAD_DIM), F32)

    @pl.when(j == 0)
    def _():
        attend(new_ref, nps * pg, 0, True)
        zero = jnp.zeros((DEC_ROWS, HEAD_DIM), F32)
        for i in range(2, 2 * pg):
            sacc_ref[i] = zero
        compress([new_ref], pg)

    @pl.when(j > 0)
    def _():
        for i in reversed(range(pg)):
            attend(page_refs[i], (nps - j) * pg + i, i, False)
        compress(list(page_refs), pg)

    @pl.when(j == nps)
    def _():
        osb_ref[...] = asb_scr[...]
        odf_ref[...] = adf_scr[...] / jnp.maximum(ldf_scr[...], 1e-30)
        sm_ref[...] = sm_scr[...]
        sl_ref[...] = sl_scr[...]


def _dec_main(layer, page_table, cache_rows, new_page, qsb, qdf, qsl, w1cat, past, ns):
    nb, npages = page_table.shape
    pg = 4 if npages % 4 == 0 else 1
    nps = npages // pg
    rows_per_page = PAGE_SIZE * ROWS_PER_TOK

    def page_spec(i):
        def imap(b, j, pt):
            return (layer, pt[b, (nps - jnp.maximum(j, 1)) * pg + i], 0, 0)
        return pl.BlockSpec((None, None, rows_per_page, HEAD_DIM), imap)

    qspec = pl.BlockSpec((None, DEC_ROWS, HEAD_DIM), lambda b, j, pt: (b, 0, 0))
    nblk_pad = (nps + 1) * 2 * pg
    grid_spec = pltpu.PrefetchScalarGridSpec(
        num_scalar_prefetch=1,
        grid=(nb, nps + 1),
        in_specs=[page_spec(i) for i in range(pg)] + [
            pl.BlockSpec((None, rows_per_page, HEAD_DIM), lambda b, j, pt: (b, 0, 0)),
            qspec, qspec, qspec,
            pl.BlockSpec((PAGE_SIZE, PAGE_SIZE), lambda b, j, pt: (0, 0)),
            pl.BlockSpec(w1cat.shape, lambda b, j, pt: (0, 0, 0, 0))],
        out_specs=[
            qspec, qspec,
            pl.BlockSpec((None, 2 * pg, DEC_ROWS, HEAD_DIM), lambda b, j, pt: (b, nps - j, 0, 0)),
            pl.BlockSpec((None, DEC_ROWS, BLK_LANES), lambda b, j, pt: (b, 0, 0)),
            pl.BlockSpec((None, DEC_ROWS, BLK_LANES), lambda b, j, pt: (b, 0, 0)),
            pl.BlockSpec((None, None, pg, 2, 16, 2 * HEAD_DIM), lambda b, j, pt: (b, nps - j, 0, 0, 0, 0))],
        scratch_shapes=[pltpu.VMEM((DEC_ROWS, 1), F32), pltpu.VMEM((DEC_ROWS, HEAD_DIM), F32),
                        pltpu.VMEM((DEC_ROWS, 1), F32), pltpu.VMEM((DEC_ROWS, 1), F32),
                        pltpu.VMEM((DEC_ROWS, HEAD_DIM), F32),
                        pltpu.VMEM((DEC_ROWS, BLK_LANES), F32), pltpu.VMEM((DEC_ROWS, BLK_LANES), F32)])
    out_shape = [jax.ShapeDtypeStruct((nb, DEC_ROWS, HEAD_DIM), F32),
                 jax.ShapeDtypeStruct((nb, DEC_ROWS, HEAD_DIM), F32),
                 jax.ShapeDtypeStruct((nb, nblk_pad, DEC_ROWS, HEAD_DIM), F32),
                 jax.ShapeDtypeStruct((nb, DEC_ROWS, BLK_LANES), F32),
                 jax.ShapeDtypeStruct((nb, DEC_ROWS, BLK_LANES), F32),
                 jax.ShapeDtypeStruct((nb, nps + 1, pg, 2, 16, 2 * HEAD_DIM), F32)]
    outs = pl.pallas_call(
        functools.partial(_dec_kernel, pg=pg, nps=nps, past=past, ns=ns),
        grid_spec=grid_spec,
        out_shape=out_shape,
        compiler_params=_cparams(("arbitrary", "arbitrary")),
        name="decode_paged",
    )(page_table, *([cache_rows] * pg), new_page, qsb, qdf, qsl, _suffix_matrix(PAGE_SIZE), w1cat)
    return outs, pg, nps


def _cmp_bias_kernel(pe_ref, w1_ref, o_ref):
    for kv in range(2):
        for c in range(2):
            o_ref[kv, c] = _dot(jnp.broadcast_to(pe_ref[kv, c:c + 1, :], (8, pe_ref.shape[-1])).astype(BF16),
                                w1_ref[kv, c].astype(BF16))


def _cmp_bias(pe, w1):
    return pl.pallas_call(
        _cmp_bias_kernel,
        out_shape=jax.ShapeDtypeStruct((2, 2, 8, HEAD_DIM), F32),
        compiler_params=pltpu.CompilerParams(vmem_limit_bytes=VMEM_LIMIT),
        name="cmp_bias",
    )(pe, w1)


def _dec_final_kernel(cab_ref, bias_ref, w2_ref, q_ref, imat_ref, jmat_ref, sacc_ref, sm_ref, sl_ref,
                      swin_ref, nwin_ref, ocm_ref, osl_ref, owi_ref, *, nc, past, ns, nblk_pad, wb):
    ncp = cab_ref.shape[2]
    rows16 = lax.broadcasted_iota(jnp.int32, (16, 1), 0)
    qpos16 = past + (rows16 & (ns - 1))
    for g in range(G_NSA):
        q = q_ref[16 * g:16 * g + 16, :]
        cmp_kv = []
        for kv in range(2):
            ab = cab_ref[kv, g]
            a, b = ab[:, :HEAD_DIM], ab[:, HEAD_DIM:]
            h = a + pltpu.roll(b, ncp - 1, 0) + bias_ref[kv, 0, 0:1, :] + bias_ref[kv, 1, 0:1, :]
            cmp_kv.append(_dot(_silu(h).astype(BF16), w2_ref[kv].astype(BF16)).astype(BF16))
        kc, vc = cmp_kv
        s = _dot_nt(q, kc)
        n = lax.broadcasted_iota(jnp.int32, s.shape, 1)
        mask = (n * CMP_STRIDE + CMP_LEN - 1 <= qpos16) & (n < nc)
        s = jnp.where(mask, s, NEG)
        m = jnp.max(s, axis=-1, keepdims=True)
        e = jnp.where(mask, jnp.exp(s - m), 0.0)
        p = e / jnp.maximum(jnp.sum(e, axis=-1, keepdims=True), 1e-30)
        ocm_ref[16 * g:16 * g + 16, :] = _dot(p.astype(BF16), vc)
        hi, mid, lo = _split3(p)
        jm = jmat_ref[...]
        psum = _dot(jm, hi) + _dot(jm, mid) + _dot(jm, lo)
        imp = _dot_f32ish(psum, imat_ref[...])
        blk = lax.broadcasted_iota(jnp.int32, imp.shape, 1)
        sel = _select_blocks(imp, qpos16 // SLC_LEN, blk)
        mb = sm_ref[16 * g:16 * g + 16, :]
        lb = sl_ref[16 * g:16 * g + 16, :]
        picked = sel > 0.5
        mtot = jnp.max(jnp.where(picked, mb, NEG), axis=-1, keepdims=True)
        wgt = jnp.where(picked, jnp.exp(mb - mtot), 0.0)
        den = jnp.sum(wgt * lb, axis=-1, keepdims=True)
        wgt_b = wgt.astype(BF16)
        pad = jnp.zeros((BLK_LANES - nblk_pad, HEAD_DIM), F32)
        for half in range(2):
            rsel = lax.broadcasted_iota(jnp.int32, (8, HEAD_DIM), 0)
            num = jnp.zeros((8, HEAD_DIM), F32)
            for i in range(8):
                r32 = 16 * g + 8 * half + i
                acc = jnp.concatenate([sacc_ref[pl.ds(r32, nblk_pad, stride=DEC_ROWS), :], pad], axis=0)
                ahi = acc.astype(BF16)
                alo = (acc - ahi.astype(F32)).astype(BF16)
                w8 = wgt_b[8 * half:8 * half + 8, :]
                res = _dot(w8, ahi) + _dot(w8, alo)
                num = jnp.where(rsel == i, res, num)
            r0 = 16 * g + 8 * half
            osl_ref[r0:r0 + 8, :] = num / jnp.maximum(den[8 * half:8 * half + 8], 1e-30)
        kw = swin_ref[pl.ds(g, wb, stride=2 * G_NSA), :].astype(BF16)
        vw = swin_ref[pl.ds(G_NSA + g, wb, stride=2 * G_NSA), :].astype(BF16)
        kn = nwin_ref[pl.ds(g, 8, stride=2 * G_NSA), :].astype(BF16)
        vn = nwin_ref[pl.ds(G_NSA + g, 8, stride=2 * G_NSA), :].astype(BF16)
        s1 = _dot_nt(q, kw)
        s2 = _dot_nt(q, kn)
        d1 = qpos16 - (past - wb + lax.broadcasted_iota(jnp.int32, s1.shape, 1))
        k1 = (d1 >= 0) & (d1 <= WINDOW)
        d2 = qpos16 - (past + lax.broadcasted_iota(jnp.int32, s2.shape, 1))
        k2 = (d2 >= 0) & (d2 <= WINDOW)
        s1 = jnp.where(k1, s1, NEG)
        s2 = jnp.where(k2, s2, NEG)
        m = jnp.maximum(jnp.max(s1, axis=-1, keepdims=True), jnp.max(s2, axis=-1, keepdims=True))
        e1 = jnp.where(k1, jnp.exp(s1 - m), 0.0)
        e2 = jnp.where(k2, jnp.exp(s2 - m), 0.0)
        den = jnp.sum(e1, axis=-1, keepdims=True) + jnp.sum(e2, axis=-1, keepdims=True)
        owi_ref[16 * g:16 * g + 16, :] = (_dot(e1.astype(BF16), vw) + _dot(e2.astype(BF16), vn)) / jnp.maximum(den, 1e-30)


def _dec_final(layer, cab, bias, w2, qsl, sacc, sm, sl, state_rows, new_win, nc, past, ns):
    nb, _, _, ncp, _ = cab.shape
    nblk_pad = sacc.shape[1] // DEC_ROWS
    wb = state_rows.shape[2] // (2 * G_NSA)
    imat = _importance_matrix(ncp, BLK_LANES)
    t_of = np.arange(16) % ns
    jmat = jnp.asarray((t_of[:, None] == t_of[None, :]).astype(np.float32), dtype=BF16)
    ospec = pl.BlockSpec((None, DEC_ROWS, HEAD_DIM), lambda b: (b, 0, 0))
    return pl.pallas_call(
        functools.partial(_dec_final_kernel, nc=nc, past=past, ns=ns, nblk_pad=nblk_pad, wb=wb),
        grid=(nb,),
        in_specs=[pl.BlockSpec((None, 2, G_NSA, ncp, 2 * HEAD_DIM), lambda b: (b, 0, 0, 0, 0)),
                  pl.BlockSpec(bias.shape, lambda b: (0, 0, 0, 0)),
                  pl.BlockSpec(w2.shape, lambda b: (0, 0, 0)),
                  ospec,
                  pl.BlockSpec(imat.shape, lambda b: (0, 0)),
                  pl.BlockSpec(jmat.shape, lambda b: (0, 0)),
                  pl.BlockSpec((None, nblk_pad * DEC_ROWS, HEAD_DIM), lambda b: (b, 0, 0)),
                  pl.BlockSpec((None, DEC_ROWS, BLK_LANES), lambda b: (b, 0, 0)),
                  pl.BlockSpec((None, DEC_ROWS, BLK_LANES), lambda b: (b, 0, 0)),
                  pl.BlockSpec((None, None, state_rows.shape[2], HEAD_DIM), lambda b: (layer, b, 0, 0)),
                  pl.BlockSpec((None, new_win.shape[1], HEAD_DIM), lambda b: (b, 0, 0))],
        out_specs=[ospec, ospec, ospec],
        out_shape=[jax.ShapeDtypeStruct((nb, DEC_ROWS, HEAD_DIM), F32)] * 3,
        compiler_params=_cparams(("arbitrary",)),
        name="decode_nsa_finish",
    )(cab, bias, w2, qsl, imat, jmat, sacc, sm, sl, state_rows, new_win)


def _sample_mixer(x, mod, lw, layer, tabs, page_table, cache_rows, state_rows, nb, ns, past):
    sh_a, sc_a, g_a = mod[0], mod[1], mod[2]
    proj = _proj(x, lw["n0"], sc_a, sh_a, lw["w_in"])
    (qsb, _, _, qdf, _, _, qns, _, _, _, _, _, _, gates, kv_rows, win_rows) = _postproj(proj, tabs)

    def per_seq(a, heads, rows_per_head):
        a = a.reshape(heads, nb, ns, HEAD_DIM).transpose(1, 0, 2, 3)
        a = jnp.pad(a, ((0, 0), (0, 0), (0, rows_per_head - ns), (0, 0)))
        return a.reshape(nb, heads * rows_per_head, HEAD_DIM)

    qsb_d = per_seq(qsb, H_SB, 8)
    qdf_d = per_seq(qdf, 2 * H_DIFF, ns)
    qsl_d = per_seq(qns, H_NSA, ns)
    kv_tok = kv_rows.reshape(nb, ns, 2, N_KV, HEAD_DIM).transpose(0, 1, 3, 2, 4).reshape(nb, ns * ROWS_PER_TOK, HEAD_DIM)
    new_page = jnp.pad(kv_tok, ((0, 0), (0, (PAGE_SIZE - ns) * ROWS_PER_TOK), (0, 0)))
    w1 = lw["cmp_w1"]
    w1cat = jnp.concatenate([w1[:, 0], w1[:, 1]], axis=-1).reshape(2, CMP_STRIDE, HEAD_DIM, 2 * HEAD_DIM).astype(BF16)
    (o_sb, o_df, sacc, sm, sl, cab), pg, nps = _dec_main(layer, page_table, cache_rows, new_page, qsb_d, qdf_d, qsl_d,
                                                        w1cat, past, ns)
    nchunks = (nps + 1) * pg * (PAGE_SIZE // CMP_STRIDE)
    cab = cab.reshape(nb, nps + 1, pg, 2, G_NSA, PAGE_SIZE // CMP_STRIDE, 2 * HEAD_DIM)
    cab = cab.transpose(0, 3, 4, 1, 2, 5, 6).reshape(nb, 2, G_NSA, nchunks, 2 * HEAD_DIM)
    bias = _cmp_bias(lw["cmp_pe"], w1)
    new_win = jnp.pad(win_rows.reshape(nb, ns * 2 * G_NSA, HEAD_DIM), ((0, 0), (0, (8 - ns) * 2 * G_NSA), (0, 0)))
    padded = -(-(past + ns) // SLC_LEN) * SLC_LEN
    nc = padded // CMP_STRIDE - CMP_LEN // CMP_STRIDE + 1
    o_cmp, o_slc, o_win = _dec_final(layer, cab, bias, lw["cmp_w2"], qsl_d,
                                     sacc.reshape(nb, -1, HEAD_DIM), sm, sl, state_rows, new_win, nc, past, ns)

    def head_major(a, heads, rows_per_head):
        a = a.reshape(nb, heads, rows_per_head, HEAD_DIM)[:, :, :ns]
        return a.transpose(1, 0, 2, 3).reshape(heads, nb * ns, HEAD_DIM)

    x1 = _merge(head_major(o_sb, H_SB, 8), head_major(o_df, 2 * H_DIFF, ns), head_major(o_cmp, H_NSA, ns),
                head_major(o_slc, H_NSA, ns), head_major(o_win, H_NSA, ns), gates, lw["diff_lambda"],
                lw["diff_norm"], x, g_a, lw["n1"], lw["w_out"], layer)
    return x1, kv_rows, win_rows


def _split_mod(mod):
    return [mod[:, i * (mod.shape[1] // 6):(i + 1) * (mod.shape[1] // 6)] for i in range(6)]


def _prompt_mixer(x, mod, lw, layer, tabs):
    t = x.shape[0]
    sh_a, sc_a, g_a = mod[0], mod[1], mod[2]
    proj = _proj(x, lw["n0"], sc_a, sh_a, lw["w_in"])
    (qsb, ksb, vsb, qdf, kdf, vdf, qns, kc, vc, ks, vs, kw, vw, gates, kv_rows, win_rows) = _postproj(proj, tabs)
    o_sb = _sb_attention(qsb, ksb, vsb)
    o_df = _flash(qdf, kdf, vdf, "causal")
    nch = t // CMP_STRIDE
    xc = jnp.stack([kc, vc]).reshape(2, G_NSA, nch, CMP_STRIDE * HEAD_DIM)
    cmp_kv = _compress(xc, lw["cmp_pe"], lw["cmp_w1"], lw["cmp_w2"])
    o_cmp, sel = _cmp_select(qns, cmp_kv[0], cmp_kv[1], nch - 1)
    o_slc = _flash(qns, ks, vs, "select", sel)
    o_win = _flash(qns, kw, vw, "window")
    x1 = _merge(o_sb, o_df, o_cmp, o_slc, o_win, gates, lw["diff_lambda"], lw["diff_norm"], x, g_a, lw["n1"],
                lw["w_out"], layer)
    return x1, kv_rows, win_rows


def _layer_weights(layer, norms, w_in_p, w_out_b, diff_lambda, diff_norm, cmp_pe, cmp_w1, cmp_w2, wgu_b, wd_b):
    d = norms.shape[-1]
    return dict(
        n0=norms[layer, 0].reshape(1, d), n1=norms[layer, 1].reshape(1, d),
        n2=norms[layer, 2].reshape(1, d), n3=norms[layer, 3].reshape(1, d),
        w_in=w_in_p[layer], w_out=w_out_b[layer],
        diff_lambda=diff_lambda[layer], diff_norm=diff_norm[layer].reshape(1, HEAD_DIM),
        cmp_pe=cmp_pe[layer].reshape(2, 2, CMP_STRIDE * HEAD_DIM),
        cmp_w1=cmp_w1[layer].reshape(2, 2, CMP_STRIDE * HEAD_DIM, HEAD_DIM),
        cmp_w2=cmp_w2[layer], w_gate_up=wgu_b[layer], w_down=wd_b[layer])


def kernel(x_prompt, x_sample, cache_kv, state_win_kv, page_table, c_prompt, c_sample, ada_w, ada_b, norms, w_in,
           w_out, diff_lambda, diff_norm, cmp_pe, cmp_w1, cmp_w2, w_gate_up, w_down):
    depth = ada_w.shape[0]
    _, seq, d = x_prompt.shape
    nb, ns, _ = x_sample.shape
    past = page_table.shape[1] * PAGE_SIZE

    w_in_p = jnp.pad(w_in, ((0, 0), (0, 0), (0, N_IN_PAD - N_IN))).astype(BF16)
    w_out_b = w_out.astype(BF16)
    wgu_b = w_gate_up.astype(BF16)
    wd_b = w_down.astype(BF16)

    n_c = 1 + nb
    rows = -(-n_c // 8) * 8
    c_all = jnp.pad(jnp.concatenate([c_prompt, c_sample], axis=0), ((0, rows - n_c), (0, 0)))
    mod_all = _ada(c_all, ada_w, ada_b)

    tabs_p = _rope_tables(jnp.arange(seq))
    tabs_s = _rope_tables(past + jnp.arange(nb * ns) % ns)
    cache_rows = cache_kv.transpose(0, 1, 2, 4, 3, 5).reshape(depth, cache_kv.shape[1], PAGE_SIZE * ROWS_PER_TOK,
                                                               HEAD_DIM)
    wb_s = state_win_kv.shape[2]
    state_rows = state_win_kv.reshape(depth, nb, wb_s * 2 * G_NSA, HEAD_DIM)
    xp = x_prompt.reshape(seq, d)
    xs = x_sample.reshape(nb * ns, d)
    kv_p, win_p, kv_s, win_s = [], [], [], []
    for layer in range(depth):
        lw = _layer_weights(layer, norms, w_in_p, w_out_b, diff_lambda, diff_norm, cmp_pe, cmp_w1, cmp_w2, wgu_b, wd_b)
        mod_p = _split_mod(mod_all[layer, 0:1])
        x1, kv_rows, win_rows = _prompt_mixer(xp, mod_p, lw, layer, tabs_p)
        xp = _ffn(x1, lw["n2"], mod_p[4], mod_p[3], lw["w_gate_up"], lw["w_down"], mod_p[5], lw["n3"])
        kv_p.append(kv_rows.reshape(1, seq, 2, N_KV, HEAD_DIM))
        wb = min(WINDOW, seq)
        win_p.append(win_rows[seq - wb:].reshape(1, wb, 2, G_NSA, HEAD_DIM))

        mod_s = _split_mod(jnp.repeat(mod_all[layer, 1:1 + nb], ns, axis=0))
        x1, kv_rows, win_rows = _sample_mixer(xs, mod_s, lw, layer, tabs_s, page_table, cache_rows, state_rows,
                                              nb, ns, past)
        xs = _ffn(x1, lw["n2"], mod_s[4], mod_s[3], lw["w_gate_up"], lw["w_down"], mod_s[5], lw["n3"])
        kv_s.append(kv_rows.reshape(nb, ns, 2, N_KV, HEAD_DIM))
        win_new = win_rows.reshape(nb, ns, 2, G_NSA, HEAD_DIM)
        win_s.append(jnp.concatenate([state_win_kv[layer], win_new], axis=1)[:, ns:])
    return (xp.reshape(1, seq, d), xs.reshape(nb, ns, d), jnp.stack(kv_p), jnp.stack(kv_s), jnp.stack(win_p),
            jnp.stack(win_s))
```

```python
import functools
import math

import numpy as np
import jax
import jax.numpy as jnp
from jax import lax
from jax.experimental import pallas as pl
from jax.experimental.pallas import tpu as pltpu

F32 = jnp.float32
BF16 = jnp.bfloat16

HEAD_DIM = 128
H_SB = 4
H_DIFF = 4
H_NSA = 8
G_NSA = 2
HPG = H_NSA // G_NSA
N_KV = H_SB + H_DIFF + 2 * G_NSA
N_HEADS = H_SB + H_DIFF + H_NSA
DIFF_DIM = HEAD_DIM // 2
CMP_LEN = 32
CMP_STRIDE = 16
SLC_LEN = 64
SLC_TOPK = 16
WINDOW = 512
PAGE_SIZE = 128
ROPE_THETA = 10000.0
EPS = 1e-6
NEG = -1e30
LANES = 128
VMEM_LIMIT = 56 * 1024 * 1024

C_QA, C_KA, C_VA = 0, 512, 1024
C_QB, C_KB, C_VB = 1536, 2048, 2560
C_QC = 3072
C_KC, C_VC, C_KS, C_VS, C_KW, C_VW = 4096, 4352, 4608, 4864, 5120, 5376
C_GATE = 5632
N_IN = 5656
N_IN_PAD = 5760


def _cparams(sem):
    return pltpu.CompilerParams(dimension_semantics=sem, vmem_limit_bytes=VMEM_LIMIT)


def _silu(x):
    return x * jax.nn.sigmoid(x)


def _rms_rows(x, g):
    return x * lax.rsqrt(jnp.mean(x * x, axis=-1, keepdims=True) + EPS) * g


def _dot(a, b):
    return jnp.dot(a, b, preferred_element_type=F32)


def _dot_nt(a, b):
    return lax.dot_general(a, b, (((1,), (1,)), ((), ())), preferred_element_type=F32)


def _split3(x):
    hi = x.astype(BF16)
    r1 = x - hi.astype(F32)
    mid = r1.astype(BF16)
    lo = (r1 - mid.astype(F32)).astype(BF16)
    return hi, mid, lo


def _dot_f32ish(x, m01):
    hi, mid, lo = _split3(x)
    return _dot(hi, m01) + _dot(mid, m01) + _dot(lo, m01)


def _ada_kernel(c_ref, w_ref, b_ref, o_ref):
    a = _silu(c_ref[...]).astype(BF16)
    o_ref[...] = _dot(a, w_ref[...].astype(BF16)) + b_ref[...]


def _ada(c_all, ada_w, ada_b):
    depth, d, n = ada_w.shape
    rows = c_all.shape[0]
    tn = 1024
    return pl.pallas_call(
        _ada_kernel,
        grid=(depth, n // tn),
        in_specs=[pl.BlockSpec((rows, d), lambda l, j: (0, 0)),
                  pl.BlockSpec((None, d, tn), lambda l, j: (l, 0, j)),
                  pl.BlockSpec((None, 1, tn), lambda l, j: (l, 0, j))],
        out_specs=pl.BlockSpec((None, rows, tn), lambda l, j: (l, 0, j)),
        out_shape=jax.ShapeDtypeStruct((depth, rows, n), F32),
        compiler_params=_cparams(("arbitrary", "arbitrary")),
        name="ada_mod",
    )(c_all, ada_w, ada_b.reshape(depth, 1, n))


def _row_spec(arr, tm, grid_rank):
    c = arr.shape[1]
    if arr.shape[0] == 1:
        if grid_rank == 1:
            return pl.BlockSpec((1, c), lambda i: (0, 0))
        return pl.BlockSpec((1, c), lambda i, j: (0, 0))
    if grid_rank == 1:
        return pl.BlockSpec((tm, c), lambda i: (i, 0))
    return pl.BlockSpec((tm, c), lambda i, j: (i, 0))


def _proj_kernel(x_ref, g_ref, sc_ref, sh_ref, w_ref, o_ref, h_scr):
    @pl.when(pl.program_id(1) == 0)
    def _():
        y = _rms_rows(x_ref[...], g_ref[...])
        h_scr[...] = (y * (1.0 + sc_ref[...]) + sh_ref[...]).astype(BF16)

    o_ref[...] = _dot(h_scr[...], w_ref[...])


def _proj(x, g, sc, sh, w):
    t, d = x.shape
    n = w.shape[1]
    tm = min(512, t)
    tn = 1920
    return pl.pallas_call(
        _proj_kernel,
        grid=(t // tm, n // tn),
        in_specs=[pl.BlockSpec((tm, d), lambda i, j: (i, 0)),
                  pl.BlockSpec((1, d), lambda i, j: (0, 0)),
                  _row_spec(sc, tm, 2), _row_spec(sh, tm, 2),
                  pl.BlockSpec((d, tn), lambda i, j: (0, j))],
        out_specs=pl.BlockSpec((tm, tn), lambda i, j: (i, j)),
        out_shape=jax.ShapeDtypeStruct((t, n), F32),
        scratch_shapes=[pltpu.VMEM((tm, d), BF16)],
        compiler_params=_cparams(("arbitrary", "arbitrary")),
        name="in_proj",
    )(x, g, sc, sh, w)


def _postproj_kernel(p_ref, ca_ref, sa_ref, cb_ref, sb_ref,
                     qsb_ref, ksb_ref, vsb_ref, qdf_ref, kdf_ref, vdf_ref, qns_ref,
                     kc_ref, vc_ref, ks_ref, vs_ref, kw_ref, vw_ref, gate_ref, kv_ref, win_ref):
    ca, sa, cb, sb = ca_ref[...], sa_ref[...], cb_ref[...], sb_ref[...]
    lane = lax.broadcasted_iota(jnp.int32, ca.shape, 1)
    low_half = (lane % DIFF_DIM) < (DIFF_DIM // 2)
    first_chunk = lane < DIFF_DIM
    sm_scale = HEAD_DIM ** -0.5
    df_scale = DIFF_DIM ** -0.5

    def col(c0, h):
        return p_ref[:, c0 + h * HEAD_DIM:c0 + (h + 1) * HEAD_DIM]

    def rope_a(x):
        return x * ca + pltpu.roll(x, HEAD_DIM // 2, 1) * sa

    def rope_b(x):
        partner = jnp.where(low_half, pltpu.roll(x, HEAD_DIM - DIFF_DIM // 2, 1), pltpu.roll(x, DIFF_DIM // 2, 1))
        return x * cb + partner * sb

    def put_kv(is_v, head, val):
        c0 = (N_KV * is_v + head) * HEAD_DIM
        kv_ref[:, c0:c0 + HEAD_DIM] = val

    for h in range(H_SB):
        qsb_ref[h] = (col(C_QA, h) * sm_scale).astype(BF16)
        k, v = col(C_KA, h), col(C_VA, h)
        ksb_ref[h] = k.astype(BF16)
        vsb_ref[h] = v.astype(BF16)
        put_kv(0, h, k)
        put_kv(1, h, v)
    for h in range(H_DIFF):
        q = rope_b(col(C_QB, h)) * df_scale
        qdf_ref[2 * h] = jnp.where(first_chunk, q, 0.0).astype(BF16)
        qdf_ref[2 * h + 1] = jnp.where(first_chunk, 0.0, q).astype(BF16)
        k, v = rope_b(col(C_KB, h)), col(C_VB, h)
        kdf_ref[h] = k.astype(BF16)
        vdf_ref[h] = v.astype(BF16)
        put_kv(0, H_SB + h, k)
        put_kv(1, H_SB + h, v)
    for h in range(H_NSA):
        qns_ref[h] = (rope_a(col(C_QC, h)) * sm_scale).astype(BF16)
    for g in range(G_NSA):
        k, v = rope_a(col(C_KC, g)), col(C_VC, g)
        kc_ref[g] = k
        vc_ref[g] = v
        put_kv(0, H_SB + H_DIFF + g, k)
        put_kv(1, H_SB + H_DIFF + g, v)
        k, v = rope_a(col(C_KS, g)), col(C_VS, g)
        ks_ref[g] = k.astype(BF16)
        vs_ref[g] = v.astype(BF16)
        put_kv(0, H_SB + H_DIFF + G_NSA + g, k)
        put_kv(1, H_SB + H_DIFF + G_NSA + g, v)
        k, v = rope_a(col(C_KW, g)), col(C_VW, g)
        kw_ref[g] = k.astype(BF16)
        vw_ref[g] = v.astype(BF16)
        win_ref[:, g * HEAD_DIM:(g + 1) * HEAD_DIM] = k
        win_ref[:, (G_NSA + g) * HEAD_DIM:(G_NSA + g + 1) * HEAD_DIM] = v
    gate_ref[...] = jax.nn.sigmoid(p_ref[:, C_GATE:C_GATE + LANES])


def _postproj(proj, tabs):
    t = proj.shape[0]
    tm = min(256, t)
    hm = lambda n, dt: jax.ShapeDtypeStruct((n, t, HEAD_DIM), dt)
    hspec = lambda n: pl.BlockSpec((n, tm, HEAD_DIM), lambda i: (0, i, 0))
    tab_spec = pl.BlockSpec((tm, LANES), lambda i: (i, 0))
    out_shapes = [hm(H_SB, BF16), hm(H_SB, BF16), hm(H_SB, BF16),
                  hm(2 * H_DIFF, BF16), hm(H_DIFF, BF16), hm(H_DIFF, BF16), hm(H_NSA, BF16),
                  hm(G_NSA, F32), hm(G_NSA, F32), hm(G_NSA, BF16), hm(G_NSA, BF16),
                  hm(G_NSA, BF16), hm(G_NSA, BF16),
                  jax.ShapeDtypeStruct((t, LANES), F32),
                  jax.ShapeDtypeStruct((t, 2 * N_KV * HEAD_DIM), F32),
                  jax.ShapeDtypeStruct((t, 2 * G_NSA * HEAD_DIM), F32)]
    out_specs = [hspec(H_SB), hspec(H_SB), hspec(H_SB), hspec(2 * H_DIFF), hspec(H_DIFF), hspec(H_DIFF),
                 hspec(H_NSA), hspec(G_NSA), hspec(G_NSA), hspec(G_NSA), hspec(G_NSA), hspec(G_NSA), hspec(G_NSA),
                 pl.BlockSpec((tm, LANES), lambda i: (i, 0)),
                 pl.BlockSpec((tm, 2 * N_KV * HEAD_DIM), lambda i: (i, 0)),
                 pl.BlockSpec((tm, 2 * G_NSA * HEAD_DIM), lambda i: (i, 0))]
    return pl.pallas_call(
        _postproj_kernel,
        grid=(t // tm,),
        in_specs=[pl.BlockSpec((tm, N_IN_PAD), lambda i: (i, 0)), tab_spec, tab_spec, tab_spec, tab_spec],
        out_specs=out_specs,
        out_shape=out_shapes,
        compiler_params=_cparams(("arbitrary",)),
        name="rope_split",
    )(proj, *tabs)


def _rope_tables(pos):
    posf = pos.astype(F32)[:, None]

    def tab(half, reps):
        inv = jnp.power(ROPE_THETA, -jnp.arange(half, dtype=F32) / half)
        ang = posf * inv[None, :]
        c, s = jnp.cos(ang), jnp.sin(ang)
        return jnp.tile(jnp.concatenate([c, c], 1), (1, reps)), jnp.tile(jnp.concatenate([-s, s], 1), (1, reps))

    ca, sa = tab(HEAD_DIM // 2, 1)
    cb, sb = tab(DIFF_DIM // 2, 2)
    return ca, sa, cb, sb


def _pairs(nq, lo_fn, hi_fn, masked_fn, reverse):
    qi, kj, first, last, msk = [], [], [], [], []
    for i in range(nq):
        ks = list(range(lo_fn(i), hi_fn(i) + 1))
        if reverse:
            ks = ks[::-1]
        for n, k in enumerate(ks):
            qi.append(i)
            kj.append(k)
            first.append(int(n == 0))
            last.append(int(n == len(ks) - 1))
            msk.append(int(masked_fn(i, k)))
    mk = lambda a: jnp.asarray(np.asarray(a, dtype=np.int32))
    return mk(qi), mk(kj), mk(first), mk(last), mk(msk)


SB_UNDERFLOW = -104.0


def _sb_kernel(q_ref, k_ref, v_ref, u_ref, o_ref, carry_scr, acc_scr, *, tq, tk):
    qi = pl.program_id(1)
    carry_scr[...] = jnp.zeros_like(carry_scr)
    acc_scr[...] = jnp.zeros_like(acc_scr)
    q = q_ref[...]
    u = u_ref[...]
    row = lax.broadcasted_iota(jnp.int32, (tq, tk), 0)
    col = lax.broadcasted_iota(jnp.int32, (tq, tk), 1)

    def body(state):
        kj, _ = state
        off = pl.multiple_of(kj * tk, tk)
        z = _dot_nt(q, k_ref[pl.ds(off, tk), :])
        mask = (off + col) < (qi * tq + row)
        l1p = jnp.log(1.0 + jnp.exp(-jnp.abs(z)))
        ls_pos = jnp.minimum(z, 0.0) - l1p
        lk = jnp.where(mask, ls_pos - z, 0.0)
        hi = lk.astype(BF16)
        lo = (lk - hi.astype(F32)).astype(BF16)
        after = _dot(hi, u) + _dot(lo, u) + carry_scr[...]
        w = jnp.where(mask, jnp.exp(ls_pos + after), 0.0)
        acc_scr[...] += _dot(w.astype(BF16), v_ref[pl.ds(off, tk), :])
        carry = carry_scr[...] + jnp.sum(lk, axis=-1, keepdims=True)
        carry_scr[...] = carry
        go = (jnp.max(carry) > SB_UNDERFLOW).astype(jnp.int32)
        return kj - 1, go

    last_tile = (qi + 1) * (tq // tk) - 1
    lax.while_loop(lambda st: (st[0] >= 0) & (st[1] > 0), body, (last_tile, jnp.int32(1)))
    o_ref[...] = acc_scr[...]


def _suffix_matrix(n):
    j = np.arange(n)[:, None]
    s = np.arange(n)[None, :]
    return jnp.asarray((j > s).astype(np.float32), dtype=BF16)


def _sb_attention(q, k, v):
    h, t, d = q.shape
    tq = min(256, t)
    tk = min(128, t)
    return pl.pallas_call(
        functools.partial(_sb_kernel, tq=tq, tk=tk),
        grid=(h, t // tq),
        in_specs=[pl.BlockSpec((None, tq, d), lambda hh, i: (hh, i, 0)),
                  pl.BlockSpec((None, t, d), lambda hh, i: (hh, 0, 0)),
                  pl.BlockSpec((None, t, d), lambda hh, i: (hh, 0, 0)),
                  pl.BlockSpec((tk, tk), lambda hh, i: (0, 0))],
        out_specs=pl.BlockSpec((None, tq, d), lambda hh, i: (hh, i, 0)),
        out_shape=jax.ShapeDtypeStruct((h, t, d), F32),
        scratch_shapes=[pltpu.VMEM((tq, 1), F32), pltpu.VMEM((tq, d), F32)],
        compiler_params=_cparams(("arbitrary", "arbitrary")),
        name="sb_attn",
    )(q, k, v, _suffix_matrix(tk))


SEL_BIAS = 16384.0


def _flash_kernel(qi_ref, kj_ref, first_ref, last_ref, msk_ref, q_ref, k_ref, v_ref, *rest, mode, rep, tq, tk):
    if mode == "select":
        bias_ref, et_ref, o_ref, m_scr, acc_scr = rest
    else:
        o_ref, m_scr, acc_scr = rest
    p = pl.program_id(1)
    rows = rep * tq

    @pl.when(first_ref[p] == 1)
    def _():
        m_scr[...] = jnp.full_like(m_scr, NEG)
        acc_scr[...] = jnp.zeros_like(acc_scr)

    q = q_ref[...].reshape(rows, HEAD_DIM)
    k = k_ref[...]
    if mode == "select":
        q = jnp.concatenate([q, jnp.concatenate([bias_ref[...]] * rep, axis=0)], axis=1)
        k = jnp.concatenate([k, et_ref[...]], axis=1)
    s = _dot_nt(q, k)
    v_ext = jnp.concatenate([v_ref[...], jnp.ones((tk, HEAD_DIM), BF16)], axis=1)

    def update(s):
        m_old = m_scr[...]
        m_new = jnp.maximum(m_old, jnp.max(s, axis=-1, keepdims=True))
        e = jnp.exp(s - m_new)
        acc_scr[...] = jnp.exp(m_old - m_new) * acc_scr[...] + _dot(e.astype(BF16), v_ext)
        m_scr[...] = m_new

    @pl.when(msk_ref[p] == 1)
    def _():
        row = lax.broadcasted_iota(jnp.int32, (rows, tk), 0)
        dist = (qi_ref[p] * tq + (row & (tq - 1))) - (kj_ref[p] * tk + lax.broadcasted_iota(jnp.int32, (rows, tk), 1))
        if mode == "window":
            mask = (dist >= 0) & (dist <= WINDOW)
        else:
            mask = dist >= 0
        update(jnp.where(mask, s, NEG))

    @pl.when(msk_ref[p] == 0)
    def _():
        update(s)

    @pl.when(last_ref[p] == 1)
    def _():
        acc = acc_scr[...]
        out = acc[:, :HEAD_DIM] / jnp.maximum(acc[:, HEAD_DIM:], 1e-30)
        o_ref[...] = out.reshape(rep, tq, HEAD_DIM)


def _block_onehot(t):
    key = np.arange(t)[:, None]
    b = np.arange(LANES)[None, :]
    return jnp.asarray((key // SLC_LEN == b).astype(np.float32), dtype=BF16)


def _flash(q, k, v, mode, sel_bias=None):
    hq, t, d = q.shape
    hk = k.shape[0]
    rep = hq // hk
    tq = min(512 // rep, t)
    tk = min(512, t)
    hi = lambda i: ((i + 1) * tq - 1) // tk
    if mode == "window":
        lo = lambda i: max(0, (i * tq - WINDOW) // tk)
        masked = lambda i, j: not (i * tq - ((j + 1) * tk - 1) >= 0 and (i + 1) * tq - 1 - j * tk <= WINDOW)
        reverse = True
    else:
        lo = lambda i: 0
        masked = lambda i, j: (j + 1) * tk - 1 > i * tq
        reverse = False
    sched = _pairs(t // tq, lo, hi, masked, reverse)
    npairs = sched[0].shape[0]
    in_specs = [pl.BlockSpec((rep, tq, d), lambda g, p, qi, kj, f, l, m: (g, qi[p], 0)),
                pl.BlockSpec((None, tk, d), lambda g, p, qi, kj, f, l, m: (g, kj[p], 0)),
                pl.BlockSpec((None, tk, d), lambda g, p, qi, kj, f, l, m: (g, kj[p], 0))]
    args = [q, k, v]
    if mode == "select":
        in_specs += [pl.BlockSpec((None, tq, LANES), lambda g, p, qi, kj, f, l, m: (g, qi[p], 0)),
                     pl.BlockSpec((tk, LANES), lambda g, p, qi, kj, f, l, m: (kj[p], 0))]
        args += [sel_bias, _block_onehot(t)]
    grid_spec = pltpu.PrefetchScalarGridSpec(
        num_scalar_prefetch=5,
        grid=(hk, npairs),
        in_specs=in_specs,
        out_specs=pl.BlockSpec((rep, tq, d), lambda g, p, qi, kj, f, l, m: (g, qi[p], 0)),
        scratch_shapes=[pltpu.VMEM((rep * tq, 1), F32), pltpu.VMEM((rep * tq, 2 * d), F32)])
    return pl.pallas_call(
        functools.partial(_flash_kernel, mode=mode, rep=rep, tq=tq, tk=tk),
        grid_spec=grid_spec,
        out_shape=jax.ShapeDtypeStruct((hq, t, d), F32),
        compiler_params=_cparams(("arbitrary", "arbitrary")),
        name="attn_" + mode,
    )(*sched, *args)


def _compress_kernel(x_ref, pe_ref, w1_ref, w2_ref, o_ref):
    x = x_ref[...]
    n = x.shape[0]
    a = _dot((x + pe_ref[0:1, :]).astype(BF16), w1_ref[0].astype(BF16))
    b = _dot((x + pe_ref[1:2, :]).astype(BF16), w1_ref[1].astype(BF16))
    h = a + pltpu.roll(b, n - 1, 0)
    o_ref[...] = _dot(_silu(h).astype(BF16), w2_ref[...].astype(BF16)).astype(o_ref.dtype)


def _compress(xc, pe, w1, w2):
    _, g, n, wdt = xc.shape
    return pl.pallas_call(
        _compress_kernel,
        grid=(2, g),
        in_specs=[pl.BlockSpec((None, None, n, wdt), lambda a, b: (a, b, 0, 0)),
                  pl.BlockSpec((None, 2, wdt), lambda a, b: (a, 0, 0)),
                  pl.BlockSpec((None, 2, wdt, HEAD_DIM), lambda a, b: (a, 0, 0, 0)),
                  pl.BlockSpec((None, HEAD_DIM, HEAD_DIM), lambda a, b: (a, 0, 0))],
        out_specs=pl.BlockSpec((None, None, n, HEAD_DIM), lambda a, b: (a, b, 0, 0)),
        out_shape=jax.ShapeDtypeStruct((2, g, n, HEAD_DIM), BF16),
        compiler_params=_cparams(("arbitrary", "arbitrary")),
        name="nsa_compress",
    )(xc, pe, w1, w2)


def _importance_matrix(nc_pad, nblk_pad):
    r = SLC_LEN // CMP_STRIDE
    c = CMP_LEN // CMP_STRIDE
    i = np.arange(nc_pad)[:, None]
    b = np.arange(nblk_pad)[None, :]
    lo = r * b - (c - 1)
    return jnp.asarray(((i >= lo) & (i <= lo + r + c - 2)).astype(np.float32), dtype=BF16)


def _select_blocks(imp, cur, blk):
    forced = (blk == 0) | (blk == cur) | (blk == cur - 1)
    score = jnp.where(forced, 1e30, jnp.where(blk <= cur, imp, NEG))
    sel = jnp.zeros_like(score)
    blkf = blk.astype(F32)
    for _ in range(SLC_TOPK):
        mx = jnp.max(score, axis=-1, keepdims=True)
        idx = jnp.min(jnp.where(score == mx, blkf, 1e9), axis=-1, keepdims=True)
        pick = blkf == idx
        sel = jnp.where(pick, 1.0, sel)
        score = jnp.where(pick, -3e38, score)
    return sel


def _cmp_kernel(q_ref, kc_ref, vc_ref, imat_ref, o_ref, sel_ref, *, tq, nc):
    i = pl.program_id(1)
    rows = HPG * tq
    ncp = kc_ref.shape[0]
    q = q_ref[...].reshape(rows, HEAD_DIM)
    s = _dot_nt(q, kc_ref[...])
    row = lax.broadcasted_iota(jnp.int32, (rows, ncp), 0)
    qpos = i * tq + (row & (tq - 1))
    n = lax.broadcasted_iota(jnp.int32, (rows, ncp), 1)
    mask = (n * CMP_STRIDE + CMP_LEN - 1 <= qpos) & (n < nc)
    s = jnp.where(mask, s, NEG)
    m = jnp.max(s, axis=-1, keepdims=True)
    e = jnp.where(mask, jnp.exp(s - m), 0.0)
    p = e / jnp.maximum(jnp.sum(e, axis=-1, keepdims=True), 1e-30)
    o_ref[...] = _dot(p.astype(BF16), vc_ref[...]).reshape(HPG, tq, HEAD_DIM)
    psum = p[0:tq]
    for h in range(1, HPG):
        psum = psum + p[h * tq:(h + 1) * tq]
    imp = _dot_f32ish(psum, imat_ref[...])
    blk = lax.broadcasted_iota(jnp.int32, imp.shape, 1)
    cur = (i * tq + lax.broadcasted_iota(jnp.int32, imp.shape, 0)) // SLC_LEN
    sel_ref[...] = ((_select_blocks(imp, cur, blk) - 1.0) * SEL_BIAS).astype(BF16)


def _cmp_select(q, kcmp, vcmp, nc):
    _, t, d = q.shape
    g, ncp, _ = kcmp.shape
    tq = min(128, t)
    imat = _importance_matrix(ncp, LANES)
    return pl.pallas_call(
        functools.partial(_cmp_kernel, tq=tq, nc=nc),
        grid=(g, t // tq),
        in_specs=[pl.BlockSpec((HPG, tq, d), lambda a, i: (a, i, 0)),
                  pl.BlockSpec((None, ncp, d), lambda a, i: (a, 0, 0)),
                  pl.BlockSpec((None, ncp, d), lambda a, i: (a, 0, 0)),
                  pl.BlockSpec((ncp, LANES), lambda a, i: (0, 0))],
        out_specs=[pl.BlockSpec((HPG, tq, d), lambda a, i: (a, i, 0)),
                   pl.BlockSpec((None, tq, LANES), lambda a, i: (a, i, 0))],
        out_shape=[jax.ShapeDtypeStruct((H_NSA, t, d), F32), jax.ShapeDtypeStruct((g, t, LANES), BF16)],
        compiler_params=_cparams(("arbitrary", "arbitrary")),
        name="nsa_cmp_select",
    )(q, kcmp, vcmp, imat)


def _merge_kernel(osb_ref, odf_ref, ocm_ref, osl_ref, owi_ref, gate_ref, lam_ref, dn_ref, x_ref, ga_ref, nrm_ref,
                  w_ref, o_ref, cat_scr, *, lam_init):
    lv = lam_ref[...]
    lam = (jnp.exp(jnp.sum(lv[0:1] * lv[1:2], axis=-1, keepdims=True))
           - jnp.exp(jnp.sum(lv[2:3] * lv[3:4], axis=-1, keepdims=True)) + lam_init)
    for h in range(H_SB):
        cat_scr[:, h * HEAD_DIM:(h + 1) * HEAD_DIM] = osb_ref[h].astype(BF16)
    dn = dn_ref[...]
    for h in range(H_DIFF):
        ob = odf_ref[2 * h] - lam * odf_ref[2 * h + 1]
        ob = _rms_rows(ob, dn) * (1.0 - lam_init)
        c0 = (H_SB + h) * HEAD_DIM
        cat_scr[:, c0:c0 + HEAD_DIM] = ob.astype(BF16)
    gates = gate_ref[...]
    for h in range(H_NSA):
        oc = (gates[:, 3 * h:3 * h + 1] * ocm_ref[h] + gates[:, 3 * h + 1:3 * h + 2] * osl_ref[h]
              + gates[:, 3 * h + 2:3 * h + 3] * owi_ref[h])
        c0 = (H_SB + H_DIFF + h) * HEAD_DIM
        cat_scr[:, c0:c0 + HEAD_DIM] = oc.astype(BF16)
    y = _dot(cat_scr[...], w_ref[...])
    o_ref[...] = x_ref[...] + ga_ref[...] * _rms_rows(y, nrm_ref[...])


def _merge(osb, odf, ocm, osl, owi, gates, lam_vecs, diff_norm, x, ga, nrm, w, layer):
    t, d = x.shape
    tm = min(256, t)
    lam_init = 0.8 - 0.6 * math.exp(-0.3 * layer)
    hspec = lambda n: pl.BlockSpec((n, tm, HEAD_DIM), lambda i: (0, i, 0))
    return pl.pallas_call(
        functools.partial(_merge_kernel, lam_init=lam_init),
        grid=(t // tm,),
        in_specs=[hspec(H_SB), hspec(2 * H_DIFF), hspec(H_NSA), hspec(H_NSA), hspec(H_NSA),
                  pl.BlockSpec((tm, LANES), lambda i: (i, 0)),
                  pl.BlockSpec(lam_vecs.shape, lambda i: (0, 0)),
                  pl.BlockSpec((1, HEAD_DIM), lambda i: (0, 0)),
                  pl.BlockSpec((tm, d), lambda i: (i, 0)),
                  _row_spec(ga, tm, 1),
                  pl.BlockSpec((1, d), lambda i: (0, 0)),
                  pl.BlockSpec(w.shape, lambda i: (0, 0))],
        out_specs=pl.BlockSpec((tm, d), lambda i: (i, 0)),
        out_shape=jax.ShapeDtypeStruct((t, d), F32),
        scratch_shapes=[pltpu.VMEM((tm, N_HEADS * HEAD_DIM), BF16)],
        compiler_params=_cparams(("arbitrary",)),
        name="merge_out_proj",
    )(osb, odf, ocm, osl, owi, gates, lam_vecs, diff_norm, x, ga, nrm, w)


def _ffn_kernel(x_ref, g2_ref, sc_ref, sh_ref, wg_ref, wu_ref, wd_ref, gf_ref, g3_ref, o_ref, h_scr, acc_scr):
    f = pl.program_id(1)

    @pl.when(f == 0)
    def _():
        y = _rms_rows(x_ref[...], g2_ref[...])
        h_scr[...] = (y * (1.0 + sc_ref[...]) + sh_ref[...]).astype(BF16)
        acc_scr[...] = jnp.zeros_like(acc_scr)

    h = h_scr[...]
    a = _dot(h, wg_ref[...])
    b = _dot(h, wu_ref[...])
    acc_scr[...] += _dot((_silu(a) * b).astype(BF16), wd_ref[...])

    @pl.when(f == pl.num_programs(1) - 1)
    def _():
        o_ref[...] = x_ref[...] + gf_ref[...] * _rms_rows(acc_scr[...], g3_ref[...])


def _ffn(x, g2, sc, sh, wgu, wd, gf, g3):
    t, d = x.shape
    dff = wd.shape[0]
    tm = min(512, t)
    tf = 512 if dff % 512 == 0 else dff
    nf = dff // tf
    return pl.pallas_call(
        _ffn_kernel,
        grid=(t // tm, nf),
        in_specs=[pl.BlockSpec((tm, d), lambda i, f: (i, 0)),
                  pl.BlockSpec((1, d), lambda i, f: (0, 0)),
                  _row_spec(sc, tm, 2), _row_spec(sh, tm, 2),
                  pl.BlockSpec((d, tf), lambda i, f: (0, f)),
                  pl.BlockSpec((d, tf), lambda i, f: (0, f + nf)),
                  pl.BlockSpec((tf, d), lambda i, f: (f, 0)),
                  _row_spec(gf, tm, 2),
                  pl.BlockSpec((1, d), lambda i, f: (0, 0))],
        out_specs=pl.BlockSpec((tm, d), lambda i, f: (i, 0)),
        out_shape=jax.ShapeDtypeStruct((t, d), F32),
        scratch_shapes=[pltpu.VMEM((tm, d), BF16), pltpu.VMEM((tm, d), F32)],
        compiler_params=_cparams(("arbitrary", "arbitrary")),
        name="ffn_swiglu",
    )(x, g2, sc, sh, wgu, wgu, wd, gf, g3)


ROWS_PER_TOK = 2 * N_KV
DEC_ROWS = 32
BLK_LANES = 256


def _dec_kernel(pt_ref, *refs, pg, nps, past, ns):
    page_refs = refs[:pg]
    (new_ref, qsb_ref, qdf_ref, qsl_ref, u_ref, w1_ref, bsel_ref,
     osb_ref, odf_ref, sacc_ref, sm_ref, sl_ref, cab_ref,
     carry_scr, asb_scr, mdf_scr, adf_scr) = refs[pg:]
    j = pl.program_id(1)
    chunks_per_page = PAGE_SIZE // CMP_STRIDE

    @pl.when(j == 0)
    def _():
        carry_scr[...] = jnp.zeros_like(carry_scr)
        asb_scr[...] = jnp.zeros_like(asb_scr)
        mdf_scr[...] = jnp.full_like(mdf_scr, NEG)
        adf_scr[...] = jnp.zeros_like(adf_scr)

    def step(prefs, page0, masked):
        n = len(prefs)
        nk = n * PAGE_SIZE
        lane = lax.broadcasted_iota(jnp.int32, (DEC_ROWS, nk), 1)
        row = lax.broadcasted_iota(jnp.int32, (DEC_ROWS, nk), 0)
        kpos = page0 * PAGE_SIZE + lane
        qpos = past + (row & (ns - 1))

        def tok_rows(r):
            return jnp.concatenate([p[pl.ds(r, PAGE_SIZE, stride=ROWS_PER_TOK), :] for p in prefs], axis=0).astype(BF16)

        def stick_breaking():
            z = jnp.concatenate([_dot_nt(qsb_ref[8 * h:8 * h + 8, :], tok_rows(2 * h)) for h in range(H_SB)], axis=0)
            l1p = jnp.log(1.0 + jnp.exp(-jnp.abs(z)))
            ls_pos = jnp.minimum(z, 0.0) - l1p
            lk = ls_pos - z
            if masked:
                valid = kpos < qpos
                lk = jnp.where(valid, lk, 0.0)
            hi = lk.astype(BF16)
            lo = (lk - hi.astype(F32)).astype(BF16)
            pieces = ([hi[:, PAGE_SIZE * i:PAGE_SIZE * (i + 1)] for i in range(n)]
                      + [lo[:, PAGE_SIZE * i:PAGE_SIZE * (i + 1)] for i in range(n)])
            local = _dot(jnp.concatenate(pieces, axis=0), u_ref[...])
            run = carry_scr[...]
            after = [None] * n
            for i in reversed(range(n)):
                after[i] = (local[DEC_ROWS * i:DEC_ROWS * (i + 1)] + local[DEC_ROWS * (n + i):DEC_ROWS * (n + i + 1)]
                            + run)
                run = run + jnp.sum(lk[:, PAGE_SIZE * i:PAGE_SIZE * (i + 1)], axis=-1, keepdims=True)
            w = jnp.exp(ls_pos + jnp.concatenate(after, axis=1))
            if masked:
                w = jnp.where(valid, w, 0.0)
            wb = w.astype(BF16)
            asb_scr[...] += jnp.concatenate(
                [_dot(wb[8 * h:8 * h + 8, :], tok_rows(2 * h + 1)) for h in range(H_SB)], axis=0)
            carry_scr[...] = run

        if masked:
            stick_breaking()
        else:
            pl.when(jnp.max(carry_scr[...]) > SB_UNDERFLOW)(stick_breaking)

        r0 = 2 * H_SB
        s = jnp.concatenate([_dot_nt(qdf_ref[8 * h:8 * h + 8, :], tok_rows(r0 + 2 * h)) for h in range(H_DIFF)], axis=0)
        if masked:
            s = jnp.where(kpos <= qpos, s, NEG)
        m_old = mdf_scr[...]
        m_new = jnp.maximum(m_old, jnp.max(s, axis=-1, keepdims=True))
        eb = jnp.exp(s - m_new).astype(BF16)
        ones = jnp.ones((nk, HEAD_DIM), BF16)
        adf_scr[...] = jnp.exp(m_old - m_new) * adf_scr[...] + jnp.concatenate(
            [_dot(eb[8 * h:8 * h + 8, :], jnp.concatenate([tok_rows(r0 + 2 * h + 1), ones], axis=1))
             for h in range(H_DIFF)], axis=0)
        mdf_scr[...] = m_new

        r0 = 2 * (H_SB + H_DIFF + G_NSA)
        s = jnp.concatenate([_dot_nt(qsl_ref[16 * g:16 * g + 16, :], tok_rows(r0 + 2 * g)) for g in range(G_NSA)], axis=0)
        if masked:
            valid = kpos <= qpos
            s = jnp.where(valid, s, NEG)
        blk = lane // SLC_LEN
        col = lax.broadcasted_iota(jnp.int32, (DEC_ROWS, LANES), 1)
        mref = jnp.zeros((DEC_ROWS, nk), F32)
        mcols = jnp.full((DEC_ROWS, LANES), NEG, F32)
        for b in range(2 * n):
            mb = jnp.max(jnp.where(blk == b, s, NEG), axis=-1, keepdims=True)
            mref = jnp.where(blk == b, mb, mref)
            mcols = jnp.where(col == b, mb, mcols)
        e = jnp.exp(s - mref)
        if masked:
            e = jnp.where(valid, e, 0.0)
        eb = e.astype(BF16)
        sm_ref[...] = mcols
        sl_ref[...] = _dot(eb, bsel_ref[0:nk, :])
        blk16 = lax.broadcasted_iota(jnp.int32, (16, nk), 1) // SLC_LEN
        for g in range(G_NSA):
            eg = eb[16 * g:16 * g + 16, :]
            per_blk = jnp.concatenate([jnp.where(blk16 == b, eg, jnp.zeros_like(eg)) for b in range(2 * n)], axis=0)
            res = _dot(per_blk, tok_rows(r0 + 2 * g + 1))
            for b in range(2 * n):
                sacc_ref[b, 16 * g:16 * g + 16, :] = res[16 * b:16 * b + 16]
        for b in range(2 * n, 2 * pg):
            sacc_ref[b] = jnp.zeros((DEC_ROWS, HEAD_DIM), F32)

        r0 = 2 * (H_SB + H_DIFF)
        lhs = jnp.concatenate(
            [jnp.concatenate([p[pl.ds((CMP_STRIDE * m + l) * ROWS_PER_TOK + r0, 8), :] for l in range(CMP_STRIDE)], axis=1)
             for p in prefs for m in range(chunks_per_page)], axis=0).astype(BF16)
        res = _dot(lhs, w1_ref[...])
        crow = lax.broadcasted_iota(jnp.int32, (n * 8 * chunks_per_page, 2 * HEAD_DIM), 0)
        keep = jnp.where((crow & 1) == 0, res[:, :2 * HEAD_DIM], res[:, 2 * HEAD_DIM:])
        live = n * 8 * chunks_per_page
        for half in range(2):
            cab_ref[half, 0:live, :] = keep[:, half * HEAD_DIM:(half + 1) * HEAD_DIM]
            if n < pg:
                cab_ref[half, live:, :] = jnp.zeros(((pg - n) * 8 * chunks_per_page, HEAD_DIM), F32)

    @pl.when(j == 0)
    def _():
        step([new_ref], nps * pg, True)

    @pl.when(j > 0)
    def _():
        step(list(page_refs), (nps - j) * pg, False)

    @pl.when(j == nps)
    def _():
        osb_ref[...] = asb_scr[...]
        adf = adf_scr[...]
        odf_ref[...] = adf[:, :HEAD_DIM] / jnp.maximum(adf[:, HEAD_DIM:], 1e-30)


def _dec_main(layer, page_table, cache_rows, new_page, qsb, qdf, qsl, w1all, past, ns):
    nb, npages = page_table.shape
    pg = 4 if npages % 4 == 0 else 1
    nps = npages // pg
    rows_per_page = PAGE_SIZE * ROWS_PER_TOK

    def page_spec(i):
        def imap(b, j, pt):
            return (layer, pt[b, (nps - jnp.maximum(j, 1)) * pg + i], 0, 0)
        return pl.BlockSpec((None, None, rows_per_page, HEAD_DIM), imap)

    qspec = pl.BlockSpec((None, DEC_ROWS, HEAD_DIM), lambda b, j, pt: (b, 0, 0))
    nblk_pad = (nps + 1) * 2 * pg
    cab_rows = pg * 8 * (PAGE_SIZE // CMP_STRIDE)
    key = np.arange(pg * PAGE_SIZE)[:, None]
    bsel = jnp.asarray((key // SLC_LEN == np.arange(LANES)[None, :]).astype(np.float32), dtype=BF16)
    stat_spec = pl.BlockSpec((None, None, DEC_ROWS, LANES), lambda b, j, pt: (b, nps - j, 0, 0))
    grid_spec = pltpu.PrefetchScalarGridSpec(
        num_scalar_prefetch=1,
        grid=(nb, nps + 1),
        in_specs=[page_spec(i) for i in range(pg)] + [
            pl.BlockSpec((None, rows_per_page, HEAD_DIM), lambda b, j, pt: (b, 0, 0)),
            qspec, qspec, qspec,
            pl.BlockSpec((PAGE_SIZE, PAGE_SIZE), lambda b, j, pt: (0, 0)),
            pl.BlockSpec(w1all.shape, lambda b, j, pt: (0, 0)),
            pl.BlockSpec(bsel.shape, lambda b, j, pt: (0, 0))],
        out_specs=[
            qspec, qspec,
            pl.BlockSpec((None, 2 * pg, DEC_ROWS, HEAD_DIM), lambda b, j, pt: (b, nps - j, 0, 0)),
            stat_spec, stat_spec,
            pl.BlockSpec((None, 2, cab_rows, HEAD_DIM), lambda b, j, pt: (b, 0, nps - j, 0))],
        scratch_shapes=[pltpu.VMEM((DEC_ROWS, 1), F32), pltpu.VMEM((DEC_ROWS, HEAD_DIM), F32),
                        pltpu.VMEM((DEC_ROWS, 1), F32), pltpu.VMEM((DEC_ROWS, 2 * HEAD_DIM), F32)])
    out_shape = [jax.ShapeDtypeStruct((nb, DEC_ROWS, HEAD_DIM), F32),
                 jax.ShapeDtypeStruct((nb, DEC_ROWS, HEAD_DIM), F32),
                 jax.ShapeDtypeStruct((nb, nblk_pad, DEC_ROWS, HEAD_DIM), F32),
                 jax.ShapeDtypeStruct((nb, nps + 1, DEC_ROWS, LANES), F32),
                 jax.ShapeDtypeStruct((nb, nps + 1, DEC_ROWS, LANES), F32),
                 jax.ShapeDtypeStruct((nb, 2, (nps + 1) * cab_rows, HEAD_DIM), F32)]
    o_sb, o_df, sacc, sm, sl, cab = pl.pallas_call(
        functools.partial(_dec_kernel, pg=pg, nps=nps, past=past, ns=ns),
        grid_spec=grid_spec,
        out_shape=out_shape,
        compiler_params=_cparams(("arbitrary", "arbitrary")),
        name="decode_paged",
    )(page_table, *([cache_rows] * pg), new_page, qsb, qdf, qsl, _suffix_matrix(PAGE_SIZE), w1all, bsel)

    def by_block(stat, fill):
        stat = stat[..., :2 * pg].transpose(0, 2, 1, 3).reshape(nb, DEC_ROWS, nblk_pad)
        return jnp.pad(stat, ((0, 0), (0, 0), (0, BLK_LANES - nblk_pad)), constant_values=fill)

    return o_sb, o_df, sacc, by_block(sm, NEG), by_block(sl, 0.0), cab


def _cmp_bias_kernel(pe_ref, w1_ref, o_ref):
    for kv in range(2):
        for c in range(2):
            o_ref[kv, c] = _dot(jnp.broadcast_to(pe_ref[kv, c:c + 1, :], (8, pe_ref.shape[-1])).astype(BF16),
                                w1_ref[kv, c].astype(BF16))


def _cmp_bias(pe, w1):
    return pl.pallas_call(
        _cmp_bias_kernel,
        out_shape=jax.ShapeDtypeStruct((2, 2, 8, HEAD_DIM), F32),
        compiler_params=pltpu.CompilerParams(vmem_limit_bytes=VMEM_LIMIT),
        name="cmp_bias",
    )(pe, w1)


def _dec_final_kernel(cab_ref, bias_ref, w2_ref, q_ref, imat_ref, jmat_ref, sacc_ref, sm_ref, sl_ref,
                      swin_ref, nwin_ref, ocm_ref, osl_ref, owi_ref, *, nc, past, ns, nblk_pad, wb):
    ncp = cab_ref.shape[1] // 8
    rows16 = lax.broadcasted_iota(jnp.int32, (16, 1), 0)
    qpos16 = past + (rows16 & (ns - 1))
    for g in range(G_NSA):
        q = q_ref[16 * g:16 * g + 16, :]
        cmp_kv = []
        for kv in range(2):
            a = cab_ref[0, pl.ds(2 * g + kv, ncp, stride=8), :]
            b = cab_ref[1, pl.ds(2 * g + kv, ncp, stride=8), :]
            h = a + pltpu.roll(b, ncp - 1, 0) + bias_ref[kv, 0, 0:1, :] + bias_ref[kv, 1, 0:1, :]
            cmp_kv.append(_dot(_silu(h).astype(BF16), w2_ref[kv].astype(BF16)).astype(BF16))
        kc, vc = cmp_kv
        s = _dot_nt(q, kc)
        n = lax.broadcasted_iota(jnp.int32, s.shape, 1)
        mask = (n * CMP_STRIDE + CMP_LEN - 1 <= qpos16) & (n < nc)
        s = jnp.where(mask, s, NEG)
        m = jnp.max(s, axis=-1, keepdims=True)
        e = jnp.where(mask, jnp.exp(s - m), 0.0)
        p = e / jnp.maximum(jnp.sum(e, axis=-1, keepdims=True), 1e-30)
        ocm_ref[16 * g:16 * g + 16, :] = _dot(p.astype(BF16), vc)
        hi, mid, lo = _split3(p)
        jm = jmat_ref[...]
        psum = _dot(jm, hi) + _dot(jm, mid) + _dot(jm, lo)
        imp = _dot_f32ish(psum, imat_ref[...])
        blk = lax.broadcasted_iota(jnp.int32, imp.shape, 1)
        sel = _select_blocks(imp, qpos16 // SLC_LEN, blk)
        mb = sm_ref[16 * g:16 * g + 16, :]
        lb = sl_ref[16 * g:16 * g + 16, :]
        picked = sel > 0.5
        mtot = jnp.max(jnp.where(picked, mb, NEG), axis=-1, keepdims=True)
        wgt = jnp.where(picked, jnp.exp(mb - mtot), 0.0)
        den = jnp.sum(wgt * lb, axis=-1, keepdims=True)
        wgt_b = wgt.astype(BF16)
        pad = jnp.zeros((BLK_LANES - nblk_pad, HEAD_DIM), F32)
        for half in range(2):
            rsel = lax.broadcasted_iota(jnp.int32, (8, HEAD_DIM), 0)
            num = jnp.zeros((8, HEAD_DIM), F32)
            for i in range(8):
                r32 = 16 * g + 8 * half + i
                acc = jnp.concatenate([sacc_ref[pl.ds(r32, nblk_pad, stride=DEC_ROWS), :], pad], axis=0)
                ahi = acc.astype(BF16)
                alo = (acc - ahi.astype(F32)).astype(BF16)
                w8 = wgt_b[8 * half:8 * half + 8, :]
                res = _dot(w8, ahi) + _dot(w8, alo)
                num = jnp.where(rsel == i, res, num)
            r0 = 16 * g + 8 * half
            osl_ref[r0:r0 + 8, :] = num / jnp.maximum(den[8 * half:8 * half + 8], 1e-30)
        kw = swin_ref[pl.ds(g, wb, stride=2 * G_NSA), :].astype(BF16)
        vw = swin_ref[pl.ds(G_NSA + g, wb, stride=2 * G_NSA), :].astype(BF16)
        kn = nwin_ref[pl.ds(g, 8, stride=2 * G_NSA), :].astype(BF16)
        vn = nwin_ref[pl.ds(G_NSA + g, 8, stride=2 * G_NSA), :].astype(BF16)
        s1 = _dot_nt(q, kw)
        s2 = _dot_nt(q, kn)
        d1 = qpos16 - (past - wb + lax.broadcasted_iota(jnp.int32, s1.shape, 1))
        k1 = (d1 >= 0) & (d1 <= WINDOW)
        d2 = qpos16 - (past + lax.broadcasted_iota(jnp.int32, s2.shape, 1))
        k2 = (d2 >= 0) & (d2 <= WINDOW)
        s1 = jnp.where(k1, s1, NEG)
        s2 = jnp.where(k2, s2, NEG)
        m = jnp.maximum(jnp.max(s1, axis=-1, keepdims=True), jnp.max(s2, axis=-1, keepdims=True))
        e1 = jnp.where(k1, jnp.exp(s1 - m), 0.0)
        e2 = jnp.where(k2, jnp.exp(s2 - m), 0.0)
        den = jnp.sum(e1, axis=-1, keepdims=True) + jnp.sum(e2, axis=-1, keepdims=True)
        owi_ref[16 * g:16 * g + 16, :] = (_dot(e1.astype(BF16), vw) + _dot(e2.astype(BF16), vn)) / jnp.maximum(den, 1e-30)


def _dec_final(layer, cab, bias, w2, qsl, sacc, sm, sl, state_rows, new_win, nc, past, ns):
    nb = cab.shape[0]
    ncp = cab.shape[2] // 8
    nblk_pad = sacc.shape[1] // DEC_ROWS
    wb = state_rows.shape[2] // (2 * G_NSA)
    imat = _importance_matrix(ncp, BLK_LANES)
    t_of = np.arange(16) % ns
    jmat = jnp.asarray((t_of[:, None] == t_of[None, :]).astype(np.float32), dtype=BF16)
    ospec = pl.BlockSpec((None, DEC_ROWS, HEAD_DIM), lambda b: (b, 0, 0))
    return pl.pallas_call(
        functools.partial(_dec_final_kernel, nc=nc, past=past, ns=ns, nblk_pad=nblk_pad, wb=wb),
        grid=(nb,),
        in_specs=[pl.BlockSpec((None, 2, ncp * 8, HEAD_DIM), lambda b: (b, 0, 0, 0)),
                  pl.BlockSpec(bias.shape, lambda b: (0, 0, 0, 0)),
                  pl.BlockSpec(w2.shape, lambda b: (0, 0, 0)),
                  ospec,
                  pl.BlockSpec(imat.shape, lambda b: (0, 0)),
                  pl.BlockSpec(jmat.shape, lambda b: (0, 0)),
                  pl.BlockSpec((None, nblk_pad * DEC_ROWS, HEAD_DIM), lambda b: (b, 0, 0)),
                  pl.BlockSpec((None, DEC_ROWS, BLK_LANES), lambda b: (b, 0, 0)),
                  pl.BlockSpec((None, DEC_ROWS, BLK_LANES), lambda b: (b, 0, 0)),
                  pl.BlockSpec((None, None, state_rows.shape[2], HEAD_DIM), lambda b: (layer, b, 0, 0)),
                  pl.BlockSpec((None, new_win.shape[1], HEAD_DIM), lambda b: (b, 0, 0))],
        out_specs=[ospec, ospec, ospec],
        out_shape=[jax.ShapeDtypeStruct((nb, DEC_ROWS, HEAD_DIM), F32)] * 3,
        compiler_params=_cparams(("arbitrary",)),
        name="decode_nsa_finish",
    )(cab, bias, w2, qsl, imat, jmat, sacc, sm, sl, state_rows, new_win)


def _sample_mixer(x, mod, lw, layer, tabs, page_table, cache_rows, state_rows, nb, ns, past):
    sh_a, sc_a, g_a = mod[0], mod[1], mod[2]
    proj = _proj(x, lw["n0"], sc_a, sh_a, lw["w_in"])
    (qsb, _, _, qdf, _, _, qns, _, _, _, _, _, _, gates, kv_rows, win_rows) = _postproj(proj, tabs)

    def per_seq(a, heads, rows_per_head):
        a = a.reshape(heads, nb, ns, HEAD_DIM).transpose(1, 0, 2, 3)
        a = jnp.pad(a, ((0, 0), (0, 0), (0, rows_per_head - ns), (0, 0)))
        return a.reshape(nb, heads * rows_per_head, HEAD_DIM)

    qsb_d = per_seq(qsb, H_SB, 8)
    qdf_d = per_seq(qdf, 2 * H_DIFF, ns)
    qsl_d = per_seq(qns, H_NSA, ns)
    kv_tok = kv_rows.reshape(nb, ns, 2, N_KV, HEAD_DIM).transpose(0, 1, 3, 2, 4).reshape(nb, ns * ROWS_PER_TOK, HEAD_DIM)
    new_page = jnp.pad(kv_tok, ((0, 0), (0, (PAGE_SIZE - ns) * ROWS_PER_TOK), (0, 0)))
    w1 = lw["cmp_w1"]
    w1all = jnp.concatenate([w1[0, 0], w1[0, 1], w1[1, 0], w1[1, 1]], axis=-1).astype(BF16)
    o_sb, o_df, sacc, sm, sl, cab = _dec_main(layer, page_table, cache_rows, new_page, qsb_d, qdf_d, qsl_d, w1all,
                                              past, ns)
    bias = _cmp_bias(lw["cmp_pe"], w1)
    new_win = jnp.pad(win_rows.reshape(nb, ns * 2 * G_NSA, HEAD_DIM), ((0, 0), (0, (8 - ns) * 2 * G_NSA), (0, 0)))
    padded = -(-(past + ns) // SLC_LEN) * SLC_LEN
    nc = padded // CMP_STRIDE - CMP_LEN // CMP_STRIDE + 1
    o_cmp, o_slc, o_win = _dec_final(layer, cab, bias, lw["cmp_w2"], qsl_d,
                                     sacc.reshape(nb, -1, HEAD_DIM), sm, sl, state_rows, new_win, nc, past, ns)

    def head_major(a, heads, rows_per_head):
        a = a.reshape(nb, heads, rows_per_head, HEAD_DIM)[:, :, :ns]
        return a.transpose(1, 0, 2, 3).reshape(heads, nb * ns, HEAD_DIM)

    x1 = _merge(head_major(o_sb, H_SB, 8), head_major(o_df, 2 * H_DIFF, ns), head_major(o_cmp, H_NSA, ns),
                head_major(o_slc, H_NSA, ns), head_major(o_win, H_NSA, ns), gates, lw["diff_lambda"],
                lw["diff_norm"], x, g_a, lw["n1"], lw["w_out"], layer)
    return x1, kv_rows, win_rows


def _split_mod(mod):
    return [mod[:, i * (mod.shape[1] // 6):(i + 1) * (mod.shape[1] // 6)] for i in range(6)]


def _prompt_mixer(x, mod, lw, layer, tabs):
    t = x.shape[0]
    sh_a, sc_a, g_a = mod[0], mod[1], mod[2]
    proj = _proj(x, lw["n0"], sc_a, sh_a, lw["w_in"])
    (qsb, ksb, vsb, qdf, kdf, vdf, qns, kc, vc, ks, vs, kw, vw, gates, kv_rows, win_rows) = _postproj(proj, tabs)
    o_sb = _sb_attention(qsb, ksb, vsb)
    o_df = _flash(qdf, kdf, vdf, "causal")
    nch = t // CMP_STRIDE
    xc = jnp.stack([kc, vc]).reshape(2, G_NSA, nch, CMP_STRIDE * HEAD_DIM)
    cmp_kv = _compress(xc, lw["cmp_pe"], lw["cmp_w1"], lw["cmp_w2"])
    o_cmp, sel = _cmp_select(qns, cmp_kv[0], cmp_kv[1], nch - 1)
    o_slc = _flash(qns, ks, vs, "select", sel)
    o_win = _flash(qns, kw, vw, "window")
    x1 = _merge(o_sb, o_df, o_cmp, o_slc, o_win, gates, lw["diff_lambda"], lw["diff_norm"], x, g_a, lw["n1"],
                lw["w_out"], layer)
    return x1, kv_rows, win_rows


def _layer_weights(layer, norms, w_in_p, w_out_b, diff_lambda, diff_norm, cmp_pe, cmp_w1, cmp_w2, wgu_b, wd_b):
    d = norms.shape[-1]
    return dict(
        n0=norms[layer, 0].reshape(1, d), n1=norms[layer, 1].reshape(1, d),
        n2=norms[layer, 2].reshape(1, d), n3=norms[layer, 3].reshape(1, d),
        w_in=w_in_p[layer], w_out=w_out_b[layer],
        diff_lambda=diff_lambda[layer], diff_norm=diff_norm[layer].reshape(1, HEAD_DIM),
        cmp_pe=cmp_pe[layer].reshape(2, 2, CMP_STRIDE * HEAD_DIM),
        cmp_w1=cmp_w1[layer].reshape(2, 2, CMP_STRIDE * HEAD_DIM, HEAD_DIM),
        cmp_w2=cmp_w2[layer], w_gate_up=wgu_b[layer], w_down=wd_b[layer])


def kernel(x_prompt, x_sample, cache_kv, state_win_kv, page_table, c_prompt, c_sample, ada_w, ada_b, norms, w_in,
           w_out, diff_lambda, diff_norm, cmp_pe, cmp_w1, cmp_w2, w_gate_up, w_down):
    depth = ada_w.shape[0]
    _, seq, d = x_prompt.shape
    nb, ns, _ = x_sample.shape
    past = page_table.shape[1] * PAGE_SIZE

    w_in_p = jnp.pad(w_in, ((0, 0), (0, 0), (0, N_IN_PAD - N_IN))).astype(BF16)
    w_out_b = w_out.astype(BF16)
    wgu_b = w_gate_up.astype(BF16)
    wd_b = w_down.astype(BF16)

    n_c = 1 + nb
    rows = -(-n_c // 8) * 8
    c_all = jnp.pad(jnp.concatenate([c_prompt, c_sample], axis=0), ((0, rows - n_c), (0, 0)))
    mod_all = _ada(c_all, ada_w, ada_b)

    tabs_p = _rope_tables(jnp.arange(seq))
    tabs_s = _rope_tables(past + jnp.arange(nb * ns) % ns)
    cache_rows = cache_kv.transpose(0, 1, 2, 4, 3, 5).reshape(depth, cache_kv.shape[1], PAGE_SIZE * ROWS_PER_TOK,
                                                               HEAD_DIM)
    wb_s = state_win_kv.shape[2]
    state_rows = state_win_kv.reshape(depth, nb, wb_s * 2 * G_NSA, HEAD_DIM)
    xp = x_prompt.reshape(seq, d)
    xs = x_sample.reshape(nb * ns, d)
    kv_p, win_p, kv_s, win_s = [], [], [], []
    for layer in range(depth):
        lw = _layer_weights(layer, norms, w_in_p, w_out_b, diff_lambda, diff_norm, cmp_pe, cmp_w1, cmp_w2, wgu_b, wd_b)
        mod_p = _split_mod(mod_all[layer, 0:1])
        x1, kv_rows, win_rows = _prompt_mixer(xp, mod_p, lw, layer, tabs_p)
        xp = _ffn(x1, lw["n2"], mod_p[4], mod_p[3], lw["w_gate_up"], lw["w_down"], mod_p[5], lw["n3"])
        kv_p.append(kv_rows.reshape(1, seq, 2, N_KV, HEAD_DIM))
        wb = min(WINDOW, seq)
        win_p.append(win_rows[seq - wb:].reshape(1, wb, 2, G_NSA, HEAD_DIM))

        mod_s = _split_mod(jnp.repeat(mod_all[layer, 1:1 + nb], ns, axis=0))
        x1, kv_rows, win_rows = _sample_mixer(xs, mod_s, lw, layer, tabs_s, page_table, cache_rows, state_rows,
                                              nb, ns, past)
        xs = _ffn(x1, lw["n2"], mod_s[4], mod_s[3], lw["w_gate_up"], lw["w_down"], mod_s[5], lw["n3"])
        kv_s.append(kv_rows.reshape(nb, ns, 2, N_KV, HEAD_DIM))
        win_new = win_rows.reshape(nb, ns, 2, G_NSA, HEAD_DIM)
        win_s.append(jnp.concatenate([state_win_kv[layer], win_new], axis=1)[:, ns:])
    return (xp.reshape(1, seq, d), xs.reshape(nb, ns, d), jnp.stack(kv_p), jnp.stack(kv_s), jnp.stack(win_p),
            jnp.stack(win_s))
```

```python
import functools
import math

import numpy as np
import jax
import jax.numpy as jnp
from jax import lax
from jax.experimental import pallas as pl
from jax.experimental.pallas import tpu as pltpu

F32 = jnp.float32
BF16 = jnp.bfloat16

HEAD_DIM = 128
H_SB = 4
H_DIFF = 4
H_NSA = 8
G_NSA = 2
HPG = H_NSA // G_NSA
N_KV = H_SB + H_DIFF + 2 * G_NSA
N_HEADS = H_SB + H_DIFF + H_NSA
DIFF_DIM = HEAD_DIM // 2
CMP_LEN = 32
CMP_STRIDE = 16
SLC_LEN = 64
SLC_TOPK = 16
WINDOW = 512
PAGE_SIZE = 128
ROPE_THETA = 10000.0
EPS = 1e-6
NEG = -1e30
LANES = 128
VMEM_LIMIT = 56 * 1024 * 1024

C_QA, C_KA, C_VA = 0, 512, 1024
C_QB, C_KB, C_VB = 1536, 2048, 2560
C_QC = 3072
C_KC, C_VC, C_KS, C_VS, C_KW, C_VW = 4096, 4352, 4608, 4864, 5120, 5376
C_GATE = 5632
N_IN = 5656
N_IN_PAD = 5760


def _cparams(sem):
    return pltpu.CompilerParams(dimension_semantics=sem, vmem_limit_bytes=VMEM_LIMIT)


def _silu(x):
    return x * jax.nn.sigmoid(x)


def _rms_rows(x, g):
    return x * lax.rsqrt(jnp.mean(x * x, axis=-1, keepdims=True) + EPS) * g


def _dot(a, b):
    return jnp.dot(a, b, preferred_element_type=F32)


def _dot_nt(a, b):
    return lax.dot_general(a, b, (((1,), (1,)), ((), ())), preferred_element_type=F32)


def _split3(x):
    hi = x.astype(BF16)
    r1 = x - hi.astype(F32)
    mid = r1.astype(BF16)
    lo = (r1 - mid.astype(F32)).astype(BF16)
    return hi, mid, lo


def _dot_f32ish(x, m01):
    hi, mid, lo = _split3(x)
    return _dot(hi, m01) + _dot(mid, m01) + _dot(lo, m01)


def _ada_kernel(c_ref, w_ref, b_ref, o_ref):
    a = _silu(c_ref[...]).astype(BF16)
    o_ref[...] = _dot(a, w_ref[...].astype(BF16)) + b_ref[...]


def _ada(c_all, ada_w, ada_b):
    depth, d, n = ada_w.shape
    rows = c_all.shape[0]
    tn = 1024
    return pl.pallas_call(
        _ada_kernel,
        grid=(depth, n // tn),
        in_specs=[pl.BlockSpec((rows, d), lambda l, j: (0, 0)),
                  pl.BlockSpec((None, d, tn), lambda l, j: (l, 0, j)),
                  pl.BlockSpec((None, 1, tn), lambda l, j: (l, 0, j))],
        out_specs=pl.BlockSpec((None, rows, tn), lambda l, j: (l, 0, j)),
        out_shape=jax.ShapeDtypeStruct((depth, rows, n), F32),
        compiler_params=_cparams(("arbitrary", "arbitrary")),
        name="ada_mod",
    )(c_all, ada_w, ada_b.reshape(depth, 1, n))


def _row_spec(arr, tm, grid_rank):
    c = arr.shape[1]
    if arr.shape[0] == 1:
        if grid_rank == 1:
            return pl.BlockSpec((1, c), lambda i: (0, 0))
        return pl.BlockSpec((1, c), lambda i, j: (0, 0))
    if grid_rank == 1:
        return pl.BlockSpec((tm, c), lambda i: (i, 0))
    return pl.BlockSpec((tm, c), lambda i, j: (i, 0))


def _proj_kernel(x_ref, g_ref, sc_ref, sh_ref, w_ref, o_ref, h_scr):
    @pl.when(pl.program_id(1) == 0)
    def _():
        y = _rms_rows(x_ref[...], g_ref[...])
        h_scr[...] = (y * (1.0 + sc_ref[...]) + sh_ref[...]).astype(BF16)

    o_ref[...] = _dot(h_scr[...], w_ref[...])


def _proj(x, g, sc, sh, w):
    t, d = x.shape
    n = w.shape[1]
    tm = min(512, t)
    tn = 1920
    return pl.pallas_call(
        _proj_kernel,
        grid=(t // tm, n // tn),
        in_specs=[pl.BlockSpec((tm, d), lambda i, j: (i, 0)),
                  pl.BlockSpec((1, d), lambda i, j: (0, 0)),
                  _row_spec(sc, tm, 2), _row_spec(sh, tm, 2),
                  pl.BlockSpec((d, tn), lambda i, j: (0, j))],
        out_specs=pl.BlockSpec((tm, tn), lambda i, j: (i, j)),
        out_shape=jax.ShapeDtypeStruct((t, n), F32),
        scratch_shapes=[pltpu.VMEM((tm, d), BF16)],
        compiler_params=_cparams(("arbitrary", "arbitrary")),
        name="in_proj",
    )(x, g, sc, sh, w)


def _postproj_kernel(p_ref, ca_ref, sa_ref, cb_ref, sb_ref,
                     qsb_ref, ksb_ref, vsb_ref, qdf_ref, kdf_ref, vdf_ref, qns_ref,
                     kc_ref, vc_ref, ks_ref, vs_ref, kw_ref, vw_ref, gate_ref, kv_ref, win_ref):
    ca, sa, cb, sb = ca_ref[...], sa_ref[...], cb_ref[...], sb_ref[...]
    lane = lax.broadcasted_iota(jnp.int32, ca.shape, 1)
    low_half = (lane % DIFF_DIM) < (DIFF_DIM // 2)
    first_chunk = lane < DIFF_DIM
    sm_scale = HEAD_DIM ** -0.5
    df_scale = DIFF_DIM ** -0.5

    def col(c0, h):
        return p_ref[:, c0 + h * HEAD_DIM:c0 + (h + 1) * HEAD_DIM]

    def rope_a(x):
        return x * ca + pltpu.roll(x, HEAD_DIM // 2, 1) * sa

    def rope_b(x):
        partner = jnp.where(low_half, pltpu.roll(x, HEAD_DIM - DIFF_DIM // 2, 1), pltpu.roll(x, DIFF_DIM // 2, 1))
        return x * cb + partner * sb

    def put_kv(is_v, head, val):
        c0 = (N_KV * is_v + head) * HEAD_DIM
        kv_ref[:, c0:c0 + HEAD_DIM] = val

    for h in range(H_SB):
        qsb_ref[h] = (col(C_QA, h) * sm_scale).astype(BF16)
        k, v = col(C_KA, h), col(C_VA, h)
        ksb_ref[h] = k.astype(BF16)
        vsb_ref[h] = v.astype(BF16)
        put_kv(0, h, k)
        put_kv(1, h, v)
    for h in range(H_DIFF):
        q = rope_b(col(C_QB, h)) * df_scale
        qdf_ref[2 * h] = jnp.where(first_chunk, q, 0.0).astype(BF16)
        qdf_ref[2 * h + 1] = jnp.where(first_chunk, 0.0, q).astype(BF16)
        k, v = rope_b(col(C_KB, h)), col(C_VB, h)
        kdf_ref[h] = k.astype(BF16)
        vdf_ref[h] = v.astype(BF16)
        put_kv(0, H_SB + h, k)
        put_kv(1, H_SB + h, v)
    for h in range(H_NSA):
        qns_ref[h] = (rope_a(col(C_QC, h)) * sm_scale).astype(BF16)
    for g in range(G_NSA):
        k, v = rope_a(col(C_KC, g)), col(C_VC, g)
        kc_ref[g] = k
        vc_ref[g] = v
        put_kv(0, H_SB + H_DIFF + g, k)
        put_kv(1, H_SB + H_DIFF + g, v)
        k, v = rope_a(col(C_KS, g)), col(C_VS, g)
        ks_ref[g] = k.astype(BF16)
        vs_ref[g] = v.astype(BF16)
        put_kv(0, H_SB + H_DIFF + G_NSA + g, k)
        put_kv(1, H_SB + H_DIFF + G_NSA + g, v)
        k, v = rope_a(col(C_KW, g)), col(C_VW, g)
        kw_ref[g] = k.astype(BF16)
        vw_ref[g] = v.astype(BF16)
        win_ref[:, g * HEAD_DIM:(g + 1) * HEAD_DIM] = k
        win_ref[:, (G_NSA + g) * HEAD_DIM:(G_NSA + g + 1) * HEAD_DIM] = v
    gate_ref[...] = jax.nn.sigmoid(p_ref[:, C_GATE:C_GATE + LANES])


def _postproj(proj, tabs):
    t = proj.shape[0]
    tm = min(256, t)
    hm = lambda n, dt: jax.ShapeDtypeStruct((n, t, HEAD_DIM), dt)
    hspec = lambda n: pl.BlockSpec((n, tm, HEAD_DIM), lambda i: (0, i, 0))
    tab_spec = pl.BlockSpec((tm, LANES), lambda i: (i, 0))
    out_shapes = [hm(H_SB, BF16), hm(H_SB, BF16), hm(H_SB, BF16),
                  hm(2 * H_DIFF, BF16), hm(H_DIFF, BF16), hm(H_DIFF, BF16), hm(H_NSA, BF16),
                  hm(G_NSA, F32), hm(G_NSA, F32), hm(G_NSA, BF16), hm(G_NSA, BF16),
                  hm(G_NSA, BF16), hm(G_NSA, BF16),
                  jax.ShapeDtypeStruct((t, LANES), F32),
                  jax.ShapeDtypeStruct((t, 2 * N_KV * HEAD_DIM), F32),
                  jax.ShapeDtypeStruct((t, 2 * G_NSA * HEAD_DIM), F32)]
    out_specs = [hspec(H_SB), hspec(H_SB), hspec(H_SB), hspec(2 * H_DIFF), hspec(H_DIFF), hspec(H_DIFF),
                 hspec(H_NSA), hspec(G_NSA), hspec(G_NSA), hspec(G_NSA), hspec(G_NSA), hspec(G_NSA), hspec(G_NSA),
                 pl.BlockSpec((tm, LANES), lambda i: (i, 0)),
                 pl.BlockSpec((tm, 2 * N_KV * HEAD_DIM), lambda i: (i, 0)),
                 pl.BlockSpec((tm, 2 * G_NSA * HEAD_DIM), lambda i: (i, 0))]
    return pl.pallas_call(
        _postproj_kernel,
        grid=(t // tm,),
        in_specs=[pl.BlockSpec((tm, N_IN_PAD), lambda i: (i, 0)), tab_spec, tab_spec, tab_spec, tab_spec],
        out_specs=out_specs,
        out_shape=out_shapes,
        compiler_params=_cparams(("arbitrary",)),
        name="rope_split",
    )(proj, *tabs)


def _rope_tables(pos):
    posf = pos.astype(F32)[:, None]

    def tab(half, reps):
        inv = jnp.power(ROPE_THETA, -jnp.arange(half, dtype=F32) / half)
        ang = posf * inv[None, :]
        c, s = jnp.cos(ang), jnp.sin(ang)
        return jnp.tile(jnp.concatenate([c, c], 1), (1, reps)), jnp.tile(jnp.concatenate([-s, s], 1), (1, reps))

    ca, sa = tab(HEAD_DIM // 2, 1)
    cb, sb = tab(DIFF_DIM // 2, 2)
    return ca, sa, cb, sb


SB_UNDERFLOW = -104.0


def _sb_kernel(q_ref, k_ref, v_ref, u_ref, o_ref, carry_scr, acc_scr, *, tq, tk):
    qi = pl.program_id(1)
    carry_scr[...] = jnp.zeros_like(carry_scr)
    acc_scr[...] = jnp.zeros_like(acc_scr)
    q = q_ref[...]
    u = u_ref[...]
    row = lax.broadcasted_iota(jnp.int32, (tq, tk), 0)
    col = lax.broadcasted_iota(jnp.int32, (tq, tk), 1)

    def body(state):
        kj, _ = state
        off = pl.multiple_of(kj * tk, tk)
        z = _dot_nt(q, k_ref[pl.ds(off, tk), :])
        mask = (off + col) < (qi * tq + row)
        l1p = jnp.log(1.0 + jnp.exp(-jnp.abs(z)))
        ls_pos = jnp.minimum(z, 0.0) - l1p
        lk = jnp.where(mask, ls_pos - z, 0.0)
        hi = lk.astype(BF16)
        lo = (lk - hi.astype(F32)).astype(BF16)
        after = _dot(hi, u) + _dot(lo, u) + carry_scr[...]
        w = jnp.where(mask, jnp.exp(ls_pos + after), 0.0)
        acc_scr[...] += _dot(w.astype(BF16), v_ref[pl.ds(off, tk), :])
        carry = carry_scr[...] + jnp.sum(lk, axis=-1, keepdims=True)
        carry_scr[...] = carry
        go = (jnp.max(carry) > SB_UNDERFLOW).astype(jnp.int32)
        return kj - 1, go

    last_tile = (qi + 1) * (tq // tk) - 1
    lax.while_loop(lambda st: (st[0] >= 0) & (st[1] > 0), body, (last_tile, jnp.int32(1)))
    o_ref[...] = acc_scr[...]


def _suffix_matrix(n):
    j = np.arange(n)[:, None]
    s = np.arange(n)[None, :]
    return jnp.asarray((j > s).astype(np.float32), dtype=BF16)


def _sb_attention(q, k, v):
    h, t, d = q.shape
    tq = min(256, t)
    tk = min(128, t)
    return pl.pallas_call(
        functools.partial(_sb_kernel, tq=tq, tk=tk),
        grid=(h, t // tq),
        in_specs=[pl.BlockSpec((None, tq, d), lambda hh, i: (hh, i, 0)),
                  pl.BlockSpec((None, t, d), lambda hh, i: (hh, 0, 0)),
                  pl.BlockSpec((None, t, d), lambda hh, i: (hh, 0, 0)),
                  pl.BlockSpec((tk, tk), lambda hh, i: (0, 0))],
        out_specs=pl.BlockSpec((None, tq, d), lambda hh, i: (hh, i, 0)),
        out_shape=jax.ShapeDtypeStruct((h, t, d), F32),
        scratch_shapes=[pltpu.VMEM((tq, 1), F32), pltpu.VMEM((tq, d), F32)],
        compiler_params=_cparams(("arbitrary", "arbitrary")),
        name="sb_attn",
    )(q, k, v, _suffix_matrix(tk))


SEL_BIAS = 16384.0


def _flash_kernel(q_ref, k_ref, v_ref, *rest, mode, rep, tq, tk):
    if mode == "select":
        bias_ref, et_ref, o_ref, m_scr, acc_scr, s_scr, s2_scr = rest
    else:
        o_ref, m_scr, acc_scr, s_scr, s2_scr = rest
    qi = pl.program_id(1)
    rows = rep * tq
    m_scr[...] = jnp.full_like(m_scr, NEG)
    acc_scr[...] = jnp.zeros_like(acc_scr)
    q = q_ref[...].reshape(rows, HEAD_DIM)
    if mode == "select":
        q = jnp.concatenate([q, jnp.concatenate([bias_ref[...]] * rep, axis=0)], axis=1)
    ones = jnp.ones((tk, HEAD_DIM), BF16)
    qpos = qi * tq + (lax.broadcasted_iota(jnp.int32, (rows, tk), 0) & (tq - 1))
    col = lax.broadcasted_iota(jnp.int32, (rows, tk), 1)

    newest = ((qi + 1) * tq - 1) // tk
    if mode == "window":
        oldest = jnp.maximum(qi * tq - WINDOW, 0) // tk
        n_tiles = newest - oldest + 1
        tile_of = lambda i: newest - i
    else:
        n_tiles = newest + 1
        tile_of = lambda i: i
    n_full = (qi * tq + 1) // tk

    def scores(i):
        off = pl.multiple_of(tile_of(jnp.minimum(i, n_tiles - 1)) * tk, tk)
        k = k_ref[pl.ds(off, tk), :]
        if mode == "select":
            k = jnp.concatenate([k, et_ref[pl.ds(off, tk), :]], axis=1)
        return _dot_nt(q, k)

    def masked_scores(s, off):
        dist = qpos - (off + col)
        mask = (dist >= 0) & (dist <= WINDOW) if mode == "window" else dist >= 0
        return jnp.where(mask, s, NEG)

    def visit(src, dst, i):
        j = tile_of(i)
        off = pl.multiple_of(j * tk, tk)
        if mode == "window":
            src[...] = masked_scores(src[...], off)
        else:
            @pl.when(j >= n_full)
            def _():
                src[...] = masked_scores(src[...], off)
        dst[...] = scores(i + 1)
        s = src[...]
        m_old = m_scr[...]
        m_new = jnp.maximum(m_old, jnp.max(s, axis=-1, keepdims=True))
        e = jnp.exp(s - m_new)
        v_ext = jnp.concatenate([v_ref[pl.ds(off, tk), :], ones], axis=1)
        acc_scr[...] = jnp.exp(m_old - m_new) * acc_scr[...] + _dot(e.astype(BF16), v_ext)
        m_scr[...] = m_new

    def body(p, carry):
        visit(s_scr, s2_scr, 2 * p)
        pl.when(2 * p + 1 < n_tiles)(lambda: visit(s2_scr, s_scr, 2 * p + 1))
        return carry

    s_scr[...] = scores(0)
    lax.fori_loop(0, (n_tiles + 1) // 2, body, 0)
    acc = acc_scr[...]
    out = acc[:, :HEAD_DIM] / jnp.maximum(acc[:, HEAD_DIM:], 1e-30)
    o_ref[...] = out.reshape(rep, tq, HEAD_DIM)


def _block_onehot(t):
    key = np.arange(t)[:, None]
    b = np.arange(LANES)[None, :]
    return jnp.asarray((key // SLC_LEN == b).astype(np.float32), dtype=BF16)


def _flash(q, k, v, mode, sel_bias=None):
    hq, t, d = q.shape
    hk = k.shape[0]
    rep = hq // hk
    tq = min(512 // rep, t)
    tk = min(256 if mode == "window" else 512, t)
    in_specs = [pl.BlockSpec((rep, tq, d), lambda g, i: (g, i, 0)),
                pl.BlockSpec((None, t, d), lambda g, i: (g, 0, 0)),
                pl.BlockSpec((None, t, d), lambda g, i: (g, 0, 0))]
    args = [q, k, v]
    if mode == "select":
        in_specs += [pl.BlockSpec((None, tq, LANES), lambda g, i: (g, i, 0)),
                     pl.BlockSpec((t, LANES), lambda g, i: (0, 0))]
        args += [sel_bias, _block_onehot(t)]
    return pl.pallas_call(
        functools.partial(_flash_kernel, mode=mode, rep=rep, tq=tq, tk=tk),
        grid=(hk, t // tq),
        in_specs=in_specs,
        out_specs=pl.BlockSpec((rep, tq, d), lambda g, i: (g, i, 0)),
        out_shape=jax.ShapeDtypeStruct((hq, t, d), F32),
        scratch_shapes=[pltpu.VMEM((rep * tq, 1), F32), pltpu.VMEM((rep * tq, 2 * d), F32),
                        pltpu.VMEM((rep * tq, tk), F32), pltpu.VMEM((rep * tq, tk), F32)],
        compiler_params=_cparams(("arbitrary", "arbitrary")),
        name="attn_" + mode,
    )(*args)


def _compress_kernel(x_ref, pe_ref, w1_ref, w2_ref, o_ref):
    x = x_ref[...]
    n = x.shape[0]
    a = _dot((x + pe_ref[0:1, :]).astype(BF16), w1_ref[0].astype(BF16))
    b = _dot((x + pe_ref[1:2, :]).astype(BF16), w1_ref[1].astype(BF16))
    h = a + pltpu.roll(b, n - 1, 0)
    o_ref[...] = _dot(_silu(h).astype(BF16), w2_ref[...].astype(BF16)).astype(o_ref.dtype)


def _compress(xc, pe, w1, w2):
    _, g, n, wdt = xc.shape
    return pl.pallas_call(
        _compress_kernel,
        grid=(2, g),
        in_specs=[pl.BlockSpec((None, None, n, wdt), lambda a, b: (a, b, 0, 0)),
                  pl.BlockSpec((None, 2, wdt), lambda a, b: (a, 0, 0)),
                  pl.BlockSpec((None, 2, wdt, HEAD_DIM), lambda a, b: (a, 0, 0, 0)),
                  pl.BlockSpec((None, HEAD_DIM, HEAD_DIM), lambda a, b: (a, 0, 0))],
        out_specs=pl.BlockSpec((None, None, n, HEAD_DIM), lambda a, b: (a, b, 0, 0)),
        out_shape=jax.ShapeDtypeStruct((2, g, n, HEAD_DIM), BF16),
        compiler_params=_cparams(("arbitrary", "arbitrary")),
        name="nsa_compress",
    )(xc, pe, w1, w2)


def _importance_matrix(nc_pad, nblk_pad):
    r = SLC_LEN // CMP_STRIDE
    c = CMP_LEN // CMP_STRIDE
    i = np.arange(nc_pad)[:, None]
    b = np.arange(nblk_pad)[None, :]
    lo = r * b - (c - 1)
    return jnp.asarray(((i >= lo) & (i <= lo + r + c - 2)).astype(np.float32), dtype=BF16)


def _select_blocks(imp, cur, blk):
    forced = (blk == 0) | (blk == cur) | (blk == cur - 1)
    score = jnp.where(forced, 1e30, jnp.where(blk <= cur, imp, NEG))
    sel = jnp.zeros_like(score)
    blkf = blk.astype(F32)
    for _ in range(SLC_TOPK):
        mx = jnp.max(score, axis=-1, keepdims=True)
        idx = jnp.min(jnp.where(score == mx, blkf, 1e9), axis=-1, keepdims=True)
        pick = blkf == idx
        sel = jnp.where(pick, 1.0, sel)
        score = jnp.where(pick, -3e38, score)
    return sel


def _cmp_kernel(q_ref, kc_ref, vc_ref, imat_ref, o_ref, sel_ref, *, tq, nc):
    i = pl.program_id(0)
    rows = HPG * tq
    ncp = kc_ref.shape[1]
    row = lax.broadcasted_iota(jnp.int32, (rows, ncp), 0)
    qpos = i * tq + (row & (tq - 1))
    n = lax.broadcasted_iota(jnp.int32, (rows, ncp), 1)
    mask = (n * CMP_STRIDE + CMP_LEN - 1 <= qpos) & (n < nc)
    imps = []
    for g in range(G_NSA):
        q = q_ref[HPG * g:HPG * (g + 1)].reshape(rows, HEAD_DIM)
        s = jnp.where(mask, _dot_nt(q, kc_ref[g]), NEG)
        m = jnp.max(s, axis=-1, keepdims=True)
        e = jnp.where(mask, jnp.exp(s - m), 0.0)
        p = e / jnp.maximum(jnp.sum(e, axis=-1, keepdims=True), 1e-30)
        o_ref[HPG * g:HPG * (g + 1)] = _dot(p.astype(BF16), vc_ref[g]).reshape(HPG, tq, HEAD_DIM)
        psum = p[0:tq]
        for h in range(1, HPG):
            psum = psum + p[h * tq:(h + 1) * tq]
        imps.append(_dot_f32ish(psum, imat_ref[...]))
    imp = jnp.concatenate(imps, axis=0)
    blk = lax.broadcasted_iota(jnp.int32, imp.shape, 1)
    cur = (i * tq + (lax.broadcasted_iota(jnp.int32, imp.shape, 0) & (tq - 1))) // SLC_LEN
    bias = ((_select_blocks(imp, cur, blk) - 1.0) * SEL_BIAS).astype(BF16)
    for g in range(G_NSA):
        sel_ref[g] = bias[tq * g:tq * (g + 1)]


def _cmp_select(q, kcmp, vcmp, nc):
    _, t, d = q.shape
    g, ncp, _ = kcmp.shape
    tq = min(256, t)
    imat = _importance_matrix(ncp, LANES)
    return pl.pallas_call(
        functools.partial(_cmp_kernel, tq=tq, nc=nc),
        grid=(t // tq,),
        in_specs=[pl.BlockSpec((H_NSA, tq, d), lambda i: (0, i, 0)),
                  pl.BlockSpec((g, ncp, d), lambda i: (0, 0, 0)),
                  pl.BlockSpec((g, ncp, d), lambda i: (0, 0, 0)),
                  pl.BlockSpec((ncp, LANES), lambda i: (0, 0))],
        out_specs=[pl.BlockSpec((H_NSA, tq, d), lambda i: (0, i, 0)),
                   pl.BlockSpec((g, tq, LANES), lambda i: (0, i, 0))],
        out_shape=[jax.ShapeDtypeStruct((H_NSA, t, d), F32), jax.ShapeDtypeStruct((g, t, LANES), BF16)],
        compiler_params=_cparams(("arbitrary",)),
        name="nsa_cmp_select",
    )(q, kcmp, vcmp, imat)


def _merge_kernel(osb_ref, odf_ref, ocm_ref, osl_ref, owi_ref, gate_ref, lam_ref, dn_ref, x_ref, ga_ref, nrm_ref,
                  w_ref, o_ref, cat_scr, *, lam_init):
    lv = lam_ref[...]
    lam = (jnp.exp(jnp.sum(lv[0:1] * lv[1:2], axis=-1, keepdims=True))
           - jnp.exp(jnp.sum(lv[2:3] * lv[3:4], axis=-1, keepdims=True)) + lam_init)
    for h in range(H_SB):
        cat_scr[:, h * HEAD_DIM:(h + 1) * HEAD_DIM] = osb_ref[h].astype(BF16)
    dn = dn_ref[...]
    for h in range(H_DIFF):
        ob = odf_ref[2 * h] - lam * odf_ref[2 * h + 1]
        ob = _rms_rows(ob, dn) * (1.0 - lam_init)
        c0 = (H_SB + h) * HEAD_DIM
        cat_scr[:, c0:c0 + HEAD_DIM] = ob.astype(BF16)
    gates = gate_ref[...]
    for h in range(H_NSA):
        oc = (gates[:, 3 * h:3 * h + 1] * ocm_ref[h] + gates[:, 3 * h + 1:3 * h + 2] * osl_ref[h]
              + gates[:, 3 * h + 2:3 * h + 3] * owi_ref[h])
        c0 = (H_SB + H_DIFF + h) * HEAD_DIM
        cat_scr[:, c0:c0 + HEAD_DIM] = oc.astype(BF16)
    y = _dot(cat_scr[...], w_ref[...])
    o_ref[...] = x_ref[...] + ga_ref[...] * _rms_rows(y, nrm_ref[...])


def _merge(osb, odf, ocm, osl, owi, gates, lam_vecs, diff_norm, x, ga, nrm, w, layer):
    t, d = x.shape
    tm = min(256, t)
    lam_init = 0.8 - 0.6 * math.exp(-0.3 * layer)
    hspec = lambda n: pl.BlockSpec((n, tm, HEAD_DIM), lambda i: (0, i, 0))
    return pl.pallas_call(
        functools.partial(_merge_kernel, lam_init=lam_init),
        grid=(t // tm,),
        in_specs=[hspec(H_SB), hspec(2 * H_DIFF), hspec(H_NSA), hspec(H_NSA), hspec(H_NSA),
                  pl.BlockSpec((tm, LANES), lambda i: (i, 0)),
                  pl.BlockSpec(lam_vecs.shape, lambda i: (0, 0)),
                  pl.BlockSpec((1, HEAD_DIM), lambda i: (0, 0)),
                  pl.BlockSpec((tm, d), lambda i: (i, 0)),
                  _row_spec(ga, tm, 1),
                  pl.BlockSpec((1, d), lambda i: (0, 0)),
                  pl.BlockSpec(w.shape, lambda i: (0, 0))],
        out_specs=pl.BlockSpec((tm, d), lambda i: (i, 0)),
        out_shape=jax.ShapeDtypeStruct((t, d), F32),
        scratch_shapes=[pltpu.VMEM((tm, N_HEADS * HEAD_DIM), BF16)],
        compiler_params=_cparams(("arbitrary",)),
        name="merge_out_proj",
    )(osb, odf, ocm, osl, owi, gates, lam_vecs, diff_norm, x, ga, nrm, w)


def _ffn_kernel(x_ref, g2_ref, sc_ref, sh_ref, wg_ref, wu_ref, wd_ref, gf_ref, g3_ref, o_ref, h_scr, acc_scr):
    f = pl.program_id(1)

    @pl.when(f == 0)
    def _():
        y = _rms_rows(x_ref[...], g2_ref[...])
        h_scr[...] = (y * (1.0 + sc_ref[...]) + sh_ref[...]).astype(BF16)
        acc_scr[...] = jnp.zeros_like(acc_scr)

    h = h_scr[...]
    a = _dot(h, wg_ref[...])
    b = _dot(h, wu_ref[...])
    acc_scr[...] += _dot((_silu(a) * b).astype(BF16), wd_ref[...])

    @pl.when(f == pl.num_programs(1) - 1)
    def _():
        o_ref[...] = x_ref[...] + gf_ref[...] * _rms_rows(acc_scr[...], g3_ref[...])


def _ffn(x, g2, sc, sh, wgu, wd, gf, g3):
    t, d = x.shape
    dff = wd.shape[0]
    tm = min(512, t)
    tf = 512 if dff % 512 == 0 else dff
    nf = dff // tf
    return pl.pallas_call(
        _ffn_kernel,
        grid=(t // tm, nf),
        in_specs=[pl.BlockSpec((tm, d), lambda i, f: (i, 0)),
                  pl.BlockSpec((1, d), lambda i, f: (0, 0)),
                  _row_spec(sc, tm, 2), _row_spec(sh, tm, 2),
                  pl.BlockSpec((d, tf), lambda i, f: (0, f)),
                  pl.BlockSpec((d, tf), lambda i, f: (0, f + nf)),
                  pl.BlockSpec((tf, d), lambda i, f: (f, 0)),
                  _row_spec(gf, tm, 2),
                  pl.BlockSpec((1, d), lambda i, f: (0, 0))],
        out_specs=pl.BlockSpec((tm, d), lambda i, f: (i, 0)),
        out_shape=jax.ShapeDtypeStruct((t, d), F32),
        scratch_shapes=[pltpu.VMEM((tm, d), BF16), pltpu.VMEM((tm, d), F32)],
        compiler_params=_cparams(("arbitrary", "arbitrary")),
        name="ffn_swiglu",
    )(x, g2, sc, sh, wgu, wgu, wd, gf, g3)


ROWS_PER_TOK = 2 * N_KV
DEC_ROWS = 32
BLK_LANES = 256


def _dec_kernel(pt_ref, *refs, pg, nps, past, ns):
    page_refs = refs[:pg]
    (new_ref, qsb_ref, qdf_ref, qsl_ref, u_ref, w1_ref, bsel_ref,
     osb_ref, odf_ref, sacc_ref, sm_ref, sl_ref, cab_ref,
     carry_scr, asb_scr, mdf_scr, adf_scr) = refs[pg:]
    j = pl.program_id(1)
    chunks_per_page = PAGE_SIZE // CMP_STRIDE

    @pl.when(j == 0)
    def _():
        carry_scr[...] = jnp.zeros_like(carry_scr)
        asb_scr[...] = jnp.zeros_like(asb_scr)
        mdf_scr[...] = jnp.full_like(mdf_scr, NEG)
        adf_scr[...] = jnp.zeros_like(adf_scr)

    def step(prefs, page0, masked):
        n = len(prefs)
        nk = n * PAGE_SIZE
        lane = lax.broadcasted_iota(jnp.int32, (DEC_ROWS, nk), 1)
        row = lax.broadcasted_iota(jnp.int32, (DEC_ROWS, nk), 0)
        kpos = page0 * PAGE_SIZE + lane
        qpos = past + (row & (ns - 1))

        def tok_rows(r):
            return jnp.concatenate([p[pl.ds(r, PAGE_SIZE, stride=ROWS_PER_TOK), :] for p in prefs], axis=0).astype(BF16)

        def stick_breaking():
            z = jnp.concatenate([_dot_nt(qsb_ref[8 * h:8 * h + 8, :], tok_rows(2 * h)) for h in range(H_SB)], axis=0)
            l1p = jnp.log(1.0 + jnp.exp(-jnp.abs(z)))
            ls_pos = jnp.minimum(z, 0.0) - l1p
            lk = ls_pos - z
            if masked:
                valid = kpos < qpos
                lk = jnp.where(valid, lk, 0.0)
            hi = lk.astype(BF16)
            lo = (lk - hi.astype(F32)).astype(BF16)
            pieces = ([hi[:, PAGE_SIZE * i:PAGE_SIZE * (i + 1)] for i in range(n)]
                      + [lo[:, PAGE_SIZE * i:PAGE_SIZE * (i + 1)] for i in range(n)])
            local = _dot(jnp.concatenate(pieces, axis=0), u_ref[...])
            run = carry_scr[...]
            after = [None] * n
            for i in reversed(range(n)):
                after[i] = (local[DEC_ROWS * i:DEC_ROWS * (i + 1)] + local[DEC_ROWS * (n + i):DEC_ROWS * (n + i + 1)]
                            + run)
                run = run + jnp.sum(lk[:, PAGE_SIZE * i:PAGE_SIZE * (i + 1)], axis=-1, keepdims=True)
            w = jnp.exp(ls_pos + jnp.concatenate(after, axis=1))
            if masked:
                w = jnp.where(valid, w, 0.0)
            wb = w.astype(BF16)
            asb_scr[...] += jnp.concatenate(
                [_dot(wb[8 * h:8 * h + 8, :], tok_rows(2 * h + 1)) for h in range(H_SB)], axis=0)
            carry_scr[...] = run

        if masked:
            stick_breaking()
        else:
            pl.when(jnp.max(carry_scr[...]) > SB_UNDERFLOW)(stick_breaking)

        r0 = 2 * H_SB
        s = jnp.concatenate([_dot_nt(qdf_ref[8 * h:8 * h + 8, :], tok_rows(r0 + 2 * h)) for h in range(H_DIFF)], axis=0)
        if masked:
            s = jnp.where(kpos <= qpos, s, NEG)
        m_old = mdf_scr[...]
        m_new = jnp.maximum(m_old, jnp.max(s, axis=-1, keepdims=True))
        eb = jnp.exp(s - m_new).astype(BF16)
        ones = jnp.ones((nk, HEAD_DIM), BF16)
        adf_scr[...] = jnp.exp(m_old - m_new) * adf_scr[...] + jnp.concatenate(
            [_dot(eb[8 * h:8 * h + 8, :], jnp.concatenate([tok_rows(r0 + 2 * h + 1), ones], axis=1))
             for h in range(H_DIFF)], axis=0)
        mdf_scr[...] = m_new

        r0 = 2 * (H_SB + H_DIFF + G_NSA)
        s = jnp.concatenate([_dot_nt(qsl_ref[16 * g:16 * g + 16, :], tok_rows(r0 + 2 * g)) for g in range(G_NSA)], axis=0)
        if masked:
            valid = kpos <= qpos
            s = jnp.where(valid, s, NEG)
        blk = lane // SLC_LEN
        col = lax.broadcasted_iota(jnp.int32, (DEC_ROWS, LANES), 1)
        mref = jnp.zeros((DEC_ROWS, nk), F32)
        mcols = jnp.full((DEC_ROWS, LANES), NEG, F32)
        for b in range(2 * n):
            mb = jnp.max(jnp.where(blk == b, s, NEG), axis=-1, keepdims=True)
            mref = jnp.where(blk == b, mb, mref)
            mcols = jnp.where(col == b, mb, mcols)
        e = jnp.exp(s - mref)
        if masked:
            e = jnp.where(valid, e, 0.0)
        eb = e.astype(BF16)
        sm_ref[...] = mcols
        sl_ref[...] = _dot(eb, bsel_ref[0:nk, :])
        blk16 = lax.broadcasted_iota(jnp.int32, (16, nk), 1) // SLC_LEN
        for g in range(G_NSA):
            eg = eb[16 * g:16 * g + 16, :]
            per_blk = jnp.concatenate([jnp.where(blk16 == b, eg, jnp.zeros_like(eg)) for b in range(2 * n)], axis=0)
            res = _dot(per_blk, tok_rows(r0 + 2 * g + 1))
            for b in range(2 * n):
                sacc_ref[b, 16 * g:16 * g + 16, :] = res[16 * b:16 * b + 16]
        for b in range(2 * n, 2 * pg):
            sacc_ref[b] = jnp.zeros((DEC_ROWS, HEAD_DIM), F32)

        r0 = 2 * (H_SB + H_DIFF)
        lhs = jnp.concatenate(
            [jnp.concatenate([p[pl.ds((CMP_STRIDE * m + l) * ROWS_PER_TOK + r0, 8), :] for l in range(CMP_STRIDE)], axis=1)
             for p in prefs for m in range(chunks_per_page)], axis=0).astype(BF16)
        res = _dot(lhs, w1_ref[...])
        crow = lax.broadcasted_iota(jnp.int32, (n * 8 * chunks_per_page, 2 * HEAD_DIM), 0)
        keep = jnp.where((crow & 1) == 0, res[:, :2 * HEAD_DIM], res[:, 2 * HEAD_DIM:])
        live = n * 8 * chunks_per_page
        for half in range(2):
            cab_ref[half, 0:live, :] = keep[:, half * HEAD_DIM:(half + 1) * HEAD_DIM]
            if n < pg:
                cab_ref[half, live:, :] = jnp.zeros(((pg - n) * 8 * chunks_per_page, HEAD_DIM), F32)

    @pl.when(j == 0)
    def _():
        step([new_ref], nps * pg, True)

    @pl.when(j > 0)
    def _():
        step(list(page_refs), (nps - j) * pg, False)

    @pl.when(j == nps)
    def _():
        osb_ref[...] = asb_scr[...]
        adf = adf_scr[...]
        odf_ref[...] = adf[:, :HEAD_DIM] / jnp.maximum(adf[:, HEAD_DIM:], 1e-30)


def _dec_main(layer, page_table, cache_rows, new_page, qsb, qdf, qsl, w1all, past, ns):
    nb, npages = page_table.shape
    pg = next(c for c in (8, 4, 2, 1) if npages % c == 0 and npages >= 2 * c)
    nps = npages // pg
    rows_per_page = PAGE_SIZE * ROWS_PER_TOK

    def page_spec(i):
        def imap(b, j, pt):
            return (layer, pt[b, (nps - jnp.maximum(j, 1)) * pg + i], 0, 0)
        return pl.BlockSpec((None, None, rows_per_page, HEAD_DIM), imap)

    qspec = pl.BlockSpec((None, DEC_ROWS, HEAD_DIM), lambda b, j, pt: (b, 0, 0))
    nblk_pad = (nps + 1) * 2 * pg
    cab_rows = pg * 8 * (PAGE_SIZE // CMP_STRIDE)
    key = np.arange(pg * PAGE_SIZE)[:, None]
    bsel = jnp.asarray((key // SLC_LEN == np.arange(LANES)[None, :]).astype(np.float32), dtype=BF16)
    stat_spec = pl.BlockSpec((None, None, DEC_ROWS, LANES), lambda b, j, pt: (b, nps - j, 0, 0))
    grid_spec = pltpu.PrefetchScalarGridSpec(
        num_scalar_prefetch=1,
        grid=(nb, nps + 1),
        in_specs=[page_spec(i) for i in range(pg)] + [
            pl.BlockSpec((None, rows_per_page, HEAD_DIM), lambda b, j, pt: (b, 0, 0)),
            qspec, qspec, qspec,
            pl.BlockSpec((PAGE_SIZE, PAGE_SIZE), lambda b, j, pt: (0, 0)),
            pl.BlockSpec(w1all.shape, lambda b, j, pt: (0, 0)),
            pl.BlockSpec(bsel.shape, lambda b, j, pt: (0, 0))],
        out_specs=[
            qspec, qspec,
            pl.BlockSpec((None, 2 * pg, DEC_ROWS, HEAD_DIM), lambda b, j, pt: (b, nps - j, 0, 0)),
            stat_spec, stat_spec,
            pl.BlockSpec((None, 2, cab_rows, HEAD_DIM), lambda b, j, pt: (b, 0, nps - j, 0))],
        scratch_shapes=[pltpu.VMEM((DEC_ROWS, 1), F32), pltpu.VMEM((DEC_ROWS, HEAD_DIM), F32),
                        pltpu.VMEM((DEC_ROWS, 1), F32), pltpu.VMEM((DEC_ROWS, 2 * HEAD_DIM), F32)])
    out_shape = [jax.ShapeDtypeStruct((nb, DEC_ROWS, HEAD_DIM), F32),
                 jax.ShapeDtypeStruct((nb, DEC_ROWS, HEAD_DIM), F32),
                 jax.ShapeDtypeStruct((nb, nblk_pad, DEC_ROWS, HEAD_DIM), F32),
                 jax.ShapeDtypeStruct((nb, nps + 1, DEC_ROWS, LANES), F32),
                 jax.ShapeDtypeStruct((nb, nps + 1, DEC_ROWS, LANES), F32),
                 jax.ShapeDtypeStruct((nb, 2, (nps + 1) * cab_rows, HEAD_DIM), F32)]
    o_sb, o_df, sacc, sm, sl, cab = pl.pallas_call(
        functools.partial(_dec_kernel, pg=pg, nps=nps, past=past, ns=ns),
        grid_spec=grid_spec,
        out_shape=out_shape,
        compiler_params=_cparams(("arbitrary", "arbitrary")),
        name="decode_paged",
    )(page_table, *([cache_rows] * pg), new_page, qsb, qdf, qsl, _suffix_matrix(PAGE_SIZE), w1all, bsel)

    def by_block(stat, fill):
        stat = stat[..., :2 * pg].transpose(0, 2, 1, 3).reshape(nb, DEC_ROWS, nblk_pad)
        return jnp.pad(stat, ((0, 0), (0, 0), (0, BLK_LANES - nblk_pad)), constant_values=fill)

    return o_sb, o_df, sacc, by_block(sm, NEG), by_block(sl, 0.0), cab


def _cmp_bias_kernel(pe_ref, w1_ref, o_ref):
    for kv in range(2):
        for c in range(2):
            o_ref[kv, c] = _dot(jnp.broadcast_to(pe_ref[kv, c:c + 1, :], (8, pe_ref.shape[-1])).astype(BF16),
                                w1_ref[kv, c].astype(BF16))


def _cmp_bias(pe, w1):
    return pl.pallas_call(
        _cmp_bias_kernel,
        out_shape=jax.ShapeDtypeStruct((2, 2, 8, HEAD_DIM), F32),
        compiler_params=pltpu.CompilerParams(vmem_limit_bytes=VMEM_LIMIT),
        name="cmp_bias",
    )(pe, w1)


def _dec_final_kernel(cab_ref, bias_ref, w2_ref, q_ref, imat_ref, jmat_ref, sacc_ref, sm_ref, sl_ref,
                      swin_ref, nwin_ref, ocm_ref, osl_ref, owi_ref, *, nc, past, ns, nblk_pad, wb):
    ncp = cab_ref.shape[1] // 8
    rows16 = lax.broadcasted_iota(jnp.int32, (16, 1), 0)
    qpos16 = past + (rows16 & (ns - 1))
    imps = []
    for g in range(G_NSA):
        q = q_ref[16 * g:16 * g + 16, :]
        cmp_kv = []
        for kv in range(2):
            a = cab_ref[0, pl.ds(2 * g + kv, ncp, stride=8), :]
            b = cab_ref[1, pl.ds(2 * g + kv, ncp, stride=8), :]
            h = a + pltpu.roll(b, ncp - 1, 0) + bias_ref[kv, 0, 0:1, :] + bias_ref[kv, 1, 0:1, :]
            cmp_kv.append(_dot(_silu(h).astype(BF16), w2_ref[kv].astype(BF16)).astype(BF16))
        kc, vc = cmp_kv
        s = _dot_nt(q, kc)
        n = lax.broadcasted_iota(jnp.int32, s.shape, 1)
        mask = (n * CMP_STRIDE + CMP_LEN - 1 <= qpos16) & (n < nc)
        s = jnp.where(mask, s, NEG)
        m = jnp.max(s, axis=-1, keepdims=True)
        e = jnp.where(mask, jnp.exp(s - m), 0.0)
        p = e / jnp.maximum(jnp.sum(e, axis=-1, keepdims=True), 1e-30)
        ocm_ref[16 * g:16 * g + 16, :] = _dot(p.astype(BF16), vc)
        hi, mid, lo = _split3(p)
        jm = jmat_ref[...]
        psum = _dot(jm, hi) + _dot(jm, mid) + _dot(jm, lo)
        imps.append(_dot_f32ish(psum, imat_ref[...]))
    imp = jnp.concatenate(imps, axis=0)
    blk = lax.broadcasted_iota(jnp.int32, imp.shape, 1)
    cur = (past + (lax.broadcasted_iota(jnp.int32, imp.shape, 0) & (ns - 1))) // SLC_LEN
    sel_all = _select_blocks(imp, cur, blk)
    for g in range(G_NSA):
        q = q_ref[16 * g:16 * g + 16, :]
        sel = sel_all[16 * g:16 * g + 16]
        mb = sm_ref[16 * g:16 * g + 16, :]
        lb = sl_ref[16 * g:16 * g + 16, :]
        picked = sel > 0.5
        mtot = jnp.max(jnp.where(picked, mb, NEG), axis=-1, keepdims=True)
        wgt = jnp.where(picked, jnp.exp(mb - mtot), 0.0)
        den = jnp.sum(wgt * lb, axis=-1, keepdims=True)
        wgt_b = wgt.astype(BF16)
        pad = jnp.zeros((BLK_LANES - nblk_pad, HEAD_DIM), F32)
        for half in range(2):
            rsel = lax.broadcasted_iota(jnp.int32, (8, HEAD_DIM), 0)
            num = jnp.zeros((8, HEAD_DIM), F32)
            for i in range(8):
                r32 = 16 * g + 8 * half + i
                acc = jnp.concatenate([sacc_ref[pl.ds(r32, nblk_pad, stride=DEC_ROWS), :], pad], axis=0)
                ahi = acc.astype(BF16)
                alo = (acc - ahi.astype(F32)).astype(BF16)
                w8 = wgt_b[8 * half:8 * half + 8, :]
                res = _dot(w8, ahi) + _dot(w8, alo)
                num = jnp.where(rsel == i, res, num)
            r0 = 16 * g + 8 * half
            osl_ref[r0:r0 + 8, :] = num / jnp.maximum(den[8 * half:8 * half + 8], 1e-30)
        kw = swin_ref[pl.ds(g, wb, stride=2 * G_NSA), :].astype(BF16)
        vw = swin_ref[pl.ds(G_NSA + g, wb, stride=2 * G_NSA), :].astype(BF16)
        kn = nwin_ref[pl.ds(g, 8, stride=2 * G_NSA), :].astype(BF16)
        vn = nwin_ref[pl.ds(G_NSA + g, 8, stride=2 * G_NSA), :].astype(BF16)
        s1 = _dot_nt(q, kw)
        s2 = _dot_nt(q, kn)
        d1 = qpos16 - (past - wb + lax.broadcasted_iota(jnp.int32, s1.shape, 1))
        k1 = (d1 >= 0) & (d1 <= WINDOW)
        d2 = qpos16 - (past + lax.broadcasted_iota(jnp.int32, s2.shape, 1))
        k2 = (d2 >= 0) & (d2 <= WINDOW)
        s1 = jnp.where(k1, s1, NEG)
        s2 = jnp.where(k2, s2, NEG)
        m = jnp.maximum(jnp.max(s1, axis=-1, keepdims=True), jnp.max(s2, axis=-1, keepdims=True))
        e1 = jnp.where(k1, jnp.exp(s1 - m), 0.0)
        e2 = jnp.where(k2, jnp.exp(s2 - m), 0.0)
        den = jnp.sum(e1, axis=-1, keepdims=True) + jnp.sum(e2, axis=-1, keepdims=True)
        owi_ref[16 * g:16 * g + 16, :] = (_dot(e1.astype(BF16), vw) + _dot(e2.astype(BF16), vn)) / jnp.maximum(den, 1e-30)


def _dec_final(layer, cab, bias, w2, qsl, sacc, sm, sl, state_rows, new_win, nc, past, ns):
    nb = cab.shape[0]
    ncp = cab.shape[2] // 8
    nblk_pad = sacc.shape[1] // DEC_ROWS
    wb = state_rows.shape[2] // (2 * G_NSA)
    imat = _importance_matrix(ncp, BLK_LANES)
    t_of = np.arange(16) % ns
    jmat = jnp.asarray((t_of[:, None] == t_of[None, :]).astype(np.float32), dtype=BF16)
    ospec = pl.BlockSpec((None, DEC_ROWS, HEAD_DIM), lambda b: (b, 0, 0))
    return pl.pallas_call(
        functools.partial(_dec_final_kernel, nc=nc, past=past, ns=ns, nblk_pad=nblk_pad, wb=wb),
        grid=(nb,),
        in_specs=[pl.BlockSpec((None, 2, ncp * 8, HEAD_DIM), lambda b: (b, 0, 0, 0)),
                  pl.BlockSpec(bias.shape, lambda b: (0, 0, 0, 0)),
                  pl.BlockSpec(w2.shape, lambda b: (0, 0, 0)),
                  ospec,
                  pl.BlockSpec(imat.shape, lambda b: (0, 0)),
                  pl.BlockSpec(jmat.shape, lambda b: (0, 0)),
                  pl.BlockSpec((None, nblk_pad * DEC_ROWS, HEAD_DIM), lambda b: (b, 0, 0)),
                  pl.BlockSpec((None, DEC_ROWS, BLK_LANES), lambda b: (b, 0, 0)),
                  pl.BlockSpec((None, DEC_ROWS, BLK_LANES), lambda b: (b, 0, 0)),
                  pl.BlockSpec((None, None, state_rows.shape[2], HEAD_DIM), lambda b: (layer, b, 0, 0)),
                  pl.BlockSpec((None, new_win.shape[1], HEAD_DIM), lambda b: (b, 0, 0))],
        out_specs=[ospec, ospec, ospec],
        out_shape=[jax.ShapeDtypeStruct((nb, DEC_ROWS, HEAD_DIM), F32)] * 3,
        compiler_params=_cparams(("arbitrary",)),
        name="decode_nsa_finish",
    )(cab, bias, w2, qsl, imat, jmat, sacc, sm, sl, state_rows, new_win)


def _sample_mixer(x, mod, lw, layer, tabs, page_table, cache_rows, state_rows, nb, ns, past):
    sh_a, sc_a, g_a = mod[0], mod[1], mod[2]
    proj = _proj(x, lw["n0"], sc_a, sh_a, lw["w_in"])
    (qsb, _, _, qdf, _, _, qns, _, _, _, _, _, _, gates, kv_rows, win_rows) = _postproj(proj, tabs)

    def per_seq(a, heads, rows_per_head):
        a = a.reshape(heads, nb, ns, HEAD_DIM).transpose(1, 0, 2, 3)
        a = jnp.pad(a, ((0, 0), (0, 0), (0, rows_per_head - ns), (0, 0)))
        return a.reshape(nb, heads * rows_per_head, HEAD_DIM)

    qsb_d = per_seq(qsb, H_SB, 8)
    qdf_d = per_seq(qdf, 2 * H_DIFF, ns)
    qsl_d = per_seq(qns, H_NSA, ns)
    kv_tok = kv_rows.reshape(nb, ns, 2, N_KV, HEAD_DIM).transpose(0, 1, 3, 2, 4).reshape(nb, ns * ROWS_PER_TOK, HEAD_DIM)
    new_page = jnp.pad(kv_tok, ((0, 0), (0, (PAGE_SIZE - ns) * ROWS_PER_TOK), (0, 0)))
    w1 = lw["cmp_w1"]
    w1all = jnp.concatenate([w1[0, 0], w1[0, 1], w1[1, 0], w1[1, 1]], axis=-1).astype(BF16)
    o_sb, o_df, sacc, sm, sl, cab = _dec_main(layer, page_table, cache_rows, new_page, qsb_d, qdf_d, qsl_d, w1all,
                                              past, ns)
    bias = _cmp_bias(lw["cmp_pe"], w1)
    new_win = jnp.pad(win_rows.reshape(nb, ns * 2 * G_NSA, HEAD_DIM), ((0, 0), (0, (8 - ns) * 2 * G_NSA), (0, 0)))
    padded = -(-(past + ns) // SLC_LEN) * SLC_LEN
    nc = padded // CMP_STRIDE - CMP_LEN // CMP_STRIDE + 1
    o_cmp, o_slc, o_win = _dec_final(layer, cab, bias, lw["cmp_w2"], qsl_d,
                                     sacc.reshape(nb, -1, HEAD_DIM), sm, sl, state_rows, new_win, nc, past, ns)

    def head_major(a, heads, rows_per_head):
        a = a.reshape(nb, heads, rows_per_head, HEAD_DIM)[:, :, :ns]
        return a.transpose(1, 0, 2, 3).reshape(heads, nb * ns, HEAD_DIM)

    x1 = _merge(head_major(o_sb, H_SB, 8), head_major(o_df, 2 * H_DIFF, ns), head_major(o_cmp, H_NSA, ns),
                head_major(o_slc, H_NSA, ns), head_major(o_win, H_NSA, ns), gates, lw["diff_lambda"],
                lw["diff_norm"], x, g_a, lw["n1"], lw["w_out"], layer)
    return x1, kv_rows, win_rows


def _split_mod(mod):
    return [mod[:, i * (mod.shape[1] // 6):(i + 1) * (mod.shape[1] // 6)] for i in range(6)]


def _prompt_mixer(x, mod, lw, layer, tabs):
    t = x.shape[0]
    sh_a, sc_a, g_a = mod[0], mod[1], mod[2]
    proj = _proj(x, lw["n0"], sc_a, sh_a, lw["w_in"])
    (qsb, ksb, vsb, qdf, kdf, vdf, qns, kc, vc, ks, vs, kw, vw, gates, kv_rows, win_rows) = _postproj(proj, tabs)
    o_sb = _sb_attention(qsb, ksb, vsb)
    o_df = _flash(qdf, kdf, vdf, "causal")
    nch = t // CMP_STRIDE
    xc = jnp.stack([kc, vc]).reshape(2, G_NSA, nch, CMP_STRIDE * HEAD_DIM)
    cmp_kv = _compress(xc, lw["cmp_pe"], lw["cmp_w1"], lw["cmp_w2"])
    o_cmp, sel = _cmp_select(qns, cmp_kv[0], cmp_kv[1], nch - 1)
    o_slc = _flash(qns, ks, vs, "select", sel)
    o_win = _flash(qns, kw, vw, "window")
    x1 = _merge(o_sb, o_df, o_cmp, o_slc, o_win, gates, lw["diff_lambda"], lw["diff_norm"], x, g_a, lw["n1"],
                lw["w_out"], layer)
    return x1, kv_rows, win_rows


def _layer_weights(layer, norms, w_in_p, w_out_b, diff_lambda, diff_norm, cmp_pe, cmp_w1, cmp_w2, wgu_b, wd_b):
    d = norms.shape[-1]
    return dict(
        n0=norms[layer, 0].reshape(1, d), n1=norms[layer, 1].reshape(1, d),
        n2=norms[layer, 2].reshape(1, d), n3=norms[layer, 3].reshape(1, d),
        w_in=w_in_p[layer], w_out=w_out_b[layer],
        diff_lambda=diff_lambda[layer], diff_norm=diff_norm[layer].reshape(1, HEAD_DIM),
        cmp_pe=cmp_pe[layer].reshape(2, 2, CMP_STRIDE * HEAD_DIM),
        cmp_w1=cmp_w1[layer].reshape(2, 2, CMP_STRIDE * HEAD_DIM, HEAD_DIM),
        cmp_w2=cmp_w2[layer], w_gate_up=wgu_b[layer], w_down=wd_b[layer])


def kernel(x_prompt, x_sample, cache_kv, state_win_kv, page_table, c_prompt, c_sample, ada_w, ada_b, norms, w_in,
           w_out, diff_lambda, diff_norm, cmp_pe, cmp_w1, cmp_w2, w_gate_up, w_down):
    depth = ada_w.shape[0]
    _, seq, d = x_prompt.shape
    nb, ns, _ = x_sample.shape
    past = page_table.shape[1] * PAGE_SIZE

    w_in_p = jnp.pad(w_in, ((0, 0), (0, 0), (0, N_IN_PAD - N_IN))).astype(BF16)
    w_out_b = w_out.astype(BF16)
    wgu_b = w_gate_up.astype(BF16)
    wd_b = w_down.astype(BF16)

    n_c = 1 + nb
    rows = -(-n_c // 8) * 8
    c_all = jnp.pad(jnp.concatenate([c_prompt, c_sample], axis=0), ((0, rows - n_c), (0, 0)))
    mod_all = _ada(c_all, ada_w, ada_b)

    tabs_p = _rope_tables(jnp.arange(seq))
    tabs_s = _rope_tables(past + jnp.arange(nb * ns) % ns)
    cache_rows = cache_kv.transpose(0, 1, 2, 4, 3, 5).reshape(depth, cache_kv.shape[1], PAGE_SIZE * ROWS_PER_TOK,
                                                               HEAD_DIM)
    wb_s = state_win_kv.shape[2]
    state_rows = state_win_kv.reshape(depth, nb, wb_s * 2 * G_NSA, HEAD_DIM)
    xp = x_prompt.reshape(seq, d)
    xs = x_sample.reshape(nb * ns, d)
    kv_p, win_p, kv_s, win_s = [], [], [], []
    for layer in range(depth):
        lw = _layer_weights(layer, norms, w_in_p, w_out_b, diff_lambda, diff_norm, cmp_pe, cmp_w1, cmp_w2, wgu_b, wd_b)
        mod_p = _split_mod(mod_all[layer, 0:1])
        x1, kv_rows, win_rows = _prompt_mixer(xp, mod_p, lw, layer, tabs_p)
        xp = _ffn(x1, lw["n2"], mod_p[4], mod_p[3], lw["w_gate_up"], lw["w_down"], mod_p[5], lw["n3"])
        kv_p.append(kv_rows.reshape(1, seq, 2, N_KV, HEAD_DIM))
        wb = min(WINDOW, seq)
        win_p.append(win_rows[seq - wb:].reshape(1, wb, 2, G_NSA, HEAD_DIM))

        mod_s = _split_mod(jnp.repeat(mod_all[layer, 1:1 + nb], ns, axis=0))
        x1, kv_rows, win_rows = _sample_mixer(xs, mod_s, lw, layer, tabs_s, page_table, cache_rows, state_rows,
                                              nb, ns, past)
        xs = _ffn(x1, lw["n2"], mod_s[4], mod_s[3], lw["w_gate_up"], lw["w_down"], mod_s[5], lw["n3"])
        kv_s.append(kv_rows.reshape(nb, ns, 2, N_KV, HEAD_DIM))
        win_new = win_rows.reshape(nb, ns, 2, G_NSA, HEAD_DIM)
        win_s.append(jnp.concatenate([state_win_kv[layer], win_new], axis=1)[:, ns:])
    return (xp.reshape(1, seq, d), xs.reshape(nb, ns, d), jnp.stack(kv_p), jnp.stack(kv_s), jnp.stack(win_p),
            jnp.stack(win_s))
```

```python
import functools
import math

import numpy as np
import jax
import jax.numpy as jnp
from jax import lax
from jax.experimental import pallas as pl
from jax.experimental.pallas import tpu as pltpu

F32 = jnp.float32
BF16 = jnp.bfloat16

HEAD_DIM = 128
H_SB = 4
H_DIFF = 4
H_NSA = 8
G_NSA = 2
HPG = H_NSA // G_NSA
N_KV = H_SB + H_DIFF + 2 * G_NSA
N_HEADS = H_SB + H_DIFF + H_NSA
DIFF_DIM = HEAD_DIM // 2
CMP_LEN = 32
CMP_STRIDE = 16
SLC_LEN = 64
SLC_TOPK = 16
WINDOW = 512
PAGE_SIZE = 128
ROPE_THETA = 10000.0
EPS = 1e-6
NEG = -1e30
LANES = 128
VMEM_LIMIT = 56 * 1024 * 1024

C_QA, C_KA, C_VA = 0, 512, 1024
C_QB, C_KB, C_VB = 1536, 2048, 2560
C_QC = 3072
C_KC, C_VC, C_KS, C_VS, C_KW, C_VW = 4096, 4352, 4608, 4864, 5120, 5376
C_GATE = 5632
N_IN = 5656
N_IN_PAD = 5760
ROWS_PER_TOK = 2 * N_KV


def _cparams(sem):
    return pltpu.CompilerParams(dimension_semantics=sem, vmem_limit_bytes=VMEM_LIMIT)


def _silu(x):
    return x * jax.nn.sigmoid(x)


def _rms_rows(x, g):
    return x * lax.rsqrt(jnp.mean(x * x, axis=-1, keepdims=True) + EPS) * g


def _dot(a, b):
    return jnp.dot(a, b, preferred_element_type=F32)


def _dot_nt(a, b):
    return lax.dot_general(a, b, (((1,), (1,)), ((), ())), preferred_element_type=F32)


def _split3(x):
    hi = x.astype(BF16)
    r1 = x - hi.astype(F32)
    mid = r1.astype(BF16)
    lo = (r1 - mid.astype(F32)).astype(BF16)
    return hi, mid, lo


def _dot_f32ish(x, m01):
    hi, mid, lo = _split3(x)
    return _dot(hi, m01) + _dot(mid, m01) + _dot(lo, m01)


def _ada_kernel(c_ref, w_ref, b_ref, o_ref):
    a = _silu(c_ref[...]).astype(BF16)
    o_ref[...] = _dot(a, w_ref[...].astype(BF16)) + b_ref[...]


def _ada(c_all, ada_w, ada_b):
    depth, d, n = ada_w.shape
    rows = c_all.shape[0]
    tn = 1024
    return pl.pallas_call(
        _ada_kernel,
        grid=(depth, n // tn),
        in_specs=[pl.BlockSpec((rows, d), lambda l, j: (0, 0)),
                  pl.BlockSpec((None, d, tn), lambda l, j: (l, 0, j)),
                  pl.BlockSpec((None, 1, tn), lambda l, j: (l, 0, j))],
        out_specs=pl.BlockSpec((None, rows, tn), lambda l, j: (l, 0, j)),
        out_shape=jax.ShapeDtypeStruct((depth, rows, n), F32),
        compiler_params=_cparams(("arbitrary", "arbitrary")),
        name="ada_mod",
    )(c_all, ada_w, ada_b.reshape(depth, 1, n))


def _row_spec(arr, tm, grid_rank):
    c = arr.shape[1]
    if arr.shape[0] == 1:
        if grid_rank == 1:
            return pl.BlockSpec((1, c), lambda i: (0, 0))
        return pl.BlockSpec((1, c), lambda i, j: (0, 0))
    if grid_rank == 1:
        return pl.BlockSpec((tm, c), lambda i: (i, 0))
    return pl.BlockSpec((tm, c), lambda i, j: (i, 0))


def _proj_kernel(x_ref, g_ref, sc_ref, sh_ref, w_ref, o_ref, h_scr):
    @pl.when(pl.program_id(1) == 0)
    def _():
        y = _rms_rows(x_ref[...], g_ref[...])
        h_scr[...] = (y * (1.0 + sc_ref[...]) + sh_ref[...]).astype(BF16)

    o_ref[...] = _dot(h_scr[...], w_ref[...])


def _proj(x, g, sc, sh, w):
    t, d = x.shape
    n = w.shape[1]
    tm = min(512, t)
    tn = 1920
    return pl.pallas_call(
        _proj_kernel,
        grid=(t // tm, n // tn),
        in_specs=[pl.BlockSpec((tm, d), lambda i, j: (i, 0)),
                  pl.BlockSpec((1, d), lambda i, j: (0, 0)),
                  _row_spec(sc, tm, 2), _row_spec(sh, tm, 2),
                  pl.BlockSpec((d, tn), lambda i, j: (0, j))],
        out_specs=pl.BlockSpec((tm, tn), lambda i, j: (i, j)),
        out_shape=jax.ShapeDtypeStruct((t, n), F32),
        scratch_shapes=[pltpu.VMEM((tm, d), BF16)],
        compiler_params=_cparams(("arbitrary", "arbitrary")),
        name="in_proj",
    )(x, g, sc, sh, w)


def _postproj_kernel(p_ref, ca_ref, sa_ref, cb_ref, sb_ref, *rest):
    (qsb_ref, ksb_ref, vsb_ref, qdf_ref, kdf_ref, vdf_ref, qns_ref,
     kc_ref, vc_ref, ks_ref, vs_ref, kw_ref, vw_ref, gate_ref, kv_ref, win_ref) = rest[-16:]
    tm = p_ref.shape[0]
    ca, sa, cb, sb = ca_ref[...], sa_ref[...], cb_ref[...], sb_ref[...]
    lane = lax.broadcasted_iota(jnp.int32, ca.shape, 1)
    low_half = (lane % DIFF_DIM) < (DIFF_DIM // 2)
    first_chunk = lane < DIFF_DIM
    sm_scale = HEAD_DIM ** -0.5
    df_scale = DIFF_DIM ** -0.5

    def col(c0, h):
        return p_ref[:, c0 + h * HEAD_DIM:c0 + (h + 1) * HEAD_DIM]

    def rope_a(x):
        return x * ca + pltpu.roll(x, HEAD_DIM // 2, 1) * sa

    def rope_b(x):
        partner = jnp.where(low_half, pltpu.roll(x, HEAD_DIM - DIFF_DIM // 2, 1), pltpu.roll(x, DIFF_DIM // 2, 1))
        return x * cb + partner * sb

    def put_kv(is_v, head, val):
        kv_ref[pl.ds(2 * head + is_v, tm, stride=ROWS_PER_TOK), :] = val

    for h in range(H_SB):
        qsb_ref[h] = (col(C_QA, h) * sm_scale).astype(BF16)
        k, v = col(C_KA, h), col(C_VA, h)
        ksb_ref[h] = k.astype(BF16)
        vsb_ref[h] = v.astype(BF16)
        put_kv(0, h, k)
        put_kv(1, h, v)
    for h in range(H_DIFF):
        q = rope_b(col(C_QB, h)) * df_scale
        qdf_ref[2 * h] = jnp.where(first_chunk, q, 0.0).astype(BF16)
        qdf_ref[2 * h + 1] = jnp.where(first_chunk, 0.0, q).astype(BF16)
        k, v = rope_b(col(C_KB, h)), col(C_VB, h)
        kdf_ref[h] = k.astype(BF16)
        vdf_ref[h] = v.astype(BF16)
        put_kv(0, H_SB + h, k)
        put_kv(1, H_SB + h, v)
    for h in range(H_NSA):
        qns_ref[h] = (rope_a(col(C_QC, h)) * sm_scale).astype(BF16)
    for g in range(G_NSA):
        k, v = rope_a(col(C_KC, g)), col(C_VC, g)
        kc_ref[g] = k
        vc_ref[g] = v
        put_kv(0, H_SB + H_DIFF + g, k)
        put_kv(1, H_SB + H_DIFF + g, v)
        k, v = rope_a(col(C_KS, g)), col(C_VS, g)
        ks_ref[g] = k.astype(BF16)
        vs_ref[g] = v.astype(BF16)
        put_kv(0, H_SB + H_DIFF + G_NSA + g, k)
        put_kv(1, H_SB + H_DIFF + G_NSA + g, v)
        k, v = rope_a(col(C_KW, g)), col(C_VW, g)
        kw_ref[g] = k.astype(BF16)
        vw_ref[g] = v.astype(BF16)
        win_ref[:, g * HEAD_DIM:(g + 1) * HEAD_DIM] = k
        win_ref[:, (G_NSA + g) * HEAD_DIM:(G_NSA + g + 1) * HEAD_DIM] = v
    gate_ref[...] = jax.nn.sigmoid(p_ref[:, C_GATE:C_GATE + LANES])


def _postproj(proj, tabs, layer=0, depth=1, kv_stack=None):
    t = proj.shape[0]
    tm = min(256, t)
    hm = lambda n, dt: jax.ShapeDtypeStruct((n, t, HEAD_DIM), dt)
    hspec = lambda n: pl.BlockSpec((n, tm, HEAD_DIM), lambda i: (0, i, 0))
    tab_spec = pl.BlockSpec((tm, LANES), lambda i: (i, 0))
    out_shapes = [hm(H_SB, BF16), hm(H_SB, BF16), hm(H_SB, BF16),
                  hm(2 * H_DIFF, BF16), hm(H_DIFF, BF16), hm(H_DIFF, BF16), hm(H_NSA, BF16),
                  hm(G_NSA, F32), hm(G_NSA, F32), hm(G_NSA, BF16), hm(G_NSA, BF16),
                  hm(G_NSA, BF16), hm(G_NSA, BF16),
                  jax.ShapeDtypeStruct((t, LANES), F32),
                  jax.ShapeDtypeStruct((depth, t * ROWS_PER_TOK, HEAD_DIM), F32),
                  jax.ShapeDtypeStruct((t, 2 * G_NSA * HEAD_DIM), F32)]
    out_specs = [hspec(H_SB), hspec(H_SB), hspec(H_SB), hspec(2 * H_DIFF), hspec(H_DIFF), hspec(H_DIFF),
                 hspec(H_NSA), hspec(G_NSA), hspec(G_NSA), hspec(G_NSA), hspec(G_NSA), hspec(G_NSA), hspec(G_NSA),
                 pl.BlockSpec((tm, LANES), lambda i: (i, 0)),
                 pl.BlockSpec((None, tm * ROWS_PER_TOK, HEAD_DIM), lambda i: (layer, i, 0)),
                 pl.BlockSpec((tm, 2 * G_NSA * HEAD_DIM), lambda i: (i, 0))]
    in_specs = [pl.BlockSpec((tm, N_IN_PAD), lambda i: (i, 0)), tab_spec, tab_spec, tab_spec, tab_spec]
    args = [proj, *tabs]
    aliases = {}
    if kv_stack is not None:
        in_specs.append(pl.BlockSpec(memory_space=pl.ANY))
        args.append(kv_stack)
        aliases = {len(args) - 1: 14}
    return pl.pallas_call(
        _postproj_kernel,
        grid=(t // tm,),
        in_specs=in_specs,
        out_specs=out_specs,
        out_shape=out_shapes,
        input_output_aliases=aliases,
        compiler_params=_cparams(("arbitrary",)),
        name="rope_split",
    )(*args)


def _rope_tables(pos):
    posf = pos.astype(F32)[:, None]

    def tab(half, reps):
        inv = jnp.power(ROPE_THETA, -jnp.arange(half, dtype=F32) / half)
        ang = posf * inv[None, :]
        c, s = jnp.cos(ang), jnp.sin(ang)
        return jnp.tile(jnp.concatenate([c, c], 1), (1, reps)), jnp.tile(jnp.concatenate([-s, s], 1), (1, reps))

    ca, sa = tab(HEAD_DIM // 2, 1)
    cb, sb = tab(DIFF_DIM // 2, 2)
    return ca, sa, cb, sb


SB_UNDERFLOW = -104.0


def _sb_kernel(q_ref, k_ref, v_ref, u_ref, o_ref, carry_scr, acc_scr, *, tq, tk):
    qi = pl.program_id(1)
    carry_scr[...] = jnp.zeros_like(carry_scr)
    acc_scr[...] = jnp.zeros_like(acc_scr)
    q = q_ref[...]
    u = u_ref[...]
    row = lax.broadcasted_iota(jnp.int32, (tq, tk), 0)
    col = lax.broadcasted_iota(jnp.int32, (tq, tk), 1)

    def body(state):
        kj, _ = state
        off = pl.multiple_of(kj * tk, tk)
        z = _dot_nt(q, k_ref[pl.ds(off, tk), :])
        mask = (off + col) < (qi * tq + row)
        l1p = jnp.log(1.0 + jnp.exp(-jnp.abs(z)))
        ls_pos = jnp.minimum(z, 0.0) - l1p
        lk = jnp.where(mask, ls_pos - z, 0.0)
        hi = lk.astype(BF16)
        lo = (lk - hi.astype(F32)).astype(BF16)
        after = _dot(hi, u) + _dot(lo, u) + carry_scr[...]
        w = jnp.where(mask, jnp.exp(ls_pos + after), 0.0)
        acc_scr[...] += _dot(w.astype(BF16), v_ref[pl.ds(off, tk), :])
        carry = carry_scr[...] + jnp.sum(lk, axis=-1, keepdims=True)
        carry_scr[...] = carry
        go = (jnp.max(carry) > SB_UNDERFLOW).astype(jnp.int32)
        return kj - 1, go

    last_tile = (qi + 1) * (tq // tk) - 1
    lax.while_loop(lambda st: (st[0] >= 0) & (st[1] > 0), body, (last_tile, jnp.int32(1)))
    o_ref[...] = acc_scr[...]


def _suffix_matrix(n):
    j = np.arange(n)[:, None]
    s = np.arange(n)[None, :]
    return jnp.asarray((j > s).astype(np.float32), dtype=BF16)


def _sb_attention(q, k, v):
    h, t, d = q.shape
    tq = min(256, t)
    tk = min(128, t)
    return pl.pallas_call(
        functools.partial(_sb_kernel, tq=tq, tk=tk),
        grid=(h, t // tq),
        in_specs=[pl.BlockSpec((None, tq, d), lambda hh, i: (hh, i, 0)),
                  pl.BlockSpec((None, t, d), lambda hh, i: (hh, 0, 0)),
                  pl.BlockSpec((None, t, d), lambda hh, i: (hh, 0, 0)),
                  pl.BlockSpec((tk, tk), lambda hh, i: (0, 0))],
        out_specs=pl.BlockSpec((None, tq, d), lambda hh, i: (hh, i, 0)),
        out_shape=jax.ShapeDtypeStruct((h, t, d), F32),
        scratch_shapes=[pltpu.VMEM((tq, 1), F32), pltpu.VMEM((tq, d), F32)],
        compiler_params=_cparams(("arbitrary", "arbitrary")),
        name="sb_attn",
    )(q, k, v, _suffix_matrix(tk))


SEL_BIAS = 16384.0


def _flash_kernel(q_ref, k_ref, v_ref, *rest, mode, rep, tq, tk):
    if mode == "select":
        bias_ref, et_ref, o_ref, m_scr, acc_scr, s_scr, s2_scr = rest
    else:
        o_ref, m_scr, acc_scr, s_scr, s2_scr = rest
    qi = pl.program_id(1)
    rows = rep * tq
    m_scr[...] = jnp.full_like(m_scr, NEG)
    acc_scr[...] = jnp.zeros_like(acc_scr)
    q = q_ref[...].reshape(rows, HEAD_DIM)
    if mode == "select":
        q = jnp.concatenate([q, jnp.concatenate([bias_ref[...]] * rep, axis=0)], axis=1)
    ones = jnp.ones((tk, HEAD_DIM), BF16)
    qpos = qi * tq + (lax.broadcasted_iota(jnp.int32, (rows, tk), 0) & (tq - 1))
    col = lax.broadcasted_iota(jnp.int32, (rows, tk), 1)

    newest = ((qi + 1) * tq - 1) // tk
    if mode == "window":
        oldest = jnp.maximum(qi * tq - WINDOW, 0) // tk
        n_tiles = newest - oldest + 1
        tile_of = lambda i: newest - i
    else:
        n_tiles = newest + 1
        tile_of = lambda i: i
    n_full = (qi * tq + 1) // tk

    def scores(i):
        off = pl.multiple_of(tile_of(jnp.minimum(i, n_tiles - 1)) * tk, tk)
        k = k_ref[pl.ds(off, tk), :]
        if mode == "select":
            k = jnp.concatenate([k, et_ref[pl.ds(off, tk), :]], axis=1)
        return _dot_nt(q, k)

    def masked_scores(s, off):
        dist = qpos - (off + col)
        mask = (dist >= 0) & (dist <= WINDOW) if mode == "window" else dist >= 0
        return jnp.where(mask, s, NEG)

    def visit(src, dst, i):
        j = tile_of(i)
        off = pl.multiple_of(j * tk, tk)
        if mode == "window":
            src[...] = masked_scores(src[...], off)
        else:
            @pl.when(j >= n_full)
            def _():
                src[...] = masked_scores(src[...], off)
        dst[...] = scores(i + 1)
        s = src[...]
        m_old = m_scr[...]
        m_new = jnp.maximum(m_old, jnp.max(s, axis=-1, keepdims=True))
        e = jnp.exp(s - m_new)
        v_ext = jnp.concatenate([v_ref[pl.ds(off, tk), :], ones], axis=1)
        acc_scr[...] = jnp.exp(m_old - m_new) * acc_scr[...] + _dot(e.astype(BF16), v_ext)
        m_scr[...] = m_new

    def body(p, carry):
        visit(s_scr, s2_scr, 2 * p)
        pl.when(2 * p + 1 < n_tiles)(lambda: visit(s2_scr, s_scr, 2 * p + 1))
        return carry

    s_scr[...] = scores(0)
    lax.fori_loop(0, (n_tiles + 1) // 2, body, 0)
    acc = acc_scr[...]
    out = acc[:, :HEAD_DIM] / jnp.maximum(acc[:, HEAD_DIM:], 1e-30)
    o_ref[...] = out.reshape(rep, tq, HEAD_DIM)


def _block_onehot(t):
    key = np.arange(t)[:, None]
    b = np.arange(LANES)[None, :]
    return jnp.asarray((key // SLC_LEN == b).astype(np.float32), dtype=BF16)


def _flash(q, k, v, mode, sel_bias=None):
    hq, t, d = q.shape
    hk = k.shape[0]
    rep = hq // hk
    tq = min(512 // rep, t)
    tk = min({"window": 256, "select": 512, "causal": 1024}[mode], t)
    in_specs = [pl.BlockSpec((rep, tq, d), lambda g, i: (g, i, 0)),
                pl.BlockSpec((None, t, d), lambda g, i: (g, 0, 0)),
                pl.BlockSpec((None, t, d), lambda g, i: (g, 0, 0))]
    args = [q, k, v]
    if mode == "select":
        in_specs += [pl.BlockSpec((None, tq, LANES), lambda g, i: (g, i, 0)),
                     pl.BlockSpec((t, LANES), lambda g, i: (0, 0))]
        args += [sel_bias, _block_onehot(t)]
    return pl.pallas_call(
        functools.partial(_flash_kernel, mode=mode, rep=rep, tq=tq, tk=tk),
        grid=(hk, t // tq),
        in_specs=in_specs,
        out_specs=pl.BlockSpec((rep, tq, d), lambda g, i: (g, i, 0)),
        out_shape=jax.ShapeDtypeStruct((hq, t, d), F32),
        scratch_shapes=[pltpu.VMEM((rep * tq, 1), F32), pltpu.VMEM((rep * tq, 2 * d), F32),
                        pltpu.VMEM((rep * tq, tk), F32), pltpu.VMEM((rep * tq, tk), F32)],
        compiler_params=_cparams(("arbitrary", "arbitrary")),
        name="attn_" + mode,
    )(*args)


def _compress_kernel(x_ref, pe_ref, w1_ref, w2_ref, o_ref):
    x = x_ref[...]
    n = x.shape[0]
    a = _dot((x + pe_ref[0:1, :]).astype(BF16), w1_ref[0].astype(BF16))
    b = _dot((x + pe_ref[1:2, :]).astype(BF16), w1_ref[1].astype(BF16))
    h = a + pltpu.roll(b, n - 1, 0)
    o_ref[...] = _dot(_silu(h).astype(BF16), w2_ref[...].astype(BF16)).astype(o_ref.dtype)


def _compress(xc, pe, w1, w2):
    _, g, n, wdt = xc.shape
    return pl.pallas_call(
        _compress_kernel,
        grid=(2, g),
        in_specs=[pl.BlockSpec((None, None, n, wdt), lambda a, b: (a, b, 0, 0)),
                  pl.BlockSpec((None, 2, wdt), lambda a, b: (a, 0, 0)),
                  pl.BlockSpec((None, 2, wdt, HEAD_DIM), lambda a, b: (a, 0, 0, 0)),
                  pl.BlockSpec((None, HEAD_DIM, HEAD_DIM), lambda a, b: (a, 0, 0))],
        out_specs=pl.BlockSpec((None, None, n, HEAD_DIM), lambda a, b: (a, b, 0, 0)),
        out_shape=jax.ShapeDtypeStruct((2, g, n, HEAD_DIM), BF16),
        compiler_params=_cparams(("arbitrary", "arbitrary")),
        name="nsa_compress",
    )(xc, pe, w1, w2)


def _importance_matrix(nc_pad, nblk_pad):
    r = SLC_LEN // CMP_STRIDE
    c = CMP_LEN // CMP_STRIDE
    i = np.arange(nc_pad)[:, None]
    b = np.arange(nblk_pad)[None, :]
    lo = r * b - (c - 1)
    return jnp.asarray(((i >= lo) & (i <= lo + r + c - 2)).astype(np.float32), dtype=BF16)


def _select_blocks(imp, cur, blk):
    forced = (blk == 0) | (blk == cur) | (blk == cur - 1)
    score = jnp.where(forced, 1e30, jnp.where(blk <= cur, imp, NEG))
    sel = jnp.zeros_like(score)
    blkf = blk.astype(F32)
    for _ in range(SLC_TOPK):
        mx = jnp.max(score, axis=-1, keepdims=True)
        idx = jnp.min(jnp.where(score == mx, blkf, 1e9), axis=-1, keepdims=True)
        pick = blkf == idx
        sel = jnp.where(pick, 1.0, sel)
        score = jnp.where(pick, -3e38, score)
    return sel


def _cmp_kernel(q_ref, kc_ref, vc_ref, imat_ref, o_ref, sel_ref, *, tq, nc):
    i = pl.program_id(0)
    rows = HPG * tq
    ncp = kc_ref.shape[1]
    row = lax.broadcasted_iota(jnp.int32, (rows, ncp), 0)
    qpos = i * tq + (row & (tq - 1))
    n = lax.broadcasted_iota(jnp.int32, (rows, ncp), 1)
    mask = (n * CMP_STRIDE + CMP_LEN - 1 <= qpos) & (n < nc)
    imps = []
    for g in range(G_NSA):
        q = q_ref[HPG * g:HPG * (g + 1)].reshape(rows, HEAD_DIM)
        s = jnp.where(mask, _dot_nt(q, kc_ref[g]), NEG)
        m = jnp.max(s, axis=-1, keepdims=True)
        e = jnp.where(mask, jnp.exp(s - m), 0.0)
        p = e / jnp.maximum(jnp.sum(e, axis=-1, keepdims=True), 1e-30)
        o_ref[HPG * g:HPG * (g + 1)] = _dot(p.astype(BF16), vc_ref[g]).reshape(HPG, tq, HEAD_DIM)
        psum = p[0:tq]
        for h in range(1, HPG):
            psum = psum + p[h * tq:(h + 1) * tq]
        imps.append(_dot_f32ish(psum, imat_ref[...]))
    imp = jnp.concatenate(imps, axis=0)
    blk = lax.broadcasted_iota(jnp.int32, imp.shape, 1)
    cur = (i * tq + (lax.broadcasted_iota(jnp.int32, imp.shape, 0) & (tq - 1))) // SLC_LEN
    bias = ((_select_blocks(imp, cur, blk) - 1.0) * SEL_BIAS).astype(BF16)
    for g in range(G_NSA):
        sel_ref[g] = bias[tq * g:tq * (g + 1)]


def _cmp_select(q, kcmp, vcmp, nc):
    _, t, d = q.shape
    g, ncp, _ = kcmp.shape
    tq = min(256, t)
    imat = _importance_matrix(ncp, LANES)
    return pl.pallas_call(
        functools.partial(_cmp_kernel, tq=tq, nc=nc),
        grid=(t // tq,),
        in_specs=[pl.BlockSpec((H_NSA, tq, d), lambda i: (0, i, 0)),
                  pl.BlockSpec((g, ncp, d), lambda i: (0, 0, 0)),
                  pl.BlockSpec((g, ncp, d), lambda i: (0, 0, 0)),
                  pl.BlockSpec((ncp, LANES), lambda i: (0, 0))],
        out_specs=[pl.BlockSpec((H_NSA, tq, d), lambda i: (0, i, 0)),
                   pl.BlockSpec((g, tq, LANES), lambda i: (0, i, 0))],
        out_shape=[jax.ShapeDtypeStruct((H_NSA, t, d), F32), jax.ShapeDtypeStruct((g, t, LANES), BF16)],
        compiler_params=_cparams(("arbitrary",)),
        name="nsa_cmp_select",
    )(q, kcmp, vcmp, imat)


def _merge_kernel(osb_ref, odf_ref, ocm_ref, osl_ref, owi_ref, gate_ref, lam_ref, dn_ref, x_ref, ga_ref, nrm_ref,
                  w_ref, o_ref, cat_scr, *, lam_init):
    lv = lam_ref[...]
    lam = (jnp.exp(jnp.sum(lv[0:1] * lv[1:2], axis=-1, keepdims=True))
           - jnp.exp(jnp.sum(lv[2:3] * lv[3:4], axis=-1, keepdims=True)) + lam_init)
    for h in range(H_SB):
        cat_scr[:, h * HEAD_DIM:(h + 1) * HEAD_DIM] = osb_ref[h].astype(BF16)
    dn = dn_ref[...]
    for h in range(H_DIFF):
        ob = odf_ref[2 * h] - lam * odf_ref[2 * h + 1]
        ob = _rms_rows(ob, dn) * (1.0 - lam_init)
        c0 = (H_SB + h) * HEAD_DIM
        cat_scr[:, c0:c0 + HEAD_DIM] = ob.astype(BF16)
    gates = gate_ref[...]
    for h in range(H_NSA):
        oc = (gates[:, 3 * h:3 * h + 1] * ocm_ref[h] + gates[:, 3 * h + 1:3 * h + 2] * osl_ref[h]
              + gates[:, 3 * h + 2:3 * h + 3] * owi_ref[h])
        c0 = (H_SB + H_DIFF + h) * HEAD_DIM
        cat_scr[:, c0:c0 + HEAD_DIM] = oc.astype(BF16)
    y = _dot(cat_scr[...], w_ref[...])
    o_ref[...] = x_ref[...] + ga_ref[...] * _rms_rows(y, nrm_ref[...])


def _merge(osb, odf, ocm, osl, owi, gates, lam_vecs, diff_norm, x, ga, nrm, w, layer):
    t, d = x.shape
    tm = min(256, t)
    lam_init = 0.8 - 0.6 * math.exp(-0.3 * layer)
    hspec = lambda n: pl.BlockSpec((n, tm, HEAD_DIM), lambda i: (0, i, 0))
    return pl.pallas_call(
        functools.partial(_merge_kernel, lam_init=lam_init),
        grid=(t // tm,),
        in_specs=[hspec(H_SB), hspec(2 * H_DIFF), hspec(H_NSA), hspec(H_NSA), hspec(H_NSA),
                  pl.BlockSpec((tm, LANES), lambda i: (i, 0)),
                  pl.BlockSpec(lam_vecs.shape, lambda i: (0, 0)),
                  pl.BlockSpec((1, HEAD_DIM), lambda i: (0, 0)),
                  pl.BlockSpec((tm, d), lambda i: (i, 0)),
                  _row_spec(ga, tm, 1),
                  pl.BlockSpec((1, d), lambda i: (0, 0)),
                  pl.BlockSpec(w.shape, lambda i: (0, 0))],
        out_specs=pl.BlockSpec((tm, d), lambda i: (i, 0)),
        out_shape=jax.ShapeDtypeStruct((t, d), F32),
        scratch_shapes=[pltpu.VMEM((tm, N_HEADS * HEAD_DIM), BF16)],
        compiler_params=_cparams(("arbitrary",)),
        name="merge_out_proj",
    )(osb, odf, ocm, osl, owi, gates, lam_vecs, diff_norm, x, ga, nrm, w)


def _ffn_kernel(x_ref, g2_ref, sc_ref, sh_ref, wg_ref, wu_ref, wd_ref, gf_ref, g3_ref, o_ref, h_scr, acc_scr):
    f = pl.program_id(1)

    @pl.when(f == 0)
    def _():
        y = _rms_rows(x_ref[...], g2_ref[...])
        h_scr[...] = (y * (1.0 + sc_ref[...]) + sh_ref[...]).astype(BF16)
        acc_scr[...] = jnp.zeros_like(acc_scr)

    h = h_scr[...]
    a = _dot(h, wg_ref[...])
    b = _dot(h, wu_ref[...])
    acc_scr[...] += _dot((_silu(a) * b).astype(BF16), wd_ref[...])

    @pl.when(f == pl.num_programs(1) - 1)
    def _():
        o_ref[...] = x_ref[...] + gf_ref[...] * _rms_rows(acc_scr[...], g3_ref[...])


def _ffn(x, g2, sc, sh, wgu, wd, gf, g3):
    t, d = x.shape
    dff = wd.shape[0]
    tm = min(512, t)
    tf = 512 if dff % 512 == 0 else dff
    nf = dff // tf
    return pl.pallas_call(
        _ffn_kernel,
        grid=(t // tm, nf),
        in_specs=[pl.BlockSpec((tm, d), lambda i, f: (i, 0)),
                  pl.BlockSpec((1, d), lambda i, f: (0, 0)),
                  _row_spec(sc, tm, 2), _row_spec(sh, tm, 2),
                  pl.BlockSpec((d, tf), lambda i, f: (0, f)),
                  pl.BlockSpec((d, tf), lambda i, f: (0, f + nf)),
                  pl.BlockSpec((tf, d), lambda i, f: (f, 0)),
                  _row_spec(gf, tm, 2),
                  pl.BlockSpec((1, d), lambda i, f: (0, 0))],
        out_specs=pl.BlockSpec((tm, d), lambda i, f: (i, 0)),
        out_shape=jax.ShapeDtypeStruct((t, d), F32),
        scratch_shapes=[pltpu.VMEM((tm, d), BF16), pltpu.VMEM((tm, d), F32)],
        compiler_params=_cparams(("arbitrary", "arbitrary")),
        name="ffn_swiglu",
    )(x, g2, sc, sh, wgu, wgu, wd, gf, g3)


DEC_ROWS = 32
BLK_LANES = 256


def _dec_kernel(pt_ref, *refs, pg, nps, past, ns):
    page_refs = refs[:pg]
    (new_ref, qsb_ref, qdf_ref, qsl_ref, u_ref, w1_ref, bsel_ref,
     osb_ref, odf_ref, sacc_ref, sm_ref, sl_ref, cab_ref,
     carry_scr, asb_scr, mdf_scr, adf_scr) = refs[pg:]
    j = pl.program_id(1)
    chunks_per_page = PAGE_SIZE // CMP_STRIDE

    @pl.when(j == 0)
    def _():
        carry_scr[...] = jnp.zeros_like(carry_scr)
        asb_scr[...] = jnp.zeros_like(asb_scr)
        mdf_scr[...] = jnp.full_like(mdf_scr, NEG)
        adf_scr[...] = jnp.zeros_like(adf_scr)

    def step(prefs, page0, masked):
        n = len(prefs)
        nk = n * PAGE_SIZE
        lane = lax.broadcasted_iota(jnp.int32, (DEC_ROWS, nk), 1)
        row = lax.broadcasted_iota(jnp.int32, (DEC_ROWS, nk), 0)
        kpos = page0 * PAGE_SIZE + lane
        qpos = past + (row & (ns - 1))

        def tok_rows(r):
            return jnp.concatenate([p[pl.ds(r, PAGE_SIZE, stride=ROWS_PER_TOK), :] for p in prefs], axis=0).astype(BF16)

        def stick_breaking():
            z = jnp.concatenate([_dot_nt(qsb_ref[8 * h:8 * h + 8, :], tok_rows(2 * h)) for h in range(H_SB)], axis=0)
            l1p = jnp.log(1.0 + jnp.exp(-jnp.abs(z)))
            ls_pos = jnp.minimum(z, 0.0) - l1p
            lk = ls_pos - z
            if masked:
                valid = kpos < qpos
                lk = jnp.where(valid, lk, 0.0)
            hi = lk.astype(BF16)
            lo = (lk - hi.astype(F32)).astype(BF16)
            pieces = ([hi[:, PAGE_SIZE * i:PAGE_SIZE * (i + 1)] for i in range(n)]
                      + [lo[:, PAGE_SIZE * i:PAGE_SIZE * (i + 1)] for i in range(n)])
            local = _dot(jnp.concatenate(pieces, axis=0), u_ref[...])
            run = carry_scr[...]
            after = [None] * n
            for i in reversed(range(n)):
                after[i] = (local[DEC_ROWS * i:DEC_ROWS * (i + 1)] + local[DEC_ROWS * (n + i):DEC_ROWS * (n + i + 1)]
                            + run)
                run = run + jnp.sum(lk[:, PAGE_SIZE * i:PAGE_SIZE * (i + 1)], axis=-1, keepdims=True)
            w = jnp.exp(ls_pos + jnp.concatenate(after, axis=1))
            if masked:
                w = jnp.where(valid, w, 0.0)
            wb = w.astype(BF16)
            asb_scr[...] += jnp.concatenate(
                [_dot(wb[8 * h:8 * h + 8, :], tok_rows(2 * h + 1)) for h in range(H_SB)], axis=0)
            carry_scr[...] = run

        if masked:
            stick_breaking()
        else:
            pl.when(jnp.max(carry_scr[...]) > SB_UNDERFLOW)(stick_breaking)

        r0 = 2 * H_SB
        s = jnp.concatenate([_dot_nt(qdf_ref[8 * h:8 * h + 8, :], tok_rows(r0 + 2 * h)) for h in range(H_DIFF)], axis=0)
        if masked:
            s = jnp.where(kpos <= qpos, s, NEG)
        m_old = mdf_scr[...]
        m_new = jnp.maximum(m_old, jnp.max(s, axis=-1, keepdims=True))
        eb = jnp.exp(s - m_new).astype(BF16)
        ones = jnp.ones((nk, HEAD_DIM), BF16)
        adf_scr[...] = jnp.exp(m_old - m_new) * adf_scr[...] + jnp.concatenate(
            [_dot(eb[8 * h:8 * h + 8, :], jnp.concatenate([tok_rows(r0 + 2 * h + 1), ones], axis=1))
             for h in range(H_DIFF)], axis=0)
        mdf_scr[...] = m_new

        r0 = 2 * (H_SB + H_DIFF + G_NSA)
        s = jnp.concatenate([_dot_nt(qsl_ref[16 * g:16 * g + 16, :], tok_rows(r0 + 2 * g)) for g in range(G_NSA)], axis=0)
        if masked:
            valid = kpos <= qpos
            s = jnp.where(valid, s, NEG)
        blk = lane // SLC_LEN
        col = lax.broadcasted_iota(jnp.int32, (DEC_ROWS, LANES), 1)
        mref = jnp.zeros((DEC_ROWS, nk), F32)
        mcols = jnp.full((DEC_ROWS, LANES), NEG, F32)
        for b in range(2 * n):
            mb = jnp.max(jnp.where(blk == b, s, NEG), axis=-1, keepdims=True)
            mref = jnp.where(blk == b, mb, mref)
            mcols = jnp.where(col == b, mb, mcols)
        e = jnp.exp(s - mref)
        if masked:
            e = jnp.where(valid, e, 0.0)
        eb = e.astype(BF16)
        sm_ref[...] = mcols
        sl_ref[...] = _dot(eb, bsel_ref[0:nk, :])
        blk16 = lax.broadcasted_iota(jnp.int32, (16, nk), 1) // SLC_LEN
        for g in range(G_NSA):
            eg = eb[16 * g:16 * g + 16, :]
            per_blk = jnp.concatenate([jnp.where(blk16 == b, eg, jnp.zeros_like(eg)) for b in range(2 * n)], axis=0)
            res = _dot(per_blk, tok_rows(r0 + 2 * g + 1))
            for b in range(2 * n):
                sacc_ref[b, 16 * g:16 * g + 16, :] = res[16 * b:16 * b + 16]
        for b in range(2 * n, 2 * pg):
            sacc_ref[b] = jnp.zeros((DEC_ROWS, HEAD_DIM), F32)

        r0 = 2 * (H_SB + H_DIFF)
        lhs = jnp.concatenate(
            [jnp.concatenate([p[pl.ds((CMP_STRIDE * m + l) * ROWS_PER_TOK + r0, 8), :] for l in range(CMP_STRIDE)], axis=1)
             for p in prefs for m in range(chunks_per_page)], axis=0).astype(BF16)
        res = _dot(lhs, w1_ref[...])
        crow = lax.broadcasted_iota(jnp.int32, (n * 8 * chunks_per_page, 2 * HEAD_DIM), 0)
        keep = jnp.where((crow & 1) == 0, res[:, :2 * HEAD_DIM], res[:, 2 * HEAD_DIM:])
        live = n * 8 * chunks_per_page
        for half in range(2):
            cab_ref[half, 0:live, :] = keep[:, half * HEAD_DIM:(half + 1) * HEAD_DIM]
            if n < pg:
                cab_ref[half, live:, :] = jnp.zeros(((pg - n) * 8 * chunks_per_page, HEAD_DIM), F32)

    @pl.when(j == 0)
    def _():
        step([new_ref], nps * pg, True)

    @pl.when(j > 0)
    def _():
        step(list(page_refs), (nps - j) * pg, False)

    @pl.when(j == nps)
    def _():
        osb_ref[...] = asb_scr[...]
        adf = adf_scr[...]
        odf_ref[...] = adf[:, :HEAD_DIM] / jnp.maximum(adf[:, HEAD_DIM:], 1e-30)


def _dec_main(layer, page_table, cache_rows, new_page, qsb, qdf, qsl, w1all, past, ns):
    nb, npages = page_table.shape
    pg = next(c for c in (8, 4, 2, 1) if npages % c == 0 and npages >= 2 * c)
    nps = npages // pg
    rows_per_page = PAGE_SIZE * ROWS_PER_TOK

    def page_spec(i):
        def imap(b, j, pt):
            return (layer, pt[b, (nps - jnp.maximum(j, 1)) * pg + i], 0, 0)
        return pl.BlockSpec((None, None, rows_per_page, HEAD_DIM), imap)

    qspec = pl.BlockSpec((None, DEC_ROWS, HEAD_DIM), lambda b, j, pt: (b, 0, 0))
    nblk_pad = (nps + 1) * 2 * pg
    cab_rows = pg * 8 * (PAGE_SIZE // CMP_STRIDE)
    key = np.arange(pg * PAGE_SIZE)[:, None]
    bsel = jnp.asarray((key // SLC_LEN == np.arange(LANES)[None, :]).astype(np.float32), dtype=BF16)
    stat_spec = pl.BlockSpec((None, None, DEC_ROWS, LANES), lambda b, j, pt: (b, nps - j, 0, 0))
    grid_spec = pltpu.PrefetchScalarGridSpec(
        num_scalar_prefetch=1,
        grid=(nb, nps + 1),
        in_specs=[page_spec(i) for i in range(pg)] + [
            pl.BlockSpec((None, rows_per_page, HEAD_DIM), lambda b, j, pt: (b, 0, 0)),
            qspec, qspec, qspec,
            pl.BlockSpec((PAGE_SIZE, PAGE_SIZE), lambda b, j, pt: (0, 0)),
            pl.BlockSpec(w1all.shape, lambda b, j, pt: (0, 0)),
            pl.BlockSpec(bsel.shape, lambda b, j, pt: (0, 0))],
        out_specs=[
            qspec, qspec,
            pl.BlockSpec((None, 2 * pg, DEC_ROWS, HEAD_DIM), lambda b, j, pt: (b, nps - j, 0, 0)),
            stat_spec, stat_spec,
            pl.BlockSpec((None, 2, cab_rows, HEAD_DIM), lambda b, j, pt: (b, 0, nps - j, 0))],
        scratch_shapes=[pltpu.VMEM((DEC_ROWS, 1), F32), pltpu.VMEM((DEC_ROWS, HEAD_DIM), F32),
                        pltpu.VMEM((DEC_ROWS, 1), F32), pltpu.VMEM((DEC_ROWS, 2 * HEAD_DIM), F32)])
    out_shape = [jax.ShapeDtypeStruct((nb, DEC_ROWS, HEAD_DIM), F32),
                 jax.ShapeDtypeStruct((nb, DEC_ROWS, HEAD_DIM), F32),
                 jax.ShapeDtypeStruct((nb, nblk_pad, DEC_ROWS, HEAD_DIM), F32),
                 jax.ShapeDtypeStruct((nb, nps + 1, DEC_ROWS, LANES), F32),
                 jax.ShapeDtypeStruct((nb, nps + 1, DEC_ROWS, LANES), F32),
                 jax.ShapeDtypeStruct((nb, 2, (nps + 1) * cab_rows, HEAD_DIM), F32)]
    o_sb, o_df, sacc, sm, sl, cab = pl.pallas_call(
        functools.partial(_dec_kernel, pg=pg, nps=nps, past=past, ns=ns),
        grid_spec=grid_spec,
        out_shape=out_shape,
        compiler_params=_cparams(("arbitrary", "arbitrary")),
        name="decode_paged",
    )(page_table, *([cache_rows] * pg), new_page, qsb, qdf, qsl, _suffix_matrix(PAGE_SIZE), w1all, bsel)

    def by_block(stat, fill):
        stat = stat[..., :2 * pg].transpose(0, 2, 1, 3).reshape(nb, DEC_ROWS, nblk_pad)
        return jnp.pad(stat, ((0, 0), (0, 0), (0, BLK_LANES - nblk_pad)), constant_values=fill)

    return o_sb, o_df, sacc, by_block(sm, NEG), by_block(sl, 0.0), cab


def _cmp_bias_kernel(pe_ref, w1_ref, o_ref):
    for kv in range(2):
        for c in range(2):
            o_ref[kv, c] = _dot(jnp.broadcast_to(pe_ref[kv, c:c + 1, :], (8, pe_ref.shape[-1])).astype(BF16),
                                w1_ref[kv, c].astype(BF16))


def _cmp_bias(pe, w1):
    return pl.pallas_call(
        _cmp_bias_kernel,
        out_shape=jax.ShapeDtypeStruct((2, 2, 8, HEAD_DIM), F32),
        compiler_params=pltpu.CompilerParams(vmem_limit_bytes=VMEM_LIMIT),
        name="cmp_bias",
    )(pe, w1)


def _dec_final_kernel(cab_ref, bias_ref, w2_ref, q_ref, imat_ref, jmat_ref, sacc_ref, sm_ref, sl_ref,
                      swin_ref, nwin_ref, *rest, nc, past, ns, nblk_pad, wb):
    ocm_ref, osl_ref, owi_ref, wout_ref = rest[-4:]
    win_rows, new_rows = swin_ref.shape[0], ns * 2 * G_NSA
    wout_ref[0:win_rows - new_rows, :] = swin_ref[new_rows:win_rows, :]
    wout_ref[win_rows - new_rows:win_rows, :] = nwin_ref[0:new_rows, :]
    ncp = cab_ref.shape[1] // 8
    rows16 = lax.broadcasted_iota(jnp.int32, (16, 1), 0)
    qpos16 = past + (rows16 & (ns - 1))
    imps = []
    for g in range(G_NSA):
        q = q_ref[16 * g:16 * g + 16, :]
        cmp_kv = []
        for kv in range(2):
            a = cab_ref[0, pl.ds(2 * g + kv, ncp, stride=8), :]
            b = cab_ref[1, pl.ds(2 * g + kv, ncp, stride=8), :]
            h = a + pltpu.roll(b, ncp - 1, 0) + bias_ref[kv, 0, 0:1, :] + bias_ref[kv, 1, 0:1, :]
            cmp_kv.append(_dot(_silu(h).astype(BF16), w2_ref[kv].astype(BF16)).astype(BF16))
        kc, vc = cmp_kv
        s = _dot_nt(q, kc)
        n = lax.broadcasted_iota(jnp.int32, s.shape, 1)
        mask = (n * CMP_STRIDE + CMP_LEN - 1 <= qpos16) & (n < nc)
        s = jnp.where(mask, s, NEG)
        m = jnp.max(s, axis=-1, keepdims=True)
        e = jnp.where(mask, jnp.exp(s - m), 0.0)
        p = e / jnp.maximum(jnp.sum(e, axis=-1, keepdims=True), 1e-30)
        ocm_ref[16 * g:16 * g + 16, :] = _dot(p.astype(BF16), vc)
        hi, mid, lo = _split3(p)
        jm = jmat_ref[...]
        psum = _dot(jm, hi) + _dot(jm, mid) + _dot(jm, lo)
        imps.append(_dot_f32ish(psum, imat_ref[...]))
    imp = jnp.concatenate(imps, axis=0)
    blk = lax.broadcasted_iota(jnp.int32, imp.shape, 1)
    cur = (past + (lax.broadcasted_iota(jnp.int32, imp.shape, 0) & (ns - 1))) // SLC_LEN
    sel_all = _select_blocks(imp, cur, blk)
    for g in range(G_NSA):
        q = q_ref[16 * g:16 * g + 16, :]
        sel = sel_all[16 * g:16 * g + 16]
        mb = sm_ref[16 * g:16 * g + 16, :]
        lb = sl_ref[16 * g:16 * g + 16, :]
        picked = sel > 0.5
        mtot = jnp.max(jnp.where(picked, mb, NEG), axis=-1, keepdims=True)
        wgt = jnp.where(picked, jnp.exp(mb - mtot), 0.0)
        den = jnp.sum(wgt * lb, axis=-1, keepdims=True)
        wgt_b = wgt.astype(BF16)
        pad = jnp.zeros((BLK_LANES - nblk_pad, HEAD_DIM), F32)
        for half in range(2):
            rsel = lax.broadcasted_iota(jnp.int32, (8, HEAD_DIM), 0)
            num = jnp.zeros((8, HEAD_DIM), F32)
            for i in range(8):
                r32 = 16 * g + 8 * half + i
                acc = jnp.concatenate([sacc_ref[pl.ds(r32, nblk_pad, stride=DEC_ROWS), :], pad], axis=0)
                ahi = acc.astype(BF16)
                alo = (acc - ahi.astype(F32)).astype(BF16)
                w8 = wgt_b[8 * half:8 * half + 8, :]
                res = _dot(w8, ahi) + _dot(w8, alo)
                num = jnp.where(rsel == i, res, num)
            r0 = 16 * g + 8 * half
            osl_ref[r0:r0 + 8, :] = num / jnp.maximum(den[8 * half:8 * half + 8], 1e-30)
        kw = swin_ref[pl.ds(g, wb, stride=2 * G_NSA), :].astype(BF16)
        vw = swin_ref[pl.ds(G_NSA + g, wb, stride=2 * G_NSA), :].astype(BF16)
        kn = nwin_ref[pl.ds(g, 8, stride=2 * G_NSA), :].astype(BF16)
        vn = nwin_ref[pl.ds(G_NSA + g, 8, stride=2 * G_NSA), :].astype(BF16)
        s1 = _dot_nt(q, kw)
        s2 = _dot_nt(q, kn)
        d1 = qpos16 - (past - wb + lax.broadcasted_iota(jnp.int32, s1.shape, 1))
        k1 = (d1 >= 0) & (d1 <= WINDOW)
        d2 = qpos16 - (past + lax.broadcasted_iota(jnp.int32, s2.shape, 1))
        k2 = (d2 >= 0) & (d2 <= WINDOW)
        s1 = jnp.where(k1, s1, NEG)
        s2 = jnp.where(k2, s2, NEG)
        m = jnp.maximum(jnp.max(s1, axis=-1, keepdims=True), jnp.max(s2, axis=-1, keepdims=True))
        e1 = jnp.where(k1, jnp.exp(s1 - m), 0.0)
        e2 = jnp.where(k2, jnp.exp(s2 - m), 0.0)
        den = jnp.sum(e1, axis=-1, keepdims=True) + jnp.sum(e2, axis=-1, keepdims=True)
        owi_ref[16 * g:16 * g + 16, :] = (_dot(e1.astype(BF16), vw) + _dot(e2.astype(BF16), vn)) / jnp.maximum(den, 1e-30)


def _dec_final(layer, cab, bias, w2, qsl, sacc, sm, sl, state_rows, new_win, nc, past, ns, win_stack):
    nb = cab.shape[0]
    depth = state_rows.shape[0]
    ncp = cab.shape[2] // 8
    nblk_pad = sacc.shape[1] // DEC_ROWS
    wb = state_rows.shape[2] // (2 * G_NSA)
    imat = _importance_matrix(ncp, BLK_LANES)
    t_of = np.arange(16) % ns
    jmat = jnp.asarray((t_of[:, None] == t_of[None, :]).astype(np.float32), dtype=BF16)
    ospec = pl.BlockSpec((None, DEC_ROWS, HEAD_DIM), lambda b: (b, 0, 0))
    wspec = pl.BlockSpec((None, None, state_rows.shape[2], HEAD_DIM), lambda b: (layer, b, 0, 0))
    in_specs = [pl.BlockSpec((None, 2, ncp * 8, HEAD_DIM), lambda b: (b, 0, 0, 0)),
                pl.BlockSpec(bias.shape, lambda b: (0, 0, 0, 0)),
                pl.BlockSpec(w2.shape, lambda b: (0, 0, 0)),
                ospec,
                pl.BlockSpec(imat.shape, lambda b: (0, 0)),
                pl.BlockSpec(jmat.shape, lambda b: (0, 0)),
                pl.BlockSpec((None, nblk_pad * DEC_ROWS, HEAD_DIM), lambda b: (b, 0, 0)),
                pl.BlockSpec((None, DEC_ROWS, BLK_LANES), lambda b: (b, 0, 0)),
                pl.BlockSpec((None, DEC_ROWS, BLK_LANES), lambda b: (b, 0, 0)),
                wspec,
                pl.BlockSpec((None, new_win.shape[1], HEAD_DIM), lambda b: (b, 0, 0))]
    args = [cab, bias, w2, qsl, imat, jmat, sacc, sm, sl, state_rows, new_win]
    aliases = {}
    if win_stack is not None:
        in_specs.append(pl.BlockSpec(memory_space=pl.ANY))
        args.append(win_stack)
        aliases = {len(args) - 1: 3}
    return pl.pallas_call(
        functools.partial(_dec_final_kernel, nc=nc, past=past, ns=ns, nblk_pad=nblk_pad, wb=wb),
        grid=(nb,),
        in_specs=in_specs,
        out_specs=[ospec, ospec, ospec, wspec],
        out_shape=[jax.ShapeDtypeStruct((nb, DEC_ROWS, HEAD_DIM), F32)] * 3 + [
            jax.ShapeDtypeStruct((depth,) + state_rows.shape[1:], F32)],
        input_output_aliases=aliases,
        compiler_params=_cparams(("arbitrary",)),
        name="decode_nsa_finish",
    )(*args)


def _sample_mixer(x, mod, lw, layer, tabs, page_table, cache_rows, state_rows, nb, ns, past, win_stack):
    sh_a, sc_a, g_a = mod[0], mod[1], mod[2]
    proj = _proj(x, lw["n0"], sc_a, sh_a, lw["w_in"])
    (qsb, _, _, qdf, _, _, qns, _, _, _, _, _, _, gates, kv_rows, win_rows) = _postproj(proj, tabs)

    def per_seq(a, heads, rows_per_head):
        a = a.reshape(heads, nb, ns, HEAD_DIM).transpose(1, 0, 2, 3)
        a = jnp.pad(a, ((0, 0), (0, 0), (0, rows_per_head - ns), (0, 0)))
        return a.reshape(nb, heads * rows_per_head, HEAD_DIM)

    qsb_d = per_seq(qsb, H_SB, 8)
    qdf_d = per_seq(qdf, 2 * H_DIFF, ns)
    qsl_d = per_seq(qns, H_NSA, ns)
    kv_tok = kv_rows.reshape(nb, ns * ROWS_PER_TOK, HEAD_DIM)
    new_page = jnp.pad(kv_tok, ((0, 0), (0, (PAGE_SIZE - ns) * ROWS_PER_TOK), (0, 0)))
    w1 = lw["cmp_w1"]
    w1all = jnp.concatenate([w1[0, 0], w1[0, 1], w1[1, 0], w1[1, 1]], axis=-1).astype(BF16)
    o_sb, o_df, sacc, sm, sl, cab = _dec_main(layer, page_table, cache_rows, new_page, qsb_d, qdf_d, qsl_d, w1all,
                                              past, ns)
    bias = _cmp_bias(lw["cmp_pe"], w1)
    new_win = jnp.pad(win_rows.reshape(nb, ns * 2 * G_NSA, HEAD_DIM), ((0, 0), (0, (8 - ns) * 2 * G_NSA), (0, 0)))
    padded = -(-(past + ns) // SLC_LEN) * SLC_LEN
    nc = padded // CMP_STRIDE - CMP_LEN // CMP_STRIDE + 1
    o_cmp, o_slc, o_win, win_stack = _dec_final(layer, cab, bias, lw["cmp_w2"], qsl_d, sacc.reshape(nb, -1, HEAD_DIM),
                                                sm, sl, state_rows, new_win, nc, past, ns, win_stack)

    def head_major(a, heads, rows_per_head):
        a = a.reshape(nb, heads, rows_per_head, HEAD_DIM)[:, :, :ns]
        return a.transpose(1, 0, 2, 3).reshape(heads, nb * ns, HEAD_DIM)

    x1 = _merge(head_major(o_sb, H_SB, 8), head_major(o_df, 2 * H_DIFF, ns), head_major(o_cmp, H_NSA, ns),
                head_major(o_slc, H_NSA, ns), head_major(o_win, H_NSA, ns), gates, lw["diff_lambda"],
                lw["diff_norm"], x, g_a, lw["n1"], lw["w_out"], layer)
    return x1, kv_tok, win_stack


def _split_mod(mod):
    return [mod[:, i * (mod.shape[1] // 6):(i + 1) * (mod.shape[1] // 6)] for i in range(6)]


def _prompt_mixer(x, mod, lw, layer, tabs, depth, kv_stack):
    t = x.shape[0]
    sh_a, sc_a, g_a = mod[0], mod[1], mod[2]
    proj = _proj(x, lw["n0"], sc_a, sh_a, lw["w_in"])
    (qsb, ksb, vsb, qdf, kdf, vdf, qns, kc, vc, ks, vs, kw, vw, gates, kv_rows, win_rows) = _postproj(
        proj, tabs, layer, depth, kv_stack)
    o_sb = _sb_attention(qsb, ksb, vsb)
    o_df = _flash(qdf, kdf, vdf, "causal")
    nch = t // CMP_STRIDE
    xc = jnp.stack([kc, vc]).reshape(2, G_NSA, nch, CMP_STRIDE * HEAD_DIM)
    cmp_kv = _compress(xc, lw["cmp_pe"], lw["cmp_w1"], lw["cmp_w2"])
    o_cmp, sel = _cmp_select(qns, cmp_kv[0], cmp_kv[1], nch - 1)
    o_slc = _flash(qns, ks, vs, "select", sel)
    o_win = _flash(qns, kw, vw, "window")
    x1 = _merge(o_sb, o_df, o_cmp, o_slc, o_win, gates, lw["diff_lambda"], lw["diff_norm"], x, g_a, lw["n1"],
                lw["w_out"], layer)
    return x1, kv_rows, win_rows


def _layer_weights(layer, norms, w_in_p, w_out_b, diff_lambda, diff_norm, cmp_pe, cmp_w1, cmp_w2, wgu_b, wd_b):
    d = norms.shape[-1]
    return dict(
        n0=norms[layer, 0].reshape(1, d), n1=norms[layer, 1].reshape(1, d),
        n2=norms[layer, 2].reshape(1, d), n3=norms[layer, 3].reshape(1, d),
        w_in=w_in_p[layer], w_out=w_out_b[layer],
        diff_lambda=diff_lambda[layer], diff_norm=diff_norm[layer].reshape(1, HEAD_DIM),
        cmp_pe=cmp_pe[layer].reshape(2, 2, CMP_STRIDE * HEAD_DIM),
        cmp_w1=cmp_w1[layer].reshape(2, 2, CMP_STRIDE * HEAD_DIM, HEAD_DIM),
        cmp_w2=cmp_w2[layer], w_gate_up=wgu_b[layer], w_down=wd_b[layer])


def kernel(x_prompt, x_sample, cache_kv, state_win_kv, page_table, c_prompt, c_sample, ada_w, ada_b, norms, w_in,
           w_out, diff_lambda, diff_norm, cmp_pe, cmp_w1, cmp_w2, w_gate_up, w_down):
    depth = ada_w.shape[0]
    _, seq, d = x_prompt.shape
    nb, ns, _ = x_sample.shape
    past = page_table.shape[1] * PAGE_SIZE

    w_in_p = jnp.pad(w_in, ((0, 0), (0, 0), (0, N_IN_PAD - N_IN))).astype(BF16)
    w_out_b = w_out.astype(BF16)
    wgu_b = w_gate_up.astype(BF16)
    wd_b = w_down.astype(BF16)

    n_c = 1 + nb
    rows = -(-n_c // 8) * 8
    c_all = jnp.pad(jnp.concatenate([c_prompt, c_sample], axis=0), ((0, rows - n_c), (0, 0)))
    mod_all = _ada(c_all, ada_w, ada_b)

    tabs_p = _rope_tables(jnp.arange(seq))
    tabs_s = _rope_tables(past + jnp.arange(nb * ns) % ns)
    cache_rows = cache_kv.transpose(0, 1, 2, 4, 3, 5).reshape(depth, cache_kv.shape[1], PAGE_SIZE * ROWS_PER_TOK,
                                                               HEAD_DIM)
    wb_s = state_win_kv.shape[2]
    state_rows = state_win_kv.reshape(depth, nb, wb_s * 2 * G_NSA, HEAD_DIM)
    xp = x_prompt.reshape(seq, d)
    xs = x_sample.reshape(nb * ns, d)
    win_p, kv_s = [], []
    kv_stack = win_stack = None
    for layer in range(depth):
        lw = _layer_weights(layer, norms, w_in_p, w_out_b, diff_lambda, diff_norm, cmp_pe, cmp_w1, cmp_w2, wgu_b, wd_b)
        mod_p = _split_mod(mod_all[layer, 0:1])
        x1, kv_stack, win_rows = _prompt_mixer(xp, mod_p, lw, layer, tabs_p, depth, kv_stack)
        xp = _ffn(x1, lw["n2"], mod_p[4], mod_p[3], lw["w_gate_up"], lw["w_down"], mod_p[5], lw["n3"])
        wb = min(WINDOW, seq)
        win_p.append(win_rows[seq - wb:].reshape(1, wb, 2, G_NSA, HEAD_DIM))

        mod_s = _split_mod(jnp.repeat(mod_all[layer, 1:1 + nb], ns, axis=0))
        x1, kv_tok, win_stack = _sample_mixer(xs, mod_s, lw, layer, tabs_s, page_table, cache_rows, state_rows,
                                              nb, ns, past, win_stack)
        xs = _ffn(x1, lw["n2"], mod_s[4], mod_s[3], lw["w_gate_up"], lw["w_down"], mod_s[5], lw["n3"])
        kv_s.append(kv_tok.reshape(nb, ns, N_KV, 2, HEAD_DIM).transpose(0, 1, 3, 2, 4))
    kv_prompt = kv_stack.reshape(depth, 1, seq, N_KV, 2, HEAD_DIM).transpose(0, 1, 2, 4, 3, 5)
    win_sample = win_stack.reshape(depth, nb, wb_s, 2, G_NSA, HEAD_DIM)
    return (xp.reshape(1, seq, d), xs.reshape(nb, ns, d), kv_prompt, jnp.stack(kv_s), jnp.stack(win_p), win_sample)
```

```python
import functools
import math

import numpy as np
import jax
import jax.numpy as jnp
from jax import lax
from jax.experimental import pallas as pl
from jax.experimental.pallas import tpu as pltpu

F32 = jnp.float32
BF16 = jnp.bfloat16

HEAD_DIM = 128
H_SB = 4
H_DIFF = 4
H_NSA = 8
G_NSA = 2
HPG = H_NSA // G_NSA
N_KV = H_SB + H_DIFF + 2 * G_NSA
N_HEADS = H_SB + H_DIFF + H_NSA
DIFF_DIM = HEAD_DIM // 2
CMP_LEN = 32
CMP_STRIDE = 16
SLC_LEN = 64
SLC_TOPK = 16
WINDOW = 512
PAGE_SIZE = 128
ROPE_THETA = 10000.0
EPS = 1e-6
NEG = -1e30
LANES = 128
VMEM_LIMIT = 56 * 1024 * 1024

C_QA, C_KA, C_VA = 0, 512, 1024
C_QB, C_KB, C_VB = 1536, 2048, 2560
C_QC = 3072
C_KC, C_VC, C_KS, C_VS, C_KW, C_VW = 4096, 4352, 4608, 4864, 5120, 5376
C_GATE = 5632
N_IN = 5656
N_IN_PAD = 5760
ROWS_PER_TOK = 2 * N_KV


def _cparams(sem):
    return pltpu.CompilerParams(dimension_semantics=sem, vmem_limit_bytes=VMEM_LIMIT)


def _silu(x):
    return x * jax.nn.sigmoid(x)


def _rms_rows(x, g):
    return x * lax.rsqrt(jnp.mean(x * x, axis=-1, keepdims=True) + EPS) * g


def _dot(a, b):
    return jnp.dot(a, b, preferred_element_type=F32)


def _dot_nt(a, b):
    return lax.dot_general(a, b, (((1,), (1,)), ((), ())), preferred_element_type=F32)


def _split3(x):
    hi = x.astype(BF16)
    r1 = x - hi.astype(F32)
    mid = r1.astype(BF16)
    lo = (r1 - mid.astype(F32)).astype(BF16)
    return hi, mid, lo


def _dot_f32ish(x, m01):
    hi, mid, lo = _split3(x)
    return _dot(hi, m01) + _dot(mid, m01) + _dot(lo, m01)


def _ada_kernel(c_ref, w_ref, b_ref, o_ref):
    a = _silu(c_ref[...]).astype(BF16)
    o_ref[...] = _dot(a, w_ref[...].astype(BF16)) + b_ref[...]


def _ada(c_all, ada_w, ada_b):
    depth, d, n = ada_w.shape
    rows = c_all.shape[0]
    tn = 1024
    return pl.pallas_call(
        _ada_kernel,
        grid=(depth, n // tn),
        in_specs=[pl.BlockSpec((rows, d), lambda l, j: (0, 0)),
                  pl.BlockSpec((None, d, tn), lambda l, j: (l, 0, j)),
                  pl.BlockSpec((None, 1, tn), lambda l, j: (l, 0, j))],
        out_specs=pl.BlockSpec((None, rows, tn), lambda l, j: (l, 0, j)),
        out_shape=jax.ShapeDtypeStruct((depth, rows, n), F32),
        compiler_params=_cparams(("arbitrary", "arbitrary")),
        name="ada_mod",
    )(c_all, ada_w, ada_b.reshape(depth, 1, n))


def _row_spec(arr, tm, grid_rank):
    c = arr.shape[1]
    if arr.shape[0] == 1:
        if grid_rank == 1:
            return pl.BlockSpec((1, c), lambda i: (0, 0))
        return pl.BlockSpec((1, c), lambda i, j: (0, 0))
    if grid_rank == 1:
        return pl.BlockSpec((tm, c), lambda i: (i, 0))
    return pl.BlockSpec((tm, c), lambda i, j: (i, 0))


def _proj_kernel(x_ref, g_ref, sc_ref, sh_ref, w_ref, o_ref, h_scr):
    @pl.when(pl.program_id(1) == 0)
    def _():
        y = _rms_rows(x_ref[...], g_ref[...])
        h_scr[...] = (y * (1.0 + sc_ref[...]) + sh_ref[...]).astype(BF16)

    o_ref[...] = _dot(h_scr[...], w_ref[...])


def _proj(x, g, sc, sh, w):
    t, d = x.shape
    n = w.shape[1]
    tm = min(512, t)
    tn = 1920
    return pl.pallas_call(
        _proj_kernel,
        grid=(t // tm, n // tn),
        in_specs=[pl.BlockSpec((tm, d), lambda i, j: (i, 0)),
                  pl.BlockSpec((1, d), lambda i, j: (0, 0)),
                  _row_spec(sc, tm, 2), _row_spec(sh, tm, 2),
                  pl.BlockSpec((d, tn), lambda i, j: (0, j))],
        out_specs=pl.BlockSpec((tm, tn), lambda i, j: (i, j)),
        out_shape=jax.ShapeDtypeStruct((t, n), F32),
        scratch_shapes=[pltpu.VMEM((tm, d), BF16)],
        compiler_params=_cparams(("arbitrary", "arbitrary")),
        name="in_proj",
    )(x, g, sc, sh, w)


def _postproj_kernel(p_ref, ca_ref, sa_ref, cb_ref, sb_ref, *rest):
    (qsb_ref, ksb_ref, vsb_ref, qdf_ref, kdf_ref, vdf_ref, qns_ref,
     kc_ref, vc_ref, ks_ref, vs_ref, kw_ref, vw_ref, gate_ref, kv_ref, win_ref) = rest[-16:]
    tm = p_ref.shape[0]
    ca, sa, cb, sb = ca_ref[...], sa_ref[...], cb_ref[...], sb_ref[...]
    lane = lax.broadcasted_iota(jnp.int32, ca.shape, 1)
    low_half = (lane % DIFF_DIM) < (DIFF_DIM // 2)
    first_chunk = lane < DIFF_DIM
    sm_scale = HEAD_DIM ** -0.5
    df_scale = DIFF_DIM ** -0.5

    def col(c0, h):
        return p_ref[:, c0 + h * HEAD_DIM:c0 + (h + 1) * HEAD_DIM]

    def rope_a(x):
        return x * ca + pltpu.roll(x, HEAD_DIM // 2, 1) * sa

    def rope_b(x):
        partner = jnp.where(low_half, pltpu.roll(x, HEAD_DIM - DIFF_DIM // 2, 1), pltpu.roll(x, DIFF_DIM // 2, 1))
        return x * cb + partner * sb

    def put_kv(is_v, head, val):
        kv_ref[pl.ds(2 * head + is_v, tm, stride=ROWS_PER_TOK), :] = val

    for h in range(H_SB):
        qsb_ref[h] = (col(C_QA, h) * sm_scale).astype(BF16)
        k, v = col(C_KA, h), col(C_VA, h)
        ksb_ref[h] = k.astype(BF16)
        vsb_ref[h] = v.astype(BF16)
        put_kv(0, h, k)
        put_kv(1, h, v)
    for h in range(H_DIFF):
        q = rope_b(col(C_QB, h)) * df_scale
        qdf_ref[2 * h] = jnp.where(first_chunk, q, 0.0).astype(BF16)
        qdf_ref[2 * h + 1] = jnp.where(first_chunk, 0.0, q).astype(BF16)
        k, v = rope_b(col(C_KB, h)), col(C_VB, h)
        kdf_ref[h] = k.astype(BF16)
        vdf_ref[h] = v.astype(BF16)
        put_kv(0, H_SB + h, k)
        put_kv(1, H_SB + h, v)
    for h in range(H_NSA):
        qns_ref[h] = (rope_a(col(C_QC, h)) * sm_scale).astype(BF16)
    for g in range(G_NSA):
        k, v = rope_a(col(C_KC, g)), col(C_VC, g)
        kc_ref[g] = k
        vc_ref[g] = v
        put_kv(0, H_SB + H_DIFF + g, k)
        put_kv(1, H_SB + H_DIFF + g, v)
        k, v = rope_a(col(C_KS, g)), col(C_VS, g)
        ks_ref[g] = k.astype(BF16)
        vs_ref[g] = v.astype(BF16)
        put_kv(0, H_SB + H_DIFF + G_NSA + g, k)
        put_kv(1, H_SB + H_DIFF + G_NSA + g, v)
        k, v = rope_a(col(C_KW, g)), col(C_VW, g)
        kw_ref[g] = k.astype(BF16)
        vw_ref[g] = v.astype(BF16)
        win_ref[:, g * HEAD_DIM:(g + 1) * HEAD_DIM] = k
        win_ref[:, (G_NSA + g) * HEAD_DIM:(G_NSA + g + 1) * HEAD_DIM] = v
    gate_ref[...] = jax.nn.sigmoid(p_ref[:, C_GATE:C_GATE + LANES])


def _postproj(proj, tabs, layer=0, depth=1, kv_stack=None):
    t = proj.shape[0]
    tm = min(256, t)
    hm = lambda n, dt: jax.ShapeDtypeStruct((n, t, HEAD_DIM), dt)
    hspec = lambda n: pl.BlockSpec((n, tm, HEAD_DIM), lambda i: (0, i, 0))
    tab_spec = pl.BlockSpec((tm, LANES), lambda i: (i, 0))
    out_shapes = [hm(H_SB, BF16), hm(H_SB, BF16), hm(H_SB, BF16),
                  hm(2 * H_DIFF, BF16), hm(H_DIFF, BF16), hm(H_DIFF, BF16), hm(H_NSA, BF16),
                  hm(G_NSA, F32), hm(G_NSA, F32), hm(G_NSA, BF16), hm(G_NSA, BF16),
                  hm(G_NSA, BF16), hm(G_NSA, BF16),
                  jax.ShapeDtypeStruct((t, LANES), F32),
                  jax.ShapeDtypeStruct((depth, t * ROWS_PER_TOK, HEAD_DIM), F32),
                  jax.ShapeDtypeStruct((t, 2 * G_NSA * HEAD_DIM), F32)]
    out_specs = [hspec(H_SB), hspec(H_SB), hspec(H_SB), hspec(2 * H_DIFF), hspec(H_DIFF), hspec(H_DIFF),
                 hspec(H_NSA), hspec(G_NSA), hspec(G_NSA), hspec(G_NSA), hspec(G_NSA), hspec(G_NSA), hspec(G_NSA),
                 pl.BlockSpec((tm, LANES), lambda i: (i, 0)),
                 pl.BlockSpec((None, tm * ROWS_PER_TOK, HEAD_DIM), lambda i: (layer, i, 0)),
                 pl.BlockSpec((tm, 2 * G_NSA * HEAD_DIM), lambda i: (i, 0))]
    in_specs = [pl.BlockSpec((tm, N_IN_PAD), lambda i: (i, 0)), tab_spec, tab_spec, tab_spec, tab_spec]
    args = [proj, *tabs]
    aliases = {}
    if kv_stack is not None:
        in_specs.append(pl.BlockSpec(memory_space=pl.ANY))
        args.append(kv_stack)
        aliases = {len(args) - 1: 14}
    return pl.pallas_call(
        _postproj_kernel,
        grid=(t // tm,),
        in_specs=in_specs,
        out_specs=out_specs,
        out_shape=out_shapes,
        input_output_aliases=aliases,
        compiler_params=_cparams(("arbitrary",)),
        name="rope_split",
    )(*args)


def _rope_tables(pos):
    posf = pos.astype(F32)[:, None]

    def tab(half, reps):
        inv = jnp.power(ROPE_THETA, -jnp.arange(half, dtype=F32) / half)
        ang = posf * inv[None, :]
        c, s = jnp.cos(ang), jnp.sin(ang)
        return jnp.tile(jnp.concatenate([c, c], 1), (1, reps)), jnp.tile(jnp.concatenate([-s, s], 1), (1, reps))

    ca, sa = tab(HEAD_DIM // 2, 1)
    cb, sb = tab(DIFF_DIM // 2, 2)
    return ca, sa, cb, sb


SB_UNDERFLOW = -104.0


def _sb_kernel(q_ref, k_ref, v_ref, u_ref, o_ref, carry_scr, acc_scr, *, tq, tk):
    qi = pl.program_id(1)
    carry_scr[...] = jnp.zeros_like(carry_scr)
    acc_scr[...] = jnp.zeros_like(acc_scr)
    q = q_ref[...]
    u = u_ref[...]
    row = lax.broadcasted_iota(jnp.int32, (tq, tk), 0)
    col = lax.broadcasted_iota(jnp.int32, (tq, tk), 1)

    def body(state):
        kj, _ = state
        off = pl.multiple_of(kj * tk, tk)
        z = _dot_nt(q, k_ref[pl.ds(off, tk), :])
        mask = (off + col) < (qi * tq + row)
        l1p = jnp.log(1.0 + jnp.exp(-jnp.abs(z)))
        ls_pos = jnp.minimum(z, 0.0) - l1p
        lk = jnp.where(mask, ls_pos - z, 0.0)
        hi = lk.astype(BF16)
        lo = (lk - hi.astype(F32)).astype(BF16)
        after = _dot(hi, u) + _dot(lo, u) + carry_scr[...]
        w = jnp.where(mask, jnp.exp(ls_pos + after), 0.0)
        acc_scr[...] += _dot(w.astype(BF16), v_ref[pl.ds(off, tk), :])
        carry = carry_scr[...] + jnp.sum(lk, axis=-1, keepdims=True)
        carry_scr[...] = carry
        go = (jnp.max(carry) > SB_UNDERFLOW).astype(jnp.int32)
        return kj - 1, go

    last_tile = (qi + 1) * (tq // tk) - 1
    lax.while_loop(lambda st: (st[0] >= 0) & (st[1] > 0), body, (last_tile, jnp.int32(1)))
    o_ref[...] = acc_scr[...]


def _suffix_matrix(n):
    j = np.arange(n)[:, None]
    s = np.arange(n)[None, :]
    return jnp.asarray((j > s).astype(np.float32), dtype=BF16)


def _sb_attention(q, k, v):
    h, t, d = q.shape
    tq = min(512, t)
    tk = min(256, t)
    return pl.pallas_call(
        functools.partial(_sb_kernel, tq=tq, tk=tk),
        grid=(h, t // tq),
        in_specs=[pl.BlockSpec((None, tq, d), lambda hh, i: (hh, i, 0)),
                  pl.BlockSpec((None, t, d), lambda hh, i: (hh, 0, 0)),
                  pl.BlockSpec((None, t, d), lambda hh, i: (hh, 0, 0)),
                  pl.BlockSpec((tk, tk), lambda hh, i: (0, 0))],
        out_specs=pl.BlockSpec((None, tq, d), lambda hh, i: (hh, i, 0)),
        out_shape=jax.ShapeDtypeStruct((h, t, d), F32),
        scratch_shapes=[pltpu.VMEM((tq, 1), F32), pltpu.VMEM((tq, d), F32)],
        compiler_params=_cparams(("arbitrary", "arbitrary")),
        name="sb_attn",
    )(q, k, v, _suffix_matrix(tk))


SEL_BIAS = 16384.0


def _flash_kernel(q_ref, k_ref, v_ref, *rest, mode, rep, tq, tk):
    if mode == "select":
        bias_ref, et_ref, o_ref, m_scr, acc_scr, s_scr, s2_scr = rest
    else:
        o_ref, m_scr, acc_scr, s_scr, s2_scr = rest
    qi = pl.program_id(1)
    rows = rep * tq
    m_scr[...] = jnp.full_like(m_scr, NEG)
    acc_scr[...] = jnp.zeros_like(acc_scr)
    q = q_ref[...].reshape(rows, HEAD_DIM)
    if mode == "select":
        q = jnp.concatenate([q, jnp.concatenate([bias_ref[...]] * rep, axis=0)], axis=1)
    ones = jnp.ones((tk, HEAD_DIM), BF16)
    qpos = qi * tq + (lax.broadcasted_iota(jnp.int32, (rows, tk), 0) & (tq - 1))
    col = lax.broadcasted_iota(jnp.int32, (rows, tk), 1)

    newest = ((qi + 1) * tq - 1) // tk
    if mode == "window":
        oldest = jnp.maximum(qi * tq - WINDOW, 0) // tk
        n_tiles = newest - oldest + 1
        tile_of = lambda i: newest - i
    else:
        n_tiles = newest + 1
        tile_of = lambda i: i
    n_full = (qi * tq + 1) // tk

    def scores(i):
        off = pl.multiple_of(tile_of(jnp.minimum(i, n_tiles - 1)) * tk, tk)
        k = k_ref[pl.ds(off, tk), :]
        if mode == "select":
            k = jnp.concatenate([k, et_ref[pl.ds(off, tk), :]], axis=1)
        return _dot_nt(q, k)

    def masked_scores(s, off):
        dist = qpos - (off + col)
        mask = (dist >= 0) & (dist <= WINDOW) if mode == "window" else dist >= 0
        return jnp.where(mask, s, NEG)

    def visit(src, dst, i):
        j = tile_of(i)
        off = pl.multiple_of(j * tk, tk)
        if mode == "window":
            src[...] = masked_scores(src[...], off)
        else:
            @pl.when(j >= n_full)
            def _():
                src[...] = masked_scores(src[...], off)
        dst[...] = scores(i + 1)
        s = src[...]
        m_old = m_scr[...]
        m_new = jnp.maximum(m_old, jnp.max(s, axis=-1, keepdims=True))
        e = jnp.exp(s - m_new)
        v_ext = jnp.concatenate([v_ref[pl.ds(off, tk), :], ones], axis=1)
        acc_scr[...] = jnp.exp(m_old - m_new) * acc_scr[...] + _dot(e.astype(BF16), v_ext)
        m_scr[...] = m_new

    def body(p, carry):
        visit(s_scr, s2_scr, 2 * p)
        pl.when(2 * p + 1 < n_tiles)(lambda: visit(s2_scr, s_scr, 2 * p + 1))
        return carry

    s_scr[...] = scores(0)
    lax.fori_loop(0, (n_tiles + 1) // 2, body, 0)
    acc = acc_scr[...]
    out = acc[:, :HEAD_DIM] / jnp.maximum(acc[:, HEAD_DIM:], 1e-30)
    o_ref[...] = out.reshape(rep, tq, HEAD_DIM)


def _block_onehot(t):
    key = np.arange(t)[:, None]
    b = np.arange(LANES)[None, :]
    return jnp.asarray((key // SLC_LEN == b).astype(np.float32), dtype=BF16)


def _flash(q, k, v, mode, sel_bias=None):
    hq, t, d = q.shape
    hk = k.shape[0]
    rep = hq // hk
    tq = min((1024 if mode == "window" else 512) // rep, t)
    tk = min(256 if mode == "window" else 1024, t)
    in_specs = [pl.BlockSpec((rep, tq, d), lambda g, i: (g, i, 0)),
                pl.BlockSpec((None, t, d), lambda g, i: (g, 0, 0)),
                pl.BlockSpec((None, t, d), lambda g, i: (g, 0, 0))]
    args = [q, k, v]
    if mode == "select":
        in_specs += [pl.BlockSpec((None, tq, LANES), lambda g, i: (g, i, 0)),
                     pl.BlockSpec((t, LANES), lambda g, i: (0, 0))]
        args += [sel_bias, _block_onehot(t)]
    return pl.pallas_call(
        functools.partial(_flash_kernel, mode=mode, rep=rep, tq=tq, tk=tk),
        grid=(hk, t // tq),
        in_specs=in_specs,
        out_specs=pl.BlockSpec((rep, tq, d), lambda g, i: (g, i, 0)),
        out_shape=jax.ShapeDtypeStruct((hq, t, d), F32),
        scratch_shapes=[pltpu.VMEM((rep * tq, 1), F32), pltpu.VMEM((rep * tq, 2 * d), F32),
                        pltpu.VMEM((rep * tq, tk), F32), pltpu.VMEM((rep * tq, tk), F32)],
        compiler_params=_cparams(("arbitrary", "arbitrary")),
        name="attn_" + mode,
    )(*args)


def _compress_kernel(x_ref, pe_ref, w1_ref, w2_ref, o_ref):
    x = x_ref[...]
    n = x.shape[0]
    a = _dot((x + pe_ref[0:1, :]).astype(BF16), w1_ref[0].astype(BF16))
    b = _dot((x + pe_ref[1:2, :]).astype(BF16), w1_ref[1].astype(BF16))
    h = a + pltpu.roll(b, n - 1, 0)
    o_ref[...] = _dot(_silu(h).astype(BF16), w2_ref[...].astype(BF16)).astype(o_ref.dtype)


def _compress(xc, pe, w1, w2):
    _, g, n, wdt = xc.shape
    return pl.pallas_call(
        _compress_kernel,
        grid=(2, g),
        in_specs=[pl.BlockSpec((None, None, n, wdt), lambda a, b: (a, b, 0, 0)),
                  pl.BlockSpec((None, 2, wdt), lambda a, b: (a, 0, 0)),
                  pl.BlockSpec((None, 2, wdt, HEAD_DIM), lambda a, b: (a, 0, 0, 0)),
                  pl.BlockSpec((None, HEAD_DIM, HEAD_DIM), lambda a, b: (a, 0, 0))],
        out_specs=pl.BlockSpec((None, None, n, HEAD_DIM), lambda a, b: (a, b, 0, 0)),
        out_shape=jax.ShapeDtypeStruct((2, g, n, HEAD_DIM), BF16),
        compiler_params=_cparams(("arbitrary", "arbitrary")),
        name="nsa_compress",
    )(xc, pe, w1, w2)


def _importance_matrix(nc_pad, nblk_pad):
    r = SLC_LEN // CMP_STRIDE
    c = CMP_LEN // CMP_STRIDE
    i = np.arange(nc_pad)[:, None]
    b = np.arange(nblk_pad)[None, :]
    lo = r * b - (c - 1)
    return jnp.asarray(((i >= lo) & (i <= lo + r + c - 2)).astype(np.float32), dtype=BF16)


def _select_blocks(imp, cur, blk):
    forced = (blk == 0) | (blk == cur) | (blk == cur - 1)
    score = jnp.where(forced, 1e30, jnp.where(blk <= cur, imp, NEG))
    sel = jnp.zeros_like(score)
    blkf = blk.astype(F32)
    for _ in range(SLC_TOPK):
        mx = jnp.max(score, axis=-1, keepdims=True)
        idx = jnp.min(jnp.where(score == mx, blkf, 1e9), axis=-1, keepdims=True)
        pick = blkf == idx
        sel = jnp.where(pick, 1.0, sel)
        score = jnp.where(pick, -3e38, score)
    return sel


def _cmp_kernel(q_ref, kc_ref, vc_ref, imat_ref, o_ref, sel_ref, *, tq, nc):
    i = pl.program_id(0)
    rows = HPG * tq
    ncp = kc_ref.shape[1]
    row = lax.broadcasted_iota(jnp.int32, (rows, ncp), 0)
    qpos = i * tq + (row & (tq - 1))
    n = lax.broadcasted_iota(jnp.int32, (rows, ncp), 1)
    mask = (n * CMP_STRIDE + CMP_LEN - 1 <= qpos) & (n < nc)
    imps = []
    for g in range(G_NSA):
        q = q_ref[HPG * g:HPG * (g + 1)].reshape(rows, HEAD_DIM)
        s = jnp.where(mask, _dot_nt(q, kc_ref[g]), NEG)
        m = jnp.max(s, axis=-1, keepdims=True)
        e = jnp.where(mask, jnp.exp(s - m), 0.0)
        p = e / jnp.maximum(jnp.sum(e, axis=-1, keepdims=True), 1e-30)
        o_ref[HPG * g:HPG * (g + 1)] = _dot(p.astype(BF16), vc_ref[g]).reshape(HPG, tq, HEAD_DIM)
        psum = p[0:tq]
        for h in range(1, HPG):
            psum = psum + p[h * tq:(h + 1) * tq]
        imps.append(_dot_f32ish(psum, imat_ref[...]))
    imp = jnp.concatenate(imps, axis=0)
    blk = lax.broadcasted_iota(jnp.int32, imp.shape, 1)
    cur = (i * tq + (lax.broadcasted_iota(jnp.int32, imp.shape, 0) & (tq - 1))) // SLC_LEN
    bias = ((_select_blocks(imp, cur, blk) - 1.0) * SEL_BIAS).astype(BF16)
    for g in range(G_NSA):
        sel_ref[g] = bias[tq * g:tq * (g + 1)]


def _cmp_select(q, kcmp, vcmp, nc):
    _, t, d = q.shape
    g, ncp, _ = kcmp.shape
    tq = min(512, t)
    imat = _importance_matrix(ncp, LANES)
    return pl.pallas_call(
        functools.partial(_cmp_kernel, tq=tq, nc=nc),
        grid=(t // tq,),
        in_specs=[pl.BlockSpec((H_NSA, tq, d), lambda i: (0, i, 0)),
                  pl.BlockSpec((g, ncp, d), lambda i: (0, 0, 0)),
                  pl.BlockSpec((g, ncp, d), lambda i: (0, 0, 0)),
                  pl.BlockSpec((ncp, LANES), lambda i: (0, 0))],
        out_specs=[pl.BlockSpec((H_NSA, tq, d), lambda i: (0, i, 0)),
                   pl.BlockSpec((g, tq, LANES), lambda i: (0, i, 0))],
        out_shape=[jax.ShapeDtypeStruct((H_NSA, t, d), F32), jax.ShapeDtypeStruct((g, t, LANES), BF16)],
        compiler_params=_cparams(("arbitrary",)),
        name="nsa_cmp_select",
    )(q, kcmp, vcmp, imat)


def _merge_kernel(osb_ref, odf_ref, ocm_ref, osl_ref, owi_ref, gate_ref, lam_ref, dn_ref, x_ref, ga_ref, nrm_ref,
                  w_ref, o_ref, cat_scr, *, lam_init):
    lv = lam_ref[...]
    lam = (jnp.exp(jnp.sum(lv[0:1] * lv[1:2], axis=-1, keepdims=True))
           - jnp.exp(jnp.sum(lv[2:3] * lv[3:4], axis=-1, keepdims=True)) + lam_init)
    for h in range(H_SB):
        cat_scr[:, h * HEAD_DIM:(h + 1) * HEAD_DIM] = osb_ref[h].astype(BF16)
    dn = dn_ref[...]
    for h in range(H_DIFF):
        ob = odf_ref[2 * h] - lam * odf_ref[2 * h + 1]
        ob = _rms_rows(ob, dn) * (1.0 - lam_init)
        c0 = (H_SB + h) * HEAD_DIM
        cat_scr[:, c0:c0 + HEAD_DIM] = ob.astype(BF16)
    gates = gate_ref[...]
    for h in range(H_NSA):
        oc = (gates[:, 3 * h:3 * h + 1] * ocm_ref[h] + gates[:, 3 * h + 1:3 * h + 2] * osl_ref[h]
              + gates[:, 3 * h + 2:3 * h + 3] * owi_ref[h])
        c0 = (H_SB + H_DIFF + h) * HEAD_DIM
        cat_scr[:, c0:c0 + HEAD_DIM] = oc.astype(BF16)
    y = _dot(cat_scr[...], w_ref[...])
    o_ref[...] = x_ref[...] + ga_ref[...] * _rms_rows(y, nrm_ref[...])


def _merge(osb, odf, ocm, osl, owi, gates, lam_vecs, diff_norm, x, ga, nrm, w, layer):
    t, d = x.shape
    tm = min(256, t)
    lam_init = 0.8 - 0.6 * math.exp(-0.3 * layer)
    hspec = lambda n: pl.BlockSpec((n, tm, HEAD_DIM), lambda i: (0, i, 0))
    return pl.pallas_call(
        functools.partial(_merge_kernel, lam_init=lam_init),
        grid=(t // tm,),
        in_specs=[hspec(H_SB), hspec(2 * H_DIFF), hspec(H_NSA), hspec(H_NSA), hspec(H_NSA),
                  pl.BlockSpec((tm, LANES), lambda i: (i, 0)),
                  pl.BlockSpec(lam_vecs.shape, lambda i: (0, 0)),
                  pl.BlockSpec((1, HEAD_DIM), lambda i: (0, 0)),
                  pl.BlockSpec((tm, d), lambda i: (i, 0)),
                  _row_spec(ga, tm, 1),
                  pl.BlockSpec((1, d), lambda i: (0, 0)),
                  pl.BlockSpec(w.shape, lambda i: (0, 0))],
        out_specs=pl.BlockSpec((tm, d), lambda i: (i, 0)),
        out_shape=jax.ShapeDtypeStruct((t, d), F32),
        scratch_shapes=[pltpu.VMEM((tm, N_HEADS * HEAD_DIM), BF16)],
        compiler_params=_cparams(("arbitrary",)),
        name="merge_out_proj",
    )(osb, odf, ocm, osl, owi, gates, lam_vecs, diff_norm, x, ga, nrm, w)


def _ffn_kernel(x_ref, g2_ref, sc_ref, sh_ref, wg_ref, wu_ref, wd_ref, gf_ref, g3_ref, o_ref, h_scr, acc_scr):
    f = pl.program_id(1)

    @pl.when(f == 0)
    def _():
        y = _rms_rows(x_ref[...], g2_ref[...])
        h_scr[...] = (y * (1.0 + sc_ref[...]) + sh_ref[...]).astype(BF16)
        acc_scr[...] = jnp.zeros_like(acc_scr)

    h = h_scr[...]
    a = _dot(h, wg_ref[...])
    b = _dot(h, wu_ref[...])
    acc_scr[...] += _dot((_silu(a) * b).astype(BF16), wd_ref[...])

    @pl.when(f == pl.num_programs(1) - 1)
    def _():
        o_ref[...] = x_ref[...] + gf_ref[...] * _rms_rows(acc_scr[...], g3_ref[...])


def _ffn(x, g2, sc, sh, wgu, wd, gf, g3):
    t, d = x.shape
    dff = wd.shape[0]
    tm = min(512, t)
    tf = 512 if dff % 512 == 0 else dff
    nf = dff // tf
    return pl.pallas_call(
        _ffn_kernel,
        grid=(t // tm, nf),
        in_specs=[pl.BlockSpec((tm, d), lambda i, f: (i, 0)),
                  pl.BlockSpec((1, d), lambda i, f: (0, 0)),
                  _row_spec(sc, tm, 2), _row_spec(sh, tm, 2),
                  pl.BlockSpec((d, tf), lambda i, f: (0, f)),
                  pl.BlockSpec((d, tf), lambda i, f: (0, f + nf)),
                  pl.BlockSpec((tf, d), lambda i, f: (f, 0)),
                  _row_spec(gf, tm, 2),
                  pl.BlockSpec((1, d), lambda i, f: (0, 0))],
        out_specs=pl.BlockSpec((tm, d), lambda i, f: (i, 0)),
        out_shape=jax.ShapeDtypeStruct((t, d), F32),
        scratch_shapes=[pltpu.VMEM((tm, d), BF16), pltpu.VMEM((tm, d), F32)],
        compiler_params=_cparams(("arbitrary", "arbitrary")),
        name="ffn_swiglu",
    )(x, g2, sc, sh, wgu, wgu, wd, gf, g3)


DEC_ROWS = 32
BLK_LANES = 256


def _dec_kernel(pt_ref, *refs, pg, nps, past, ns):
    page_refs = refs[:pg]
    (new_ref, qsb_ref, qdf_ref, qsl_ref, u_ref, w1_ref, bsel_ref,
     osb_ref, odf_ref, sacc_ref, sm_ref, sl_ref, cab_ref,
     carry_scr, asb_scr, mdf_scr, adf_scr) = refs[pg:]
    j = pl.program_id(1)
    chunks_per_page = PAGE_SIZE // CMP_STRIDE

    @pl.when(j == 0)
    def _():
        carry_scr[...] = jnp.zeros_like(carry_scr)
        asb_scr[...] = jnp.zeros_like(asb_scr)
        mdf_scr[...] = jnp.full_like(mdf_scr, NEG)
        adf_scr[...] = jnp.zeros_like(adf_scr)

    def step(prefs, page0, masked):
        n = len(prefs)
        nk = n * PAGE_SIZE
        lane = lax.broadcasted_iota(jnp.int32, (DEC_ROWS, nk), 1)
        row = lax.broadcasted_iota(jnp.int32, (DEC_ROWS, nk), 0)
        kpos = page0 * PAGE_SIZE + lane
        qpos = past + (row & (ns - 1))

        def tok_rows(r):
            return jnp.concatenate([p[pl.ds(r, PAGE_SIZE, stride=ROWS_PER_TOK), :] for p in prefs], axis=0).astype(BF16)

        def stick_breaking():
            z = jnp.concatenate([_dot_nt(qsb_ref[8 * h:8 * h + 8, :], tok_rows(2 * h)) for h in range(H_SB)], axis=0)
            l1p = jnp.log(1.0 + jnp.exp(-jnp.abs(z)))
            ls_pos = jnp.minimum(z, 0.0) - l1p
            lk = ls_pos - z
            if masked:
                valid = kpos < qpos
                lk = jnp.where(valid, lk, 0.0)
            hi = lk.astype(BF16)
            lo = (lk - hi.astype(F32)).astype(BF16)
            pieces = ([hi[:, PAGE_SIZE * i:PAGE_SIZE * (i + 1)] for i in range(n)]
                      + [lo[:, PAGE_SIZE * i:PAGE_SIZE * (i + 1)] for i in range(n)])
            local = _dot(jnp.concatenate(pieces, axis=0), u_ref[...])
            run = carry_scr[...]
            after = [None] * n
            for i in reversed(range(n)):
                after[i] = (local[DEC_ROWS * i:DEC_ROWS * (i + 1)] + local[DEC_ROWS * (n + i):DEC_ROWS * (n + i + 1)]
                            + run)
                run = run + jnp.sum(lk[:, PAGE_SIZE * i:PAGE_SIZE * (i + 1)], axis=-1, keepdims=True)
            w = jnp.exp(ls_pos + jnp.concatenate(after, axis=1))
            if masked:
                w = jnp.where(valid, w, 0.0)
            wb = w.astype(BF16)
            asb_scr[...] += jnp.concatenate(
                [_dot(wb[8 * h:8 * h + 8, :], tok_rows(2 * h + 1)) for h in range(H_SB)], axis=0)
            carry_scr[...] = run

        if masked:
            stick_breaking()
        else:
            pl.when(jnp.max(carry_scr[...]) > SB_UNDERFLOW)(stick_breaking)

        r0 = 2 * H_SB
        s = jnp.concatenate([_dot_nt(qdf_ref[8 * h:8 * h + 8, :], tok_rows(r0 + 2 * h)) for h in range(H_DIFF)], axis=0)
        if masked:
            s = jnp.where(kpos <= qpos, s, NEG)
        m_old = mdf_scr[...]
        m_new = jnp.maximum(m_old, jnp.max(s, axis=-1, keepdims=True))
        eb = jnp.exp(s - m_new).astype(BF16)
        ones = jnp.ones((nk, HEAD_DIM), BF16)
        adf_scr[...] = jnp.exp(m_old - m_new) * adf_scr[...] + jnp.concatenate(
            [_dot(eb[8 * h:8 * h + 8, :], jnp.concatenate([tok_rows(r0 + 2 * h + 1), ones], axis=1))
             for h in range(H_DIFF)], axis=0)
        mdf_scr[...] = m_new

        r0 = 2 * (H_SB + H_DIFF + G_NSA)
        s = jnp.concatenate([_dot_nt(qsl_ref[16 * g:16 * g + 16, :], tok_rows(r0 + 2 * g)) for g in range(G_NSA)], axis=0)
        if masked:
            valid = kpos <= qpos
            s = jnp.where(valid, s, NEG)
        blk = lane // SLC_LEN
        col = lax.broadcasted_iota(jnp.int32, (DEC_ROWS, LANES), 1)
        mref = jnp.zeros((DEC_ROWS, nk), F32)
        mcols = jnp.full((DEC_ROWS, LANES), NEG, F32)
        for b in range(2 * n):
            mb = jnp.max(jnp.where(blk == b, s, NEG), axis=-1, keepdims=True)
            mref = jnp.where(blk == b, mb, mref)
            mcols = jnp.where(col == b, mb, mcols)
        e = jnp.exp(s - mref)
        if masked:
            e = jnp.where(valid, e, 0.0)
        eb = e.astype(BF16)
        sm_ref[...] = mcols
        sl_ref[...] = _dot(eb, bsel_ref[0:nk, :])
        blk16 = lax.broadcasted_iota(jnp.int32, (16, nk), 1) // SLC_LEN
        for g in range(G_NSA):
            eg = eb[16 * g:16 * g + 16, :]
            per_blk = jnp.concatenate([jnp.where(blk16 == b, eg, jnp.zeros_like(eg)) for b in range(2 * n)], axis=0)
            res = _dot(per_blk, tok_rows(r0 + 2 * g + 1))
            for b in range(2 * n):
                sacc_ref[b, 16 * g:16 * g + 16, :] = res[16 * b:16 * b + 16]
        for b in range(2 * n, 2 * pg):
            sacc_ref[b] = jnp.zeros((DEC_ROWS, HEAD_DIM), F32)

        r0 = 2 * (H_SB + H_DIFF)
        lhs = jnp.concatenate(
            [jnp.concatenate([p[pl.ds((CMP_STRIDE * m + l) * ROWS_PER_TOK + r0, 8), :] for l in range(CMP_STRIDE)], axis=1)
             for p in prefs for m in range(chunks_per_page)], axis=0).astype(BF16)
        res = _dot(lhs, w1_ref[...])
        crow = lax.broadcasted_iota(jnp.int32, (n * 8 * chunks_per_page, 2 * HEAD_DIM), 0)
        keep = jnp.where((crow & 1) == 0, res[:, :2 * HEAD_DIM], res[:, 2 * HEAD_DIM:])
        live = n * 8 * chunks_per_page
        for half in range(2):
            cab_ref[half, 0:live, :] = keep[:, half * HEAD_DIM:(half + 1) * HEAD_DIM]
            if n < pg:
                cab_ref[half, live:, :] = jnp.zeros(((pg - n) * 8 * chunks_per_page, HEAD_DIM), F32)

    @pl.when(j == 0)
    def _():
        step([new_ref], nps * pg, True)

    @pl.when(j > 0)
    def _():
        step(list(page_refs), (nps - j) * pg, False)

    @pl.when(j == nps)
    def _():
        osb_ref[...] = asb_scr[...]
        adf = adf_scr[...]
        odf_ref[...] = adf[:, :HEAD_DIM] / jnp.maximum(adf[:, HEAD_DIM:], 1e-30)


def _dec_main(layer, page_table, cache_rows, new_page, qsb, qdf, qsl, w1all, past, ns):
    nb, npages = page_table.shape
    pg = next(c for c in (8, 4, 2, 1) if npages % c == 0 and npages >= 2 * c)
    nps = npages // pg
    rows_per_page = PAGE_SIZE * ROWS_PER_TOK

    def page_spec(i):
        def imap(b, j, pt):
            return (layer, pt[b, (nps - jnp.maximum(j, 1)) * pg + i], 0, 0)
        return pl.BlockSpec((None, None, rows_per_page, HEAD_DIM), imap)

    qspec = pl.BlockSpec((None, DEC_ROWS, HEAD_DIM), lambda b, j, pt: (b, 0, 0))
    nblk_pad = (nps + 1) * 2 * pg
    cab_rows = pg * 8 * (PAGE_SIZE // CMP_STRIDE)
    key = np.arange(pg * PAGE_SIZE)[:, None]
    bsel = jnp.asarray((key // SLC_LEN == np.arange(LANES)[None, :]).astype(np.float32), dtype=BF16)
    stat_spec = pl.BlockSpec((None, None, DEC_ROWS, LANES), lambda b, j, pt: (b, nps - j, 0, 0))
    grid_spec = pltpu.PrefetchScalarGridSpec(
        num_scalar_prefetch=1,
        grid=(nb, nps + 1),
        in_specs=[page_spec(i) for i in range(pg)] + [
            pl.BlockSpec((None, rows_per_page, HEAD_DIM), lambda b, j, pt: (b, 0, 0)),
            qspec, qspec, qspec,
            pl.BlockSpec((PAGE_SIZE, PAGE_SIZE), lambda b, j, pt: (0, 0)),
            pl.BlockSpec(w1all.shape, lambda b, j, pt: (0, 0)),
            pl.BlockSpec(bsel.shape, lambda b, j, pt: (0, 0))],
        out_specs=[
            qspec, qspec,
            pl.BlockSpec((None, 2 * pg, DEC_ROWS, HEAD_DIM), lambda b, j, pt: (b, nps - j, 0, 0)),
            stat_spec, stat_spec,
            pl.BlockSpec((None, 2, cab_rows, HEAD_DIM), lambda b, j, pt: (b, 0, nps - j, 0))],
        scratch_shapes=[pltpu.VMEM((DEC_ROWS, 1), F32), pltpu.VMEM((DEC_ROWS, HEAD_DIM), F32),
                        pltpu.VMEM((DEC_ROWS, 1), F32), pltpu.VMEM((DEC_ROWS, 2 * HEAD_DIM), F32)])
    out_shape = [jax.ShapeDtypeStruct((nb, DEC_ROWS, HEAD_DIM), F32),
                 jax.ShapeDtypeStruct((nb, DEC_ROWS, HEAD_DIM), F32),
                 jax.ShapeDtypeStruct((nb, nblk_pad, DEC_ROWS, HEAD_DIM), F32),
                 jax.ShapeDtypeStruct((nb, nps + 1, DEC_ROWS, LANES), F32),
                 jax.ShapeDtypeStruct((nb, nps + 1, DEC_ROWS, LANES), F32),
                 jax.ShapeDtypeStruct((nb, 2, (nps + 1) * cab_rows, HEAD_DIM), F32)]
    o_sb, o_df, sacc, sm, sl, cab = pl.pallas_call(
        functools.partial(_dec_kernel, pg=pg, nps=nps, past=past, ns=ns),
        grid_spec=grid_spec,
        out_shape=out_shape,
        compiler_params=_cparams(("arbitrary", "arbitrary")),
        name="decode_paged",
    )(page_table, *([cache_rows] * pg), new_page, qsb, qdf, qsl, _suffix_matrix(PAGE_SIZE), w1all, bsel)

    def by_block(stat, fill):
        stat = stat[..., :2 * pg].transpose(0, 2, 1, 3).reshape(nb, DEC_ROWS, nblk_pad)
        return jnp.pad(stat, ((0, 0), (0, 0), (0, BLK_LANES - nblk_pad)), constant_values=fill)

    return o_sb, o_df, sacc, by_block(sm, NEG), by_block(sl, 0.0), cab


def _cmp_bias_kernel(pe_ref, w1_ref, o_ref):
    for kv in range(2):
        for c in range(2):
            o_ref[kv, c] = _dot(jnp.broadcast_to(pe_ref[kv, c:c + 1, :], (8, pe_ref.shape[-1])).astype(BF16),
                                w1_ref[kv, c].astype(BF16))


def _cmp_bias(pe, w1):
    return pl.pallas_call(
        _cmp_bias_kernel,
        out_shape=jax.ShapeDtypeStruct((2, 2, 8, HEAD_DIM), F32),
        compiler_params=pltpu.CompilerParams(vmem_limit_bytes=VMEM_LIMIT),
        name="cmp_bias",
    )(pe, w1)


def _dec_final_kernel(cab_ref, bias_ref, w2_ref, q_ref, imat_ref, jmat_ref, sacc_ref, sm_ref, sl_ref,
                      swin_ref, nwin_ref, *rest, nc, past, ns, nblk_pad, wb):
    ocm_ref, osl_ref, owi_ref, wout_ref = rest[-4:]
    win_rows, new_rows = swin_ref.shape[0], ns * 2 * G_NSA
    wout_ref[0:win_rows - new_rows, :] = swin_ref[new_rows:win_rows, :]
    wout_ref[win_rows - new_rows:win_rows, :] = nwin_ref[0:new_rows, :]
    ncp = cab_ref.shape[1] // 8
    rows16 = lax.broadcasted_iota(jnp.int32, (16, 1), 0)
    qpos16 = past + (rows16 & (ns - 1))
    imps = []
    for g in range(G_NSA):
        q = q_ref[16 * g:16 * g + 16, :]
        cmp_kv = []
        for kv in range(2):
            a = cab_ref[0, pl.ds(2 * g + kv, ncp, stride=8), :]
            b = cab_ref[1, pl.ds(2 * g + kv, ncp, stride=8), :]
            h = a + pltpu.roll(b, ncp - 1, 0) + bias_ref[kv, 0, 0:1, :] + bias_ref[kv, 1, 0:1, :]
            cmp_kv.append(_dot(_silu(h).astype(BF16), w2_ref[kv].astype(BF16)).astype(BF16))
        kc, vc = cmp_kv
        s = _dot_nt(q, kc)
        n = lax.broadcasted_iota(jnp.int32, s.shape, 1)
        mask = (n * CMP_STRIDE + CMP_LEN - 1 <= qpos16) & (n < nc)
        s = jnp.where(mask, s, NEG)
        m = jnp.max(s, axis=-1, keepdims=True)
        e = jnp.where(mask, jnp.exp(s - m), 0.0)
        p = e / jnp.maximum(jnp.sum(e, axis=-1, keepdims=True), 1e-30)
        ocm_ref[16 * g:16 * g + 16, :] = _dot(p.astype(BF16), vc)
        hi, mid, lo = _split3(p)
        jm = jmat_ref[...]
        psum = _dot(jm, hi) + _dot(jm, mid) + _dot(jm, lo)
        imps.append(_dot_f32ish(psum, imat_ref[...]))
    imp = jnp.concatenate(imps, axis=0)
    blk = lax.broadcasted_iota(jnp.int32, imp.shape, 1)
    cur = (past + (lax.broadcasted_iota(jnp.int32, imp.shape, 0) & (ns - 1))) // SLC_LEN
    sel_all = _select_blocks(imp, cur, blk)
    for g in range(G_NSA):
        q = q_ref[16 * g:16 * g + 16, :]
        sel = sel_all[16 * g:16 * g + 16]
        mb = sm_ref[16 * g:16 * g + 16, :]
        lb = sl_ref[16 * g:16 * g + 16, :]
        picked = sel > 0.5
        mtot = jnp.max(jnp.where(picked, mb, NEG), axis=-1, keepdims=True)
        wgt = jnp.where(picked, jnp.exp(mb - mtot), 0.0)
        den = jnp.sum(wgt * lb, axis=-1, keepdims=True)
        wgt_b = wgt.astype(BF16)
        pad = jnp.zeros((BLK_LANES - nblk_pad, HEAD_DIM), F32)
        for half in range(2):
            rsel = lax.broadcasted_iota(jnp.int32, (8, HEAD_DIM), 0)
            num = jnp.zeros((8, HEAD_DIM), F32)
            for i in range(8):
                r32 = 16 * g + 8 * half + i
                acc = jnp.concatenate([sacc_ref[pl.ds(r32, nblk_pad, stride=DEC_ROWS), :], pad], axis=0)
                ahi = acc.astype(BF16)
                alo = (acc - ahi.astype(F32)).astype(BF16)
                w8 = wgt_b[8 * half:8 * half + 8, :]
                res = _dot(w8, ahi) + _dot(w8, alo)
                num = jnp.where(rsel == i, res, num)
            r0 = 16 * g + 8 * half
            osl_ref[r0:r0 + 8, :] = num / jnp.maximum(den[8 * half:8 * half + 8], 1e-30)
        kw = swin_ref[pl.ds(g, wb, stride=2 * G_NSA), :].astype(BF16)
        vw = swin_ref[pl.ds(G_NSA + g, wb, stride=2 * G_NSA), :].astype(BF16)
        kn = nwin_ref[pl.ds(g, 8, stride=2 * G_NSA), :].astype(BF16)
        vn = nwin_ref[pl.ds(G_NSA + g, 8, stride=2 * G_NSA), :].astype(BF16)
        s1 = _dot_nt(q, kw)
        s2 = _dot_nt(q, kn)
        d1 = qpos16 - (past - wb + lax.broadcasted_iota(jnp.int32, s1.shape, 1))
        k1 = (d1 >= 0) & (d1 <= WINDOW)
        d2 = qpos16 - (past + lax.broadcasted_iota(jnp.int32, s2.shape, 1))
        k2 = (d2 >= 0) & (d2 <= WINDOW)
        s1 = jnp.where(k1, s1, NEG)
        s2 = jnp.where(k2, s2, NEG)
        m = jnp.maximum(jnp.max(s1, axis=-1, keepdims=True), jnp.max(s2, axis=-1, keepdims=True))
        e1 = jnp.where(k1, jnp.exp(s1 - m), 0.0)
        e2 = jnp.where(k2, jnp.exp(s2 - m), 0.0)
        den = jnp.sum(e1, axis=-1, keepdims=True) + jnp.sum(e2, axis=-1, keepdims=True)
        owi_ref[16 * g:16 * g + 16, :] = (_dot(e1.astype(BF16), vw) + _dot(e2.astype(BF16), vn)) / jnp.maximum(den, 1e-30)


def _dec_final(layer, cab, bias, w2, qsl, sacc, sm, sl, state_rows, new_win, nc, past, ns, win_stack):
    nb = cab.shape[0]
    depth = state_rows.shape[0]
    ncp = cab.shape[2] // 8
    nblk_pad = sacc.shape[1] // DEC_ROWS
    wb = state_rows.shape[2] // (2 * G_NSA)
    imat = _importance_matrix(ncp, BLK_LANES)
    t_of = np.arange(16) % ns
    jmat = jnp.asarray((t_of[:, None] == t_of[None, :]).astype(np.float32), dtype=BF16)
    ospec = pl.BlockSpec((None, DEC_ROWS, HEAD_DIM), lambda b: (b, 0, 0))
    wspec = pl.BlockSpec((None, None, state_rows.shape[2], HEAD_DIM), lambda b: (layer, b, 0, 0))
    in_specs = [pl.BlockSpec((None, 2, ncp * 8, HEAD_DIM), lambda b: (b, 0, 0, 0)),
                pl.BlockSpec(bias.shape, lambda b: (0, 0, 0, 0)),
                pl.BlockSpec(w2.shape, lambda b: (0, 0, 0)),
                ospec,
                pl.BlockSpec(imat.shape, lambda b: (0, 0)),
                pl.BlockSpec(jmat.shape, lambda b: (0, 0)),
                pl.BlockSpec((None, nblk_pad * DEC_ROWS, HEAD_DIM), lambda b: (b, 0, 0)),
                pl.BlockSpec((None, DEC_ROWS, BLK_LANES), lambda b: (b, 0, 0)),
                pl.BlockSpec((None, DEC_ROWS, BLK_LANES), lambda b: (b, 0, 0)),
                wspec,
                pl.BlockSpec((None, new_win.shape[1], HEAD_DIM), lambda b: (b, 0, 0))]
    args = [cab, bias, w2, qsl, imat, jmat, sacc, sm, sl, state_rows, new_win]
    aliases = {}
    if win_stack is not None:
        in_specs.append(pl.BlockSpec(memory_space=pl.ANY))
        args.append(win_stack)
        aliases = {len(args) - 1: 3}
    return pl.pallas_call(
        functools.partial(_dec_final_kernel, nc=nc, past=past, ns=ns, nblk_pad=nblk_pad, wb=wb),
        grid=(nb,),
        in_specs=in_specs,
        out_specs=[ospec, ospec, ospec, wspec],
        out_shape=[jax.ShapeDtypeStruct((nb, DEC_ROWS, HEAD_DIM), F32)] * 3 + [
            jax.ShapeDtypeStruct((depth,) + state_rows.shape[1:], F32)],
        input_output_aliases=aliases,
        compiler_params=_cparams(("arbitrary",)),
        name="decode_nsa_finish",
    )(*args)


def _sample_mixer(x, mod, lw, layer, tabs, page_table, cache_rows, state_rows, nb, ns, past, win_stack):
    sh_a, sc_a, g_a = mod[0], mod[1], mod[2]
    proj = _proj(x, lw["n0"], sc_a, sh_a, lw["w_in"])
    (qsb, _, _, qdf, _, _, qns, _, _, _, _, _, _, gates, kv_rows, win_rows) = _postproj(proj, tabs)

    def per_seq(a, heads, rows_per_head):
        a = a.reshape(heads, nb, ns, HEAD_DIM).transpose(1, 0, 2, 3)
        a = jnp.pad(a, ((0, 0), (0, 0), (0, rows_per_head - ns), (0, 0)))
        return a.reshape(nb, heads * rows_per_head, HEAD_DIM)

    qsb_d = per_seq(qsb, H_SB, 8)
    qdf_d = per_seq(qdf, 2 * H_DIFF, ns)
    qsl_d = per_seq(qns, H_NSA, ns)
    kv_tok = kv_rows.reshape(nb, ns * ROWS_PER_TOK, HEAD_DIM)
    new_page = jnp.pad(kv_tok, ((0, 0), (0, (PAGE_SIZE - ns) * ROWS_PER_TOK), (0, 0)))
    w1 = lw["cmp_w1"]
    w1all = jnp.concatenate([w1[0, 0], w1[0, 1], w1[1, 0], w1[1, 1]], axis=-1).astype(BF16)
    o_sb, o_df, sacc, sm, sl, cab = _dec_main(layer, page_table, cache_rows, new_page, qsb_d, qdf_d, qsl_d, w1all,
                                              past, ns)
    bias = _cmp_bias(lw["cmp_pe"], w1)
    new_win = jnp.pad(win_rows.reshape(nb, ns * 2 * G_NSA, HEAD_DIM), ((0, 0), (0, (8 - ns) * 2 * G_NSA), (0, 0)))
    padded = -(-(past + ns) // SLC_LEN) * SLC_LEN
    nc = padded // CMP_STRIDE - CMP_LEN // CMP_STRIDE + 1
    o_cmp, o_slc, o_win, win_stack = _dec_final(layer, cab, bias, lw["cmp_w2"], qsl_d, sacc.reshape(nb, -1, HEAD_DIM),
                                                sm, sl, state_rows, new_win, nc, past, ns, win_stack)

    def head_major(a, heads, rows_per_head):
        a = a.reshape(nb, heads, rows_per_head, HEAD_DIM)[:, :, :ns]
        return a.transpose(1, 0, 2, 3).reshape(heads, nb * ns, HEAD_DIM)

    x1 = _merge(head_major(o_sb, H_SB, 8), head_major(o_df, 2 * H_DIFF, ns), head_major(o_cmp, H_NSA, ns),
                head_major(o_slc, H_NSA, ns), head_major(o_win, H_NSA, ns), gates, lw["diff_lambda"],
                lw["diff_norm"], x, g_a, lw["n1"], lw["w_out"], layer)
    return x1, kv_tok, win_stack


def _split_mod(mod):
    return [mod[:, i * (mod.shape[1] // 6):(i + 1) * (mod.shape[1] // 6)] for i in range(6)]


def _prompt_mixer(x, mod, lw, layer, tabs, depth, kv_stack):
    t = x.shape[0]
    sh_a, sc_a, g_a = mod[0], mod[1], mod[2]
    proj = _proj(x, lw["n0"], sc_a, sh_a, lw["w_in"])
    (qsb, ksb, vsb, qdf, kdf, vdf, qns, kc, vc, ks, vs, kw, vw, gates, kv_rows, win_rows) = _postproj(
        proj, tabs, layer, depth, kv_stack)
    o_sb = _sb_attention(qsb, ksb, vsb)
    o_df = _flash(qdf, kdf, vdf, "causal")
    nch = t // CMP_STRIDE
    xc = jnp.stack([kc, vc]).reshape(2, G_NSA, nch, CMP_STRIDE * HEAD_DIM)
    cmp_kv = _compress(xc, lw["cmp_pe"], lw["cmp_w1"], lw["cmp_w2"])
    o_cmp, sel = _cmp_select(qns, cmp_kv[0], cmp_kv[1], nch - 1)
    o_slc = _flash(qns, ks, vs, "select", sel)
    o_win = _flash(qns, kw, vw, "window")
    x1 = _merge(o_sb, o_df, o_cmp, o_slc, o_win, gates, lw["diff_lambda"], lw["diff_norm"], x, g_a, lw["n1"],
                lw["w_out"], layer)
    return x1, kv_rows, win_rows


def _layer_weights(layer, norms, w_in_p, w_out_b, diff_lambda, diff_norm, cmp_pe, cmp_w1, cmp_w2, wgu_b, wd_b):
    d = norms.shape[-1]
    return dict(
        n0=norms[layer, 0].reshape(1, d), n1=norms[layer, 1].reshape(1, d),
        n2=norms[layer, 2].reshape(1, d), n3=norms[layer, 3].reshape(1, d),
        w_in=w_in_p[layer], w_out=w_out_b[layer],
        diff_lambda=diff_lambda[layer], diff_norm=diff_norm[layer].reshape(1, HEAD_DIM),
        cmp_pe=cmp_pe[layer].reshape(2, 2, CMP_STRIDE * HEAD_DIM),
        cmp_w1=cmp_w1[layer].reshape(2, 2, CMP_STRIDE * HEAD_DIM, HEAD_DIM),
        cmp_w2=cmp_w2[layer], w_gate_up=wgu_b[layer], w_down=wd_b[layer])


def kernel(x_prompt, x_sample, cache_kv, state_win_kv, page_table, c_prompt, c_sample, ada_w, ada_b, norms, w_in,
           w_out, diff_lambda, diff_norm, cmp_pe, cmp_w1, cmp_w2, w_gate_up, w_down):
    depth = ada_w.shape[0]
    _, seq, d = x_prompt.shape
    nb, ns, _ = x_sample.shape
    past = page_table.shape[1] * PAGE_SIZE

    w_in_p = jnp.pad(w_in, ((0, 0), (0, 0), (0, N_IN_PAD - N_IN))).astype(BF16)
    w_out_b = w_out.astype(BF16)
    wgu_b = w_gate_up.astype(BF16)
    wd_b = w_down.astype(BF16)

    n_c = 1 + nb
    rows = -(-n_c // 8) * 8
    c_all = jnp.pad(jnp.concatenate([c_prompt, c_sample], axis=0), ((0, rows - n_c), (0, 0)))
    mod_all = _ada(c_all, ada_w, ada_b)

    tabs_p = _rope_tables(jnp.arange(seq))
    tabs_s = _rope_tables(past + jnp.arange(nb * ns) % ns)
    cache_rows = cache_kv.transpose(0, 1, 2, 4, 3, 5).reshape(depth, cache_kv.shape[1], PAGE_SIZE * ROWS_PER_TOK,
                                                               HEAD_DIM)
    wb_s = state_win_kv.shape[2]
    state_rows = state_win_kv.reshape(depth, nb, wb_s * 2 * G_NSA, HEAD_DIM)
    xp = x_prompt.reshape(seq, d)
    xs = x_sample.reshape(nb * ns, d)
    win_p, kv_s = [], []
    kv_stack = win_stack = None
    for layer in range(depth):
        lw = _layer_weights(layer, norms, w_in_p, w_out_b, diff_lambda, diff_norm, cmp_pe, cmp_w1, cmp_w2, wgu_b, wd_b)
        mod_p = _split_mod(mod_all[layer, 0:1])
        x1, kv_stack, win_rows = _prompt_mixer(xp, mod_p, lw, layer, tabs_p, depth, kv_stack)
        xp = _ffn(x1, lw["n2"], mod_p[4], mod_p[3], lw["w_gate_up"], lw["w_down"], mod_p[5], lw["n3"])
        wb = min(WINDOW, seq)
        win_p.append(win_rows[seq - wb:].reshape(1, wb, 2, G_NSA, HEAD_DIM))

        mod_s = _split_mod(jnp.repeat(mod_all[layer, 1:1 + nb], ns, axis=0))
        x1, kv_tok, win_stack = _sample_mixer(xs, mod_s, lw, layer, tabs_s, page_table, cache_rows, state_rows,
                                              nb, ns, past, win_stack)
        xs = _ffn(x1, lw["n2"], mod_s[4], mod_s[3], lw["w_gate_up"], lw["w_down"], mod_s[5], lw["n3"])
        kv_s.append(kv_tok.reshape(nb, ns, N_KV, 2, HEAD_DIM).transpose(0, 1, 3, 2, 4))
    kv_prompt = kv_stack.reshape(depth, 1, seq, N_KV, 2, HEAD_DIM).transpose(0, 1, 2, 4, 3, 5)
    win_sample = win_stack.reshape(depth, nb, wb_s, 2, G_NSA, HEAD_DIM)
    return (xp.reshape(1, seq, d), xs.reshape(nb, ns, d), kv_prompt, jnp.stack(kv_s), jnp.stack(win_p), win_sample)
```

```python
import functools
import math

import numpy as np
import jax
import jax.numpy as jnp
from jax import lax
from jax.experimental import pallas as pl
from jax.experimental.pallas import tpu as pltpu

F32 = jnp.float32
BF16 = jnp.bfloat16

HEAD_DIM = 128
H_SB = 4
H_DIFF = 4
H_NSA = 8
G_NSA = 2
HPG = H_NSA // G_NSA
N_KV = H_SB + H_DIFF + 2 * G_NSA
N_HEADS = H_SB + H_DIFF + H_NSA
DIFF_DIM = HEAD_DIM // 2
CMP_LEN = 32
CMP_STRIDE = 16
SLC_LEN = 64
SLC_TOPK = 16
WINDOW = 512
PAGE_SIZE = 128
ROPE_THETA = 10000.0
EPS = 1e-6
NEG = -1e30
LANES = 128
VMEM_LIMIT = 56 * 1024 * 1024

C_QA, C_KA, C_VA = 0, 512, 1024
C_QB, C_KB, C_VB = 1536, 2048, 2560
C_QC = 3072
C_KC, C_VC, C_KS, C_VS, C_KW, C_VW = 4096, 4352, 4608, 4864, 5120, 5376
C_GATE = 5632
N_IN = 5656
N_IN_PAD = 5760
ROWS_PER_TOK = 2 * N_KV


def _cparams(sem):
    return pltpu.CompilerParams(dimension_semantics=sem, vmem_limit_bytes=VMEM_LIMIT)


def _silu(x):
    return x * jax.nn.sigmoid(x)


def _rms_rows(x, g):
    return x * lax.rsqrt(jnp.mean(x * x, axis=-1, keepdims=True) + EPS) * g


def _dot(a, b):
    return jnp.dot(a, b, preferred_element_type=F32)


def _dot_nt(a, b):
    return lax.dot_general(a, b, (((1,), (1,)), ((), ())), preferred_element_type=F32)


def _split3(x):
    hi = x.astype(BF16)
    r1 = x - hi.astype(F32)
    mid = r1.astype(BF16)
    lo = (r1 - mid.astype(F32)).astype(BF16)
    return hi, mid, lo


def _dot_f32ish(x, m01):
    hi, mid, lo = _split3(x)
    return _dot(hi, m01) + _dot(mid, m01) + _dot(lo, m01)


def _ada_kernel(c_ref, w_ref, b_ref, o_ref):
    a = _silu(c_ref[...]).astype(BF16)
    o_ref[...] = _dot(a, w_ref[...].astype(BF16)) + b_ref[...]


def _ada(c_all, ada_w, ada_b):
    depth, d, n = ada_w.shape
    rows = c_all.shape[0]
    tn = 1024
    return pl.pallas_call(
        _ada_kernel,
        grid=(depth, n // tn),
        in_specs=[pl.BlockSpec((rows, d), lambda l, j: (0, 0)),
                  pl.BlockSpec((None, d, tn), lambda l, j: (l, 0, j)),
                  pl.BlockSpec((None, 1, tn), lambda l, j: (l, 0, j))],
        out_specs=pl.BlockSpec((None, rows, tn), lambda l, j: (l, 0, j)),
        out_shape=jax.ShapeDtypeStruct((depth, rows, n), F32),
        compiler_params=_cparams(("arbitrary", "arbitrary")),
        name="ada_mod",
    )(c_all, ada_w, ada_b.reshape(depth, 1, n))


def _row_spec(arr, tm, grid_rank):
    c = arr.shape[1]
    if arr.shape[0] == 1:
        if grid_rank == 1:
            return pl.BlockSpec((1, c), lambda i: (0, 0))
        return pl.BlockSpec((1, c), lambda i, j: (0, 0))
    if grid_rank == 1:
        return pl.BlockSpec((tm, c), lambda i: (i, 0))
    return pl.BlockSpec((tm, c), lambda i, j: (i, 0))


def _proj_kernel(x_ref, g_ref, sc_ref, sh_ref, w_ref, o_ref, h_scr):
    @pl.when(pl.program_id(1) == 0)
    def _():
        y = _rms_rows(x_ref[...], g_ref[...])
        h_scr[...] = (y * (1.0 + sc_ref[...]) + sh_ref[...]).astype(BF16)

    o_ref[...] = _dot(h_scr[...], w_ref[...])


def _proj(x, g, sc, sh, w):
    t, d = x.shape
    n = w.shape[1]
    tm = min(512, t)
    tn = 1920
    return pl.pallas_call(
        _proj_kernel,
        grid=(t // tm, n // tn),
        in_specs=[pl.BlockSpec((tm, d), lambda i, j: (i, 0)),
                  pl.BlockSpec((1, d), lambda i, j: (0, 0)),
                  _row_spec(sc, tm, 2), _row_spec(sh, tm, 2),
                  pl.BlockSpec((d, tn), lambda i, j: (0, j))],
        out_specs=pl.BlockSpec((tm, tn), lambda i, j: (i, j)),
        out_shape=jax.ShapeDtypeStruct((t, n), F32),
        scratch_shapes=[pltpu.VMEM((tm, d), BF16)],
        compiler_params=_cparams(("arbitrary", "arbitrary")),
        name="in_proj",
    )(x, g, sc, sh, w)


def _postproj_kernel(p_ref, ca_ref, sa_ref, cb_ref, sb_ref, *rest):
    (qsb_ref, ksb_ref, vsb_ref, qdf_ref, kdf_ref, vdf_ref, qns_ref,
     kc_ref, vc_ref, ks_ref, vs_ref, kw_ref, vw_ref, gate_ref, kv_ref, win_ref) = rest[-16:]
    tm = p_ref.shape[0]
    ca, sa, cb, sb = ca_ref[...], sa_ref[...], cb_ref[...], sb_ref[...]
    lane = lax.broadcasted_iota(jnp.int32, ca.shape, 1)
    low_half = (lane % DIFF_DIM) < (DIFF_DIM // 2)
    first_chunk = lane < DIFF_DIM
    sm_scale = HEAD_DIM ** -0.5
    df_scale = DIFF_DIM ** -0.5

    def col(c0, h):
        return p_ref[:, c0 + h * HEAD_DIM:c0 + (h + 1) * HEAD_DIM]

    def rope_a(x):
        return x * ca + pltpu.roll(x, HEAD_DIM // 2, 1) * sa

    def rope_b(x):
        partner = jnp.where(low_half, pltpu.roll(x, HEAD_DIM - DIFF_DIM // 2, 1), pltpu.roll(x, DIFF_DIM // 2, 1))
        return x * cb + partner * sb

    def put_kv(is_v, head, val):
        kv_ref[pl.ds(2 * head + is_v, tm, stride=ROWS_PER_TOK), :] = val

    for h in range(H_SB):
        qsb_ref[h] = (col(C_QA, h) * sm_scale).astype(BF16)
        k, v = col(C_KA, h), col(C_VA, h)
        ksb_ref[h] = k.astype(BF16)
        vsb_ref[h] = v.astype(BF16)
        put_kv(0, h, k)
        put_kv(1, h, v)
    for h in range(H_DIFF):
        q = rope_b(col(C_QB, h)) * df_scale
        qdf_ref[2 * h] = jnp.where(first_chunk, q, 0.0).astype(BF16)
        qdf_ref[2 * h + 1] = jnp.where(first_chunk, 0.0, q).astype(BF16)
        k, v = rope_b(col(C_KB, h)), col(C_VB, h)
        kdf_ref[h] = k.astype(BF16)
        vdf_ref[h] = v.astype(BF16)
        put_kv(0, H_SB + h, k)
        put_kv(1, H_SB + h, v)
    for h in range(H_NSA):
        qns_ref[h] = (rope_a(col(C_QC, h)) * sm_scale).astype(BF16)
    for g in range(G_NSA):
        k, v = rope_a(col(C_KC, g)), col(C_VC, g)
        kc_ref[g] = k
        vc_ref[g] = v
        put_kv(0, H_SB + H_DIFF + g, k)
        put_kv(1, H_SB + H_DIFF + g, v)
        k, v = rope_a(col(C_KS, g)), col(C_VS, g)
        ks_ref[g] = k.astype(BF16)
        vs_ref[g] = v.astype(BF16)
        put_kv(0, H_SB + H_DIFF + G_NSA + g, k)
        put_kv(1, H_SB + H_DIFF + G_NSA + g, v)
        k, v = rope_a(col(C_KW, g)), col(C_VW, g)
        kw_ref[g] = k.astype(BF16)
        vw_ref[g] = v.astype(BF16)
        win_ref[:, g * HEAD_DIM:(g + 1) * HEAD_DIM] = k
        win_ref[:, (G_NSA + g) * HEAD_DIM:(G_NSA + g + 1) * HEAD_DIM] = v
    gate_ref[...] = jax.nn.sigmoid(p_ref[:, C_GATE:C_GATE + LANES])


def _postproj(proj, tabs, layer=0, depth=1, kv_stack=None):
    t = proj.shape[0]
    tm = min(256, t)
    hm = lambda n, dt: jax.ShapeDtypeStruct((n, t, HEAD_DIM), dt)
    hspec = lambda n: pl.BlockSpec((n, tm, HEAD_DIM), lambda i: (0, i, 0))
    tab_spec = pl.BlockSpec((tm, LANES), lambda i: (i, 0))
    out_shapes = [hm(H_SB, BF16), hm(H_SB, BF16), hm(H_SB, BF16),
                  hm(2 * H_DIFF, BF16), hm(H_DIFF, BF16), hm(H_DIFF, BF16), hm(H_NSA, BF16),
                  hm(G_NSA, F32), hm(G_NSA, F32), hm(G_NSA, BF16), hm(G_NSA, BF16),
                  hm(G_NSA, BF16), hm(G_NSA, BF16),
                  jax.ShapeDtypeStruct((t, LANES), F32),
                  jax.ShapeDtypeStruct((depth, t * ROWS_PER_TOK, HEAD_DIM), F32),
                  jax.ShapeDtypeStruct((t, 2 * G_NSA * HEAD_DIM), F32)]
    out_specs = [hspec(H_SB), hspec(H_SB), hspec(H_SB), hspec(2 * H_DIFF), hspec(H_DIFF), hspec(H_DIFF),
                 hspec(H_NSA), hspec(G_NSA), hspec(G_NSA), hspec(G_NSA), hspec(G_NSA), hspec(G_NSA), hspec(G_NSA),
                 pl.BlockSpec((tm, LANES), lambda i: (i, 0)),
                 pl.BlockSpec((None, tm * ROWS_PER_TOK, HEAD_DIM), lambda i: (layer, i, 0)),
                 pl.BlockSpec((tm, 2 * G_NSA * HEAD_DIM), lambda i: (i, 0))]
    in_specs = [pl.BlockSpec((tm, N_IN_PAD), lambda i: (i, 0)), tab_spec, tab_spec, tab_spec, tab_spec]
    args = [proj, *tabs]
    aliases = {}
    if kv_stack is not None:
        in_specs.append(pl.BlockSpec(memory_space=pl.ANY))
        args.append(kv_stack)
        aliases = {len(args) - 1: 14}
    return pl.pallas_call(
        _postproj_kernel,
        grid=(t // tm,),
        in_specs=in_specs,
        out_specs=out_specs,
        out_shape=out_shapes,
        input_output_aliases=aliases,
        compiler_params=_cparams(("arbitrary",)),
        name="rope_split",
    )(*args)


def _rope_tables(pos):
    posf = pos.astype(F32)[:, None]

    def tab(half, reps):
        inv = jnp.power(ROPE_THETA, -jnp.arange(half, dtype=F32) / half)
        ang = posf * inv[None, :]
        c, s = jnp.cos(ang), jnp.sin(ang)
        return jnp.tile(jnp.concatenate([c, c], 1), (1, reps)), jnp.tile(jnp.concatenate([-s, s], 1), (1, reps))

    ca, sa = tab(HEAD_DIM // 2, 1)
    cb, sb = tab(DIFF_DIM // 2, 2)
    return ca, sa, cb, sb


SB_UNDERFLOW = -104.0


def _sb_kernel(q_ref, k_ref, v_ref, u_ref, o_ref, carry_scr, acc_scr, *, tq, tk):
    qi = pl.program_id(1)
    carry_scr[...] = jnp.zeros_like(carry_scr)
    acc_scr[...] = jnp.zeros_like(acc_scr)
    q = q_ref[...]
    u = u_ref[...]
    row = lax.broadcasted_iota(jnp.int32, (tq, tk), 0)
    col = lax.broadcasted_iota(jnp.int32, (tq, tk), 1)

    def body(state):
        kj, _ = state
        off = pl.multiple_of(kj * tk, tk)
        z = _dot_nt(q, k_ref[pl.ds(off, tk), :])
        mask = (off + col) < (qi * tq + row)
        l1p = jnp.log(1.0 + jnp.exp(-jnp.abs(z)))
        ls_pos = jnp.minimum(z, 0.0) - l1p
        lk = jnp.where(mask, ls_pos - z, 0.0)
        hi = lk.astype(BF16)
        lo = (lk - hi.astype(F32)).astype(BF16)
        after = _dot(hi, u) + _dot(lo, u) + carry_scr[...]
        w = jnp.where(mask, jnp.exp(ls_pos + after), 0.0)
        acc_scr[...] += _dot(w.astype(BF16), v_ref[pl.ds(off, tk), :])
        carry = carry_scr[...] + jnp.sum(lk, axis=-1, keepdims=True)
        carry_scr[...] = carry
        go = (jnp.max(carry) > SB_UNDERFLOW).astype(jnp.int32)
        return kj - 1, go

    last_tile = (qi + 1) * (tq // tk) - 1
    lax.while_loop(lambda st: (st[0] >= 0) & (st[1] > 0), body, (last_tile, jnp.int32(1)))
    o_ref[...] = acc_scr[...]


def _suffix_matrix(n):
    j = np.arange(n)[:, None]
    s = np.arange(n)[None, :]
    return jnp.asarray((j > s).astype(np.float32), dtype=BF16)


def _sb_attention(q, k, v):
    h, t, d = q.shape
    tq = min(512, t)
    tk = min(256, t)
    return pl.pallas_call(
        functools.partial(_sb_kernel, tq=tq, tk=tk),
        grid=(h, t // tq),
        in_specs=[pl.BlockSpec((None, tq, d), lambda hh, i: (hh, i, 0)),
                  pl.BlockSpec((None, t, d), lambda hh, i: (hh, 0, 0)),
                  pl.BlockSpec((None, t, d), lambda hh, i: (hh, 0, 0)),
                  pl.BlockSpec((tk, tk), lambda hh, i: (0, 0))],
        out_specs=pl.BlockSpec((None, tq, d), lambda hh, i: (hh, i, 0)),
        out_shape=jax.ShapeDtypeStruct((h, t, d), F32),
        scratch_shapes=[pltpu.VMEM((tq, 1), F32), pltpu.VMEM((tq, d), F32)],
        compiler_params=_cparams(("arbitrary", "arbitrary")),
        name="sb_attn",
    )(q, k, v, _suffix_matrix(tk))


SEL_BIAS = 16384.0


def _flash_kernel(q_ref, k_ref, v_ref, *rest, mode, rep, tq, tk):
    if mode == "select":
        bias_ref, et_ref, o_ref, m_scr, acc_scr, s_scr, s2_scr = rest
    else:
        o_ref, m_scr, acc_scr, s_scr, s2_scr = rest
    qi = pl.program_id(1)
    rows = rep * tq
    m_scr[...] = jnp.full_like(m_scr, NEG)
    acc_scr[...] = jnp.zeros_like(acc_scr)
    q = q_ref[...].reshape(rows, HEAD_DIM)
    if mode == "select":
        q = jnp.concatenate([q, jnp.concatenate([bias_ref[...]] * rep, axis=0)], axis=1)
    ones = jnp.ones((tk, HEAD_DIM), BF16)
    qpos = qi * tq + (lax.broadcasted_iota(jnp.int32, (rows, tk), 0) & (tq - 1))
    col = lax.broadcasted_iota(jnp.int32, (rows, tk), 1)

    newest = ((qi + 1) * tq - 1) // tk
    if mode == "window":
        oldest = jnp.maximum(qi * tq - WINDOW, 0) // tk
        n_tiles = newest - oldest + 1
        tile_of = lambda i: newest - i
    else:
        n_tiles = newest + 1
        tile_of = lambda i: i
    n_full = (qi * tq + 1) // tk

    def scores(i):
        off = pl.multiple_of(tile_of(jnp.minimum(i, n_tiles - 1)) * tk, tk)
        k = k_ref[pl.ds(off, tk), :]
        if mode == "select":
            k = jnp.concatenate([k, et_ref[pl.ds(off, tk), :]], axis=1)
        return _dot_nt(q, k)

    def masked_scores(s, off):
        dist = qpos - (off + col)
        mask = (dist >= 0) & (dist <= WINDOW) if mode == "window" else dist >= 0
        return jnp.where(mask, s, NEG)

    def visit(src, dst, i):
        j = tile_of(i)
        off = pl.multiple_of(j * tk, tk)
        if mode == "window":
            src[...] = masked_scores(src[...], off)
        else:
            @pl.when(j >= n_full)
            def _():
                src[...] = masked_scores(src[...], off)
        dst[...] = scores(i + 1)
        s = src[...]
        m_old = m_scr[...]
        m_new = jnp.maximum(m_old, jnp.max(s, axis=-1, keepdims=True))
        e = jnp.exp(s - m_new)
        v_ext = jnp.concatenate([v_ref[pl.ds(off, tk), :], ones], axis=1)
        acc_scr[...] = jnp.exp(m_old - m_new) * acc_scr[...] + _dot(e.astype(BF16), v_ext)
        m_scr[...] = m_new

    def body(p, carry):
        visit(s_scr, s2_scr, 2 * p)
        pl.when(2 * p + 1 < n_tiles)(lambda: visit(s2_scr, s_scr, 2 * p + 1))
        return carry

    s_scr[...] = scores(0)
    lax.fori_loop(0, (n_tiles + 1) // 2, body, 0)
    acc = acc_scr[...]
    out = acc[:, :HEAD_DIM] / jnp.maximum(acc[:, HEAD_DIM:], 1e-30)
    o_ref[...] = out.reshape(rep, tq, HEAD_DIM)


def _block_onehot(t):
    key = np.arange(t)[:, None]
    b = np.arange(LANES)[None, :]
    return jnp.asarray((key // SLC_LEN == b).astype(np.float32), dtype=BF16)


def _flash(q, k, v, mode, sel_bias=None):
    hq, t, d = q.shape
    hk = k.shape[0]
    rep = hq // hk
    tq = min(1024 // rep, t)
    tk = min(256 if mode == "window" else 1024, t)
    in_specs = [pl.BlockSpec((rep, tq, d), lambda g, i: (g, i, 0)),
                pl.BlockSpec((None, t, d), lambda g, i: (g, 0, 0)),
                pl.BlockSpec((None, t, d), lambda g, i: (g, 0, 0))]
    args = [q, k, v]
    if mode == "select":
        in_specs += [pl.BlockSpec((None, tq, LANES), lambda g, i: (g, i, 0)),
                     pl.BlockSpec((t, LANES), lambda g, i: (0, 0))]
        args += [sel_bias, _block_onehot(t)]
    return pl.pallas_call(
        functools.partial(_flash_kernel, mode=mode, rep=rep, tq=tq, tk=tk),
        grid=(hk, t // tq),
        in_specs=in_specs,
        out_specs=pl.BlockSpec((rep, tq, d), lambda g, i: (g, i, 0)),
        out_shape=jax.ShapeDtypeStruct((hq, t, d), F32),
        scratch_shapes=[pltpu.VMEM((rep * tq, 1), F32), pltpu.VMEM((rep * tq, 2 * d), F32),
                        pltpu.VMEM((rep * tq, tk), F32), pltpu.VMEM((rep * tq, tk), F32)],
        compiler_params=_cparams(("arbitrary", "arbitrary")),
        name="attn_" + mode,
    )(*args)


def _compress_kernel(x_ref, pe_ref, w1_ref, w2_ref, o_ref):
    x = x_ref[...]
    n = x.shape[0]
    a = _dot((x + pe_ref[0:1, :]).astype(BF16), w1_ref[0].astype(BF16))
    b = _dot((x + pe_ref[1:2, :]).astype(BF16), w1_ref[1].astype(BF16))
    h = a + pltpu.roll(b, n - 1, 0)
    o_ref[...] = _dot(_silu(h).astype(BF16), w2_ref[...].astype(BF16)).astype(o_ref.dtype)


def _compress(xc, pe, w1, w2):
    _, g, n, wdt = xc.shape
    return pl.pallas_call(
        _compress_kernel,
        grid=(2, g),
        in_specs=[pl.BlockSpec((None, None, n, wdt), lambda a, b: (a, b, 0, 0)),
                  pl.BlockSpec((None, 2, wdt), lambda a, b: (a, 0, 0)),
                  pl.BlockSpec((None, 2, wdt, HEAD_DIM), lambda a, b: (a, 0, 0, 0)),
                  pl.BlockSpec((None, HEAD_DIM, HEAD_DIM), lambda a, b: (a, 0, 0))],
        out_specs=pl.BlockSpec((None, None, n, HEAD_DIM), lambda a, b: (a, b, 0, 0)),
        out_shape=jax.ShapeDtypeStruct((2, g, n, HEAD_DIM), BF16),
        compiler_params=_cparams(("arbitrary", "arbitrary")),
        name="nsa_compress",
    )(xc, pe, w1, w2)


def _importance_matrix(nc_pad, nblk_pad):
    r = SLC_LEN // CMP_STRIDE
    c = CMP_LEN // CMP_STRIDE
    i = np.arange(nc_pad)[:, None]
    b = np.arange(nblk_pad)[None, :]
    lo = r * b - (c - 1)
    return jnp.asarray(((i >= lo) & (i <= lo + r + c - 2)).astype(np.float32), dtype=BF16)


def _select_blocks(imp, cur, blk):
    forced = (blk == 0) | (blk == cur) | (blk == cur - 1)
    score = jnp.where(forced, 1e30, jnp.where(blk <= cur, imp, NEG))
    sel = jnp.zeros_like(score)
    blkf = blk.astype(F32)
    for _ in range(SLC_TOPK):
        mx = jnp.max(score, axis=-1, keepdims=True)
        idx = jnp.min(jnp.where(score == mx, blkf, 1e9), axis=-1, keepdims=True)
        pick = blkf == idx
        sel = jnp.where(pick, 1.0, sel)
        score = jnp.where(pick, -3e38, score)
    return sel


def _cmp_kernel(q_ref, kc_ref, vc_ref, imat_ref, o_ref, sel_ref, *, tq, nc):
    i = pl.program_id(0)
    rows = HPG * tq
    ncp = kc_ref.shape[1]
    row = lax.broadcasted_iota(jnp.int32, (rows, ncp), 0)
    qpos = i * tq + (row & (tq - 1))
    n = lax.broadcasted_iota(jnp.int32, (rows, ncp), 1)
    mask = (n * CMP_STRIDE + CMP_LEN - 1 <= qpos) & (n < nc)
    imps = []
    for g in range(G_NSA):
        q = q_ref[HPG * g:HPG * (g + 1)].reshape(rows, HEAD_DIM)
        s = jnp.where(mask, _dot_nt(q, kc_ref[g]), NEG)
        m = jnp.max(s, axis=-1, keepdims=True)
        e = jnp.where(mask, jnp.exp(s - m), 0.0)
        p = e / jnp.maximum(jnp.sum(e, axis=-1, keepdims=True), 1e-30)
        o_ref[HPG * g:HPG * (g + 1)] = _dot(p.astype(BF16), vc_ref[g]).reshape(HPG, tq, HEAD_DIM)
        psum = p[0:tq]
        for h in range(1, HPG):
            psum = psum + p[h * tq:(h + 1) * tq]
        imps.append(_dot_f32ish(psum, imat_ref[...]))
    imp = jnp.concatenate(imps, axis=0)
    blk = lax.broadcasted_iota(jnp.int32, imp.shape, 1)
    cur = (i * tq + (lax.broadcasted_iota(jnp.int32, imp.shape, 0) & (tq - 1))) // SLC_LEN
    bias = ((_select_blocks(imp, cur, blk) - 1.0) * SEL_BIAS).astype(BF16)
    for g in range(G_NSA):
        sel_ref[g] = bias[tq * g:tq * (g + 1)]


def _cmp_select(q, kcmp, vcmp, nc):
    _, t, d = q.shape
    g, ncp, _ = kcmp.shape
    tq = min(512, t)
    imat = _importance_matrix(ncp, LANES)
    return pl.pallas_call(
        functools.partial(_cmp_kernel, tq=tq, nc=nc),
        grid=(t // tq,),
        in_specs=[pl.BlockSpec((H_NSA, tq, d), lambda i: (0, i, 0)),
                  pl.BlockSpec((g, ncp, d), lambda i: (0, 0, 0)),
                  pl.BlockSpec((g, ncp, d), lambda i: (0, 0, 0)),
                  pl.BlockSpec((ncp, LANES), lambda i: (0, 0))],
        out_specs=[pl.BlockSpec((H_NSA, tq, d), lambda i: (0, i, 0)),
                   pl.BlockSpec((g, tq, LANES), lambda i: (0, i, 0))],
        out_shape=[jax.ShapeDtypeStruct((H_NSA, t, d), F32), jax.ShapeDtypeStruct((g, t, LANES), BF16)],
        compiler_params=_cparams(("arbitrary",)),
        name="nsa_cmp_select",
    )(q, kcmp, vcmp, imat)


def _merge_kernel(osb_ref, odf_ref, ocm_ref, osl_ref, owi_ref, gate_ref, lam_ref, dn_ref, x_ref, ga_ref, nrm_ref,
                  w_ref, o_ref, cat_scr, *, lam_init):
    lv = lam_ref[...]
    lam = (jnp.exp(jnp.sum(lv[0:1] * lv[1:2], axis=-1, keepdims=True))
           - jnp.exp(jnp.sum(lv[2:3] * lv[3:4], axis=-1, keepdims=True)) + lam_init)
    for h in range(H_SB):
        cat_scr[:, h * HEAD_DIM:(h + 1) * HEAD_DIM] = osb_ref[h].astype(BF16)
    dn = dn_ref[...]
    for h in range(H_DIFF):
        ob = odf_ref[2 * h] - lam * odf_ref[2 * h + 1]
        ob = _rms_rows(ob, dn) * (1.0 - lam_init)
        c0 = (H_SB + h) * HEAD_DIM
        cat_scr[:, c0:c0 + HEAD_DIM] = ob.astype(BF16)
    gates = gate_ref[...]
    for h in range(H_NSA):
        oc = (gates[:, 3 * h:3 * h + 1] * ocm_ref[h] + gates[:, 3 * h + 1:3 * h + 2] * osl_ref[h]
              + gates[:, 3 * h + 2:3 * h + 3] * owi_ref[h])
        c0 = (H_SB + H_DIFF + h) * HEAD_DIM
        cat_scr[:, c0:c0 + HEAD_DIM] = oc.astype(BF16)
    y = _dot(cat_scr[...], w_ref[...])
    o_ref[...] = x_ref[...] + ga_ref[...] * _rms_rows(y, nrm_ref[...])


def _merge(osb, odf, ocm, osl, owi, gates, lam_vecs, diff_norm, x, ga, nrm, w, layer):
    t, d = x.shape
    tm = min(256, t)
    lam_init = 0.8 - 0.6 * math.exp(-0.3 * layer)
    hspec = lambda n: pl.BlockSpec((n, tm, HEAD_DIM), lambda i: (0, i, 0))
    return pl.pallas_call(
        functools.partial(_merge_kernel, lam_init=lam_init),
        grid=(t // tm,),
        in_specs=[hspec(H_SB), hspec(2 * H_DIFF), hspec(H_NSA), hspec(H_NSA), hspec(H_NSA),
                  pl.BlockSpec((tm, LANES), lambda i: (i, 0)),
                  pl.BlockSpec(lam_vecs.shape, lambda i: (0, 0)),
                  pl.BlockSpec((1, HEAD_DIM), lambda i: (0, 0)),
                  pl.BlockSpec((tm, d), lambda i: (i, 0)),
                  _row_spec(ga, tm, 1),
                  pl.BlockSpec((1, d), lambda i: (0, 0)),
                  pl.BlockSpec(w.shape, lambda i: (0, 0))],
        out_specs=pl.BlockSpec((tm, d), lambda i: (i, 0)),
        out_shape=jax.ShapeDtypeStruct((t, d), F32),
        scratch_shapes=[pltpu.VMEM((tm, N_HEADS * HEAD_DIM), BF16)],
        compiler_params=_cparams(("arbitrary",)),
        name="merge_out_proj",
    )(osb, odf, ocm, osl, owi, gates, lam_vecs, diff_norm, x, ga, nrm, w)


def _ffn_kernel(x_ref, g2_ref, sc_ref, sh_ref, wg_ref, wu_ref, wd_ref, gf_ref, g3_ref, o_ref, h_scr, acc_scr):
    f = pl.program_id(1)

    @pl.when(f == 0)
    def _():
        y = _rms_rows(x_ref[...], g2_ref[...])
        h_scr[...] = (y * (1.0 + sc_ref[...]) + sh_ref[...]).astype(BF16)
        acc_scr[...] = jnp.zeros_like(acc_scr)

    h = h_scr[...]
    a = _dot(h, wg_ref[...])
    b = _dot(h, wu_ref[...])
    acc_scr[...] += _dot((_silu(a) * b).astype(BF16), wd_ref[...])

    @pl.when(f == pl.num_programs(1) - 1)
    def _():
        o_ref[...] = x_ref[...] + gf_ref[...] * _rms_rows(acc_scr[...], g3_ref[...])


def _ffn(x, g2, sc, sh, wgu, wd, gf, g3):
    t, d = x.shape
    dff = wd.shape[0]
    tm = min(512, t)
    tf = 512 if dff % 512 == 0 else dff
    nf = dff // tf
    return pl.pallas_call(
        _ffn_kernel,
        grid=(t // tm, nf),
        in_specs=[pl.BlockSpec((tm, d), lambda i, f: (i, 0)),
                  pl.BlockSpec((1, d), lambda i, f: (0, 0)),
                  _row_spec(sc, tm, 2), _row_spec(sh, tm, 2),
                  pl.BlockSpec((d, tf), lambda i, f: (0, f)),
                  pl.BlockSpec((d, tf), lambda i, f: (0, f + nf)),
                  pl.BlockSpec((tf, d), lambda i, f: (f, 0)),
                  _row_spec(gf, tm, 2),
                  pl.BlockSpec((1, d), lambda i, f: (0, 0))],
        out_specs=pl.BlockSpec((tm, d), lambda i, f: (i, 0)),
        out_shape=jax.ShapeDtypeStruct((t, d), F32),
        scratch_shapes=[pltpu.VMEM((tm, d), BF16), pltpu.VMEM((tm, d), F32)],
        compiler_params=_cparams(("arbitrary", "arbitrary")),
        name="ffn_swiglu",
    )(x, g2, sc, sh, wgu, wgu, wd, gf, g3)


DEC_ROWS = 32
BLK_LANES = 256


def _dec_kernel(pt_ref, *refs, pg, nps, past, ns):
    page_refs = refs[:pg]
    (new_ref, qsb_ref, qdf_ref, qsl_ref, u_ref, w1_ref, bsel_ref,
     osb_ref, odf_ref, sacc_ref, sm_ref, sl_ref, cab_ref,
     carry_scr, asb_scr, mdf_scr, adf_scr) = refs[pg:]
    j = pl.program_id(1)
    chunks_per_page = PAGE_SIZE // CMP_STRIDE

    @pl.when(j == 0)
    def _():
        carry_scr[...] = jnp.zeros_like(carry_scr)
        asb_scr[...] = jnp.zeros_like(asb_scr)
        mdf_scr[...] = jnp.full_like(mdf_scr, NEG)
        adf_scr[...] = jnp.zeros_like(adf_scr)

    def step(prefs, page0, masked):
        n = len(prefs)
        nk = n * PAGE_SIZE
        lane = lax.broadcasted_iota(jnp.int32, (DEC_ROWS, nk), 1)
        row = lax.broadcasted_iota(jnp.int32, (DEC_ROWS, nk), 0)
        kpos = page0 * PAGE_SIZE + lane
        qpos = past + (row & (ns - 1))

        def tok_rows(r):
            return jnp.concatenate([p[pl.ds(r, PAGE_SIZE, stride=ROWS_PER_TOK), :] for p in prefs], axis=0).astype(BF16)

        def stick_breaking():
            z = jnp.concatenate([_dot_nt(qsb_ref[8 * h:8 * h + 8, :], tok_rows(2 * h)) for h in range(H_SB)], axis=0)
            l1p = jnp.log(1.0 + jnp.exp(-jnp.abs(z)))
            ls_pos = jnp.minimum(z, 0.0) - l1p
            lk = ls_pos - z
            if masked:
                valid = kpos < qpos
                lk = jnp.where(valid, lk, 0.0)
            hi = lk.astype(BF16)
            lo = (lk - hi.astype(F32)).astype(BF16)
            pieces = ([hi[:, PAGE_SIZE * i:PAGE_SIZE * (i + 1)] for i in range(n)]
                      + [lo[:, PAGE_SIZE * i:PAGE_SIZE * (i + 1)] for i in range(n)])
            local = _dot(jnp.concatenate(pieces, axis=0), u_ref[...])
            run = carry_scr[...]
            after = [None] * n
            for i in reversed(range(n)):
                after[i] = (local[DEC_ROWS * i:DEC_ROWS * (i + 1)] + local[DEC_ROWS * (n + i):DEC_ROWS * (n + i + 1)]
                            + run)
                run = run + jnp.sum(lk[:, PAGE_SIZE * i:PAGE_SIZE * (i + 1)], axis=-1, keepdims=True)
            w = jnp.exp(ls_pos + jnp.concatenate(after, axis=1))
            if masked:
                w = jnp.where(valid, w, 0.0)
            wb = w.astype(BF16)
            asb_scr[...] += jnp.concatenate(
                [_dot(wb[8 * h:8 * h + 8, :], tok_rows(2 * h + 1)) for h in range(H_SB)], axis=0)
            carry_scr[...] = run

        if masked:
            stick_breaking()
        else:
            pl.when(jnp.max(carry_scr[...]) > SB_UNDERFLOW)(stick_breaking)

        r0 = 2 * H_SB
        s = jnp.concatenate([_dot_nt(qdf_ref[8 * h:8 * h + 8, :], tok_rows(r0 + 2 * h)) for h in range(H_DIFF)], axis=0)
        if masked:
            s = jnp.where(kpos <= qpos, s, NEG)
        m_old = mdf_scr[...]
        m_new = jnp.maximum(m_old, jnp.max(s, axis=-1, keepdims=True))
        eb = jnp.exp(s - m_new).astype(BF16)
        ones = jnp.ones((nk, HEAD_DIM), BF16)
        adf_scr[...] = jnp.exp(m_old - m_new) * adf_scr[...] + jnp.concatenate(
            [_dot(eb[8 * h:8 * h + 8, :], jnp.concatenate([tok_rows(r0 + 2 * h + 1), ones], axis=1))
             for h in range(H_DIFF)], axis=0)
        mdf_scr[...] = m_new

        r0 = 2 * (H_SB + H_DIFF + G_NSA)
        s = jnp.concatenate([_dot_nt(qsl_ref[16 * g:16 * g + 16, :], tok_rows(r0 + 2 * g)) for g in range(G_NSA)], axis=0)
        if masked:
            valid = kpos <= qpos
            s = jnp.where(valid, s, NEG)
        blk = lane // SLC_LEN
        col = lax.broadcasted_iota(jnp.int32, (DEC_ROWS, LANES), 1)
        mref = jnp.zeros((DEC_ROWS, nk), F32)
        mcols = jnp.full((DEC_ROWS, LANES), NEG, F32)
        for b in range(2 * n):
            mb = jnp.max(jnp.where(blk == b, s, NEG), axis=-1, keepdims=True)
            mref = jnp.where(blk == b, mb, mref)
            mcols = jnp.where(col == b, mb, mcols)
        e = jnp.exp(s - mref)
        if masked:
            e = jnp.where(valid, e, 0.0)
        eb = e.astype(BF16)
        sm_ref[...] = mcols
        sl_ref[...] = _dot(eb, bsel_ref[0:nk, :])
        blk16 = lax.broadcasted_iota(jnp.int32, (16, nk), 1) // SLC_LEN
        for g in range(G_NSA):
            eg = eb[16 * g:16 * g + 16, :]
            per_blk = jnp.concatenate([jnp.where(blk16 == b, eg, jnp.zeros_like(eg)) for b in range(2 * n)], axis=0)
            res = _dot(per_blk, tok_rows(r0 + 2 * g + 1))
            for b in range(2 * n):
                sacc_ref[b, 16 * g:16 * g + 16, :] = res[16 * b:16 * b + 16]
        for b in range(2 * n, 2 * pg):
            sacc_ref[b] = jnp.zeros((DEC_ROWS, HEAD_DIM), F32)

        r0 = 2 * (H_SB + H_DIFF)
        lhs = jnp.concatenate(
            [jnp.concatenate([p[pl.ds((CMP_STRIDE * m + l) * ROWS_PER_TOK + r0, 8), :] for l in range(CMP_STRIDE)], axis=1)
             for p in prefs for m in range(chunks_per_page)], axis=0).astype(BF16)
        res = _dot(lhs, w1_ref[...])
        crow = lax.broadcasted_iota(jnp.int32, (n * 8 * chunks_per_page, 2 * HEAD_DIM), 0)
        keep = jnp.where((crow & 1) == 0, res[:, :2 * HEAD_DIM], res[:, 2 * HEAD_DIM:])
        live = n * 8 * chunks_per_page
        for half in range(2):
            cab_ref[half, 0:live, :] = keep[:, half * HEAD_DIM:(half + 1) * HEAD_DIM]
            if n < pg:
                cab_ref[half, live:, :] = jnp.zeros(((pg - n) * 8 * chunks_per_page, HEAD_DIM), F32)

    @pl.when(j == 0)
    def _():
        step([new_ref], nps * pg, True)

    @pl.when(j > 0)
    def _():
        step(list(page_refs), (nps - j) * pg, False)

    @pl.when(j == nps)
    def _():
        osb_ref[...] = asb_scr[...]
        adf = adf_scr[...]
        odf_ref[...] = adf[:, :HEAD_DIM] / jnp.maximum(adf[:, HEAD_DIM:], 1e-30)


def _dec_main(layer, page_table, cache_rows, new_page, qsb, qdf, qsl, w1all, past, ns):
    nb, npages = page_table.shape
    pg = next(c for c in (8, 4, 2, 1) if npages % c == 0 and npages >= 2 * c)
    nps = npages // pg
    rows_per_page = PAGE_SIZE * ROWS_PER_TOK

    def page_spec(i):
        def imap(b, j, pt):
            return (layer, pt[b, (nps - jnp.maximum(j, 1)) * pg + i], 0, 0)
        return pl.BlockSpec((None, None, rows_per_page, HEAD_DIM), imap)

    qspec = pl.BlockSpec((None, DEC_ROWS, HEAD_DIM), lambda b, j, pt: (b, 0, 0))
    nblk_pad = (nps + 1) * 2 * pg
    cab_rows = pg * 8 * (PAGE_SIZE // CMP_STRIDE)
    key = np.arange(pg * PAGE_SIZE)[:, None]
    bsel = jnp.asarray((key // SLC_LEN == np.arange(LANES)[None, :]).astype(np.float32), dtype=BF16)
    stat_spec = pl.BlockSpec((None, None, DEC_ROWS, LANES), lambda b, j, pt: (b, nps - j, 0, 0))
    grid_spec = pltpu.PrefetchScalarGridSpec(
        num_scalar_prefetch=1,
        grid=(nb, nps + 1),
        in_specs=[page_spec(i) for i in range(pg)] + [
            pl.BlockSpec((None, rows_per_page, HEAD_DIM), lambda b, j, pt: (b, 0, 0)),
            qspec, qspec, qspec,
            pl.BlockSpec((PAGE_SIZE, PAGE_SIZE), lambda b, j, pt: (0, 0)),
            pl.BlockSpec(w1all.shape, lambda b, j, pt: (0, 0)),
            pl.BlockSpec(bsel.shape, lambda b, j, pt: (0, 0))],
        out_specs=[
            qspec, qspec,
            pl.BlockSpec((None, 2 * pg, DEC_ROWS, HEAD_DIM), lambda b, j, pt: (b, nps - j, 0, 0)),
            stat_spec, stat_spec,
            pl.BlockSpec((None, 2, cab_rows, HEAD_DIM), lambda b, j, pt: (b, 0, nps - j, 0))],
        scratch_shapes=[pltpu.VMEM((DEC_ROWS, 1), F32), pltpu.VMEM((DEC_ROWS, HEAD_DIM), F32),
                        pltpu.VMEM((DEC_ROWS, 1), F32), pltpu.VMEM((DEC_ROWS, 2 * HEAD_DIM), F32)])
    out_shape = [jax.ShapeDtypeStruct((nb, DEC_ROWS, HEAD_DIM), F32),
                 jax.ShapeDtypeStruct((nb, DEC_ROWS, HEAD_DIM), F32),
                 jax.ShapeDtypeStruct((nb, nblk_pad, DEC_ROWS, HEAD_DIM), F32),
                 jax.ShapeDtypeStruct((nb, nps + 1, DEC_ROWS, LANES), F32),
                 jax.ShapeDtypeStruct((nb, nps + 1, DEC_ROWS, LANES), F32),
                 jax.ShapeDtypeStruct((nb, 2, (nps + 1) * cab_rows, HEAD_DIM), F32)]
    o_sb, o_df, sacc, sm, sl, cab = pl.pallas_call(
        functools.partial(_dec_kernel, pg=pg, nps=nps, past=past, ns=ns),
        grid_spec=grid_spec,
        out_shape=out_shape,
        compiler_params=_cparams(("arbitrary", "arbitrary")),
        name="decode_paged",
    )(page_table, *([cache_rows] * pg), new_page, qsb, qdf, qsl, _suffix_matrix(PAGE_SIZE), w1all, bsel)

    def by_block(stat, fill):
        stat = stat[..., :2 * pg].transpose(0, 2, 1, 3).reshape(nb, DEC_ROWS, nblk_pad)
        return jnp.pad(stat, ((0, 0), (0, 0), (0, BLK_LANES - nblk_pad)), constant_values=fill)

    return o_sb, o_df, sacc, by_block(sm, NEG), by_block(sl, 0.0), cab


def _cmp_bias_kernel(pe_ref, w1_ref, o_ref):
    for kv in range(2):
        for c in range(2):
            o_ref[kv, c] = _dot(jnp.broadcast_to(pe_ref[kv, c:c + 1, :], (8, pe_ref.shape[-1])).astype(BF16),
                                w1_ref[kv, c].astype(BF16))


def _cmp_bias(pe, w1):
    return pl.pallas_call(
        _cmp_bias_kernel,
        out_shape=jax.ShapeDtypeStruct((2, 2, 8, HEAD_DIM), F32),
        compiler_params=pltpu.CompilerParams(vmem_limit_bytes=VMEM_LIMIT),
        name="cmp_bias",
    )(pe, w1)


def _dec_final_kernel(cab_ref, bias_ref, w2_ref, q_ref, imat_ref, jmat_ref, sacc_ref, sm_ref, sl_ref,
                      swin_ref, nwin_ref, *rest, nseq, **static):
    outs = rest[-4:]
    for s in range(nseq):
        _dec_final_one(cab_ref.at[s], bias_ref, w2_ref, q_ref.at[s], imat_ref, jmat_ref, sacc_ref.at[s], sm_ref.at[s],
                       sl_ref.at[s], swin_ref.at[s], nwin_ref.at[s], *[o.at[s] for o in outs], **static)


def _dec_final_one(cab_ref, bias_ref, w2_ref, q_ref, imat_ref, jmat_ref, sacc_ref, sm_ref, sl_ref,
                   swin_ref, nwin_ref, ocm_ref, osl_ref, owi_ref, wout_ref, *, nc, past, ns, nblk_pad, wb):
    win_rows, new_rows = swin_ref.shape[0], ns * 2 * G_NSA
    wout_ref[0:win_rows - new_rows, :] = swin_ref[new_rows:win_rows, :]
    wout_ref[win_rows - new_rows:win_rows, :] = nwin_ref[0:new_rows, :]
    ncp = cab_ref.shape[1] // 8
    rows16 = lax.broadcasted_iota(jnp.int32, (16, 1), 0)
    qpos16 = past + (rows16 & (ns - 1))
    imps = []
    for g in range(G_NSA):
        q = q_ref[16 * g:16 * g + 16, :]
        cmp_kv = []
        for kv in range(2):
            a = cab_ref[0, pl.ds(2 * g + kv, ncp, stride=8), :]
            b = cab_ref[1, pl.ds(2 * g + kv, ncp, stride=8), :]
            h = a + pltpu.roll(b, ncp - 1, 0) + bias_ref[kv, 0, 0:1, :] + bias_ref[kv, 1, 0:1, :]
            cmp_kv.append(_dot(_silu(h).astype(BF16), w2_ref[kv].astype(BF16)).astype(BF16))
        kc, vc = cmp_kv
        s = _dot_nt(q, kc)
        n = lax.broadcasted_iota(jnp.int32, s.shape, 1)
        mask = (n * CMP_STRIDE + CMP_LEN - 1 <= qpos16) & (n < nc)
        s = jnp.where(mask, s, NEG)
        m = jnp.max(s, axis=-1, keepdims=True)
        e = jnp.where(mask, jnp.exp(s - m), 0.0)
        p = e / jnp.maximum(jnp.sum(e, axis=-1, keepdims=True), 1e-30)
        ocm_ref[16 * g:16 * g + 16, :] = _dot(p.astype(BF16), vc)
        hi, mid, lo = _split3(p)
        jm = jmat_ref[...]
        psum = _dot(jm, hi) + _dot(jm, mid) + _dot(jm, lo)
        imps.append(_dot_f32ish(psum, imat_ref[...]))
    imp = jnp.concatenate(imps, axis=0)
    blk = lax.broadcasted_iota(jnp.int32, imp.shape, 1)
    cur = (past + (lax.broadcasted_iota(jnp.int32, imp.shape, 0) & (ns - 1))) // SLC_LEN
    sel_all = _select_blocks(imp, cur, blk)
    for g in range(G_NSA):
        q = q_ref[16 * g:16 * g + 16, :]
        sel = sel_all[16 * g:16 * g + 16]
        mb = sm_ref[16 * g:16 * g + 16, :]
        lb = sl_ref[16 * g:16 * g + 16, :]
        picked = sel > 0.5
        mtot = jnp.max(jnp.where(picked, mb, NEG), axis=-1, keepdims=True)
        wgt = jnp.where(picked, jnp.exp(mb - mtot), 0.0)
        den = jnp.sum(wgt * lb, axis=-1, keepdims=True)
        num = jnp.zeros((16, HEAD_DIM), F32)
        for b in range(nblk_pad):
            r0 = b * DEC_ROWS + 16 * g
            num = num + wgt[:, b:b + 1] * sacc_ref[r0:r0 + 16, :]
        osl_ref[16 * g:16 * g + 16, :] = num / jnp.maximum(den, 1e-30)
        kw = swin_ref[pl.ds(g, wb, stride=2 * G_NSA), :].astype(BF16)
        vw = swin_ref[pl.ds(G_NSA + g, wb, stride=2 * G_NSA), :].astype(BF16)
        kn = nwin_ref[pl.ds(g, 8, stride=2 * G_NSA), :].astype(BF16)
        vn = nwin_ref[pl.ds(G_NSA + g, 8, stride=2 * G_NSA), :].astype(BF16)
        s1 = _dot_nt(q, kw)
        s2 = _dot_nt(q, kn)
        d1 = qpos16 - (past - wb + lax.broadcasted_iota(jnp.int32, s1.shape, 1))
        k1 = (d1 >= 0) & (d1 <= WINDOW)
        d2 = qpos16 - (past + lax.broadcasted_iota(jnp.int32, s2.shape, 1))
        k2 = (d2 >= 0) & (d2 <= WINDOW)
        s1 = jnp.where(k1, s1, NEG)
        s2 = jnp.where(k2, s2, NEG)
        m = jnp.maximum(jnp.max(s1, axis=-1, keepdims=True), jnp.max(s2, axis=-1, keepdims=True))
        e1 = jnp.where(k1, jnp.exp(s1 - m), 0.0)
        e2 = jnp.where(k2, jnp.exp(s2 - m), 0.0)
        den = jnp.sum(e1, axis=-1, keepdims=True) + jnp.sum(e2, axis=-1, keepdims=True)
        owi_ref[16 * g:16 * g + 16, :] = (_dot(e1.astype(BF16), vw) + _dot(e2.astype(BF16), vn)) / jnp.maximum(den, 1e-30)


def _dec_final(layer, cab, bias, w2, qsl, sacc, sm, sl, state_rows, new_win, nc, past, ns, win_stack):
    nb = cab.shape[0]
    depth = state_rows.shape[0]
    ncp = cab.shape[2] // 8
    nblk_pad = sacc.shape[1] // DEC_ROWS
    wb = state_rows.shape[2] // (2 * G_NSA)
    imat = _importance_matrix(ncp, BLK_LANES)
    t_of = np.arange(16) % ns
    jmat = jnp.asarray((t_of[:, None] == t_of[None, :]).astype(np.float32), dtype=BF16)
    nseq = 2 if nb % 2 == 0 else 1
    ospec = pl.BlockSpec((nseq, DEC_ROWS, HEAD_DIM), lambda b: (b, 0, 0))
    wspec = pl.BlockSpec((None, nseq, state_rows.shape[2], HEAD_DIM), lambda b: (layer, b, 0, 0))
    in_specs = [pl.BlockSpec((nseq, 2, ncp * 8, HEAD_DIM), lambda b: (b, 0, 0, 0)),
                pl.BlockSpec(bias.shape, lambda b: (0, 0, 0, 0)),
                pl.BlockSpec(w2.shape, lambda b: (0, 0, 0)),
                ospec,
                pl.BlockSpec(imat.shape, lambda b: (0, 0)),
                pl.BlockSpec(jmat.shape, lambda b: (0, 0)),
                pl.BlockSpec((nseq, nblk_pad * DEC_ROWS, HEAD_DIM), lambda b: (b, 0, 0)),
                pl.BlockSpec((nseq, DEC_ROWS, BLK_LANES), lambda b: (b, 0, 0)),
                pl.BlockSpec((nseq, DEC_ROWS, BLK_LANES), lambda b: (b, 0, 0)),
                wspec,
                pl.BlockSpec((nseq, new_win.shape[1], HEAD_DIM), lambda b: (b, 0, 0))]
    args = [cab, bias, w2, qsl, imat, jmat, sacc, sm, sl, state_rows, new_win]
    aliases = {}
    if win_stack is not None:
        in_specs.append(pl.BlockSpec(memory_space=pl.ANY))
        args.append(win_stack)
        aliases = {len(args) - 1: 3}
    return pl.pallas_call(
        functools.partial(_dec_final_kernel, nseq=nseq, nc=nc, past=past, ns=ns, nblk_pad=nblk_pad, wb=wb),
        grid=(nb // nseq,),
        in_specs=in_specs,
        out_specs=[ospec, ospec, ospec, wspec],
        out_shape=[jax.ShapeDtypeStruct((nb, DEC_ROWS, HEAD_DIM), F32)] * 3 + [
            jax.ShapeDtypeStruct((depth,) + state_rows.shape[1:], F32)],
        input_output_aliases=aliases,
        compiler_params=_cparams(("arbitrary",)),
        name="decode_nsa_finish",
    )(*args)


def _sample_mixer(x, mod, lw, layer, tabs, page_table, cache_rows, state_rows, nb, ns, past, win_stack):
    sh_a, sc_a, g_a = mod[0], mod[1], mod[2]
    proj = _proj(x, lw["n0"], sc_a, sh_a, lw["w_in"])
    (qsb, _, _, qdf, _, _, qns, _, _, _, _, _, _, gates, kv_rows, win_rows) = _postproj(proj, tabs)

    def per_seq(a, heads, rows_per_head):
        a = a.reshape(heads, nb, ns, HEAD_DIM).transpose(1, 0, 2, 3)
        a = jnp.pad(a, ((0, 0), (0, 0), (0, rows_per_head - ns), (0, 0)))
        return a.reshape(nb, heads * rows_per_head, HEAD_DIM)

    qsb_d = per_seq(qsb, H_SB, 8)
    qdf_d = per_seq(qdf, 2 * H_DIFF, ns)
    qsl_d = per_seq(qns, H_NSA, ns)
    kv_tok = kv_rows.reshape(nb, ns * ROWS_PER_TOK, HEAD_DIM)
    new_page = jnp.pad(kv_tok, ((0, 0), (0, (PAGE_SIZE - ns) * ROWS_PER_TOK), (0, 0)))
    w1 = lw["cmp_w1"]
    w1all = jnp.concatenate([w1[0, 0], w1[0, 1], w1[1, 0], w1[1, 1]], axis=-1).astype(BF16)
    o_sb, o_df, sacc, sm, sl, cab = _dec_main(layer, page_table, cache_rows, new_page, qsb_d, qdf_d, qsl_d, w1all,
                                              past, ns)
    bias = _cmp_bias(lw["cmp_pe"], w1)
    new_win = jnp.pad(win_rows.reshape(nb, ns * 2 * G_NSA, HEAD_DIM), ((0, 0), (0, (8 - ns) * 2 * G_NSA), (0, 0)))
    padded = -(-(past + ns) // SLC_LEN) * SLC_LEN
    nc = padded // CMP_STRIDE - CMP_LEN // CMP_STRIDE + 1
    o_cmp, o_slc, o_win, win_stack = _dec_final(layer, cab, bias, lw["cmp_w2"], qsl_d, sacc.reshape(nb, -1, HEAD_DIM),
                                                sm, sl, state_rows, new_win, nc, past, ns, win_stack)

    def head_major(a, heads, rows_per_head):
        a = a.reshape(nb, heads, rows_per_head, HEAD_DIM)[:, :, :ns]
        return a.transpose(1, 0, 2, 3).reshape(heads, nb * ns, HEAD_DIM)

    x1 = _merge(head_major(o_sb, H_SB, 8), head_major(o_df, 2 * H_DIFF, ns), head_major(o_cmp, H_NSA, ns),
                head_major(o_slc, H_NSA, ns), head_major(o_win, H_NSA, ns), gates, lw["diff_lambda"],
                lw["diff_norm"], x, g_a, lw["n1"], lw["w_out"], layer)
    return x1, kv_tok, win_stack


def _split_mod(mod):
    return [mod[:, i * (mod.shape[1] // 6):(i + 1) * (mod.shape[1] // 6)] for i in range(6)]


def _prompt_mixer(x, mod, lw, layer, tabs, depth, kv_stack):
    t = x.shape[0]
    sh_a, sc_a, g_a = mod[0], mod[1], mod[2]
    proj = _proj(x, lw["n0"], sc_a, sh_a, lw["w_in"])
    (qsb, ksb, vsb, qdf, kdf, vdf, qns, kc, vc, ks, vs, kw, vw, gates, kv_rows, win_rows) = _postproj(
        proj, tabs, layer, depth, kv_stack)
    o_sb = _sb_attention(qsb, ksb, vsb)
    o_df = _flash(qdf, kdf, vdf, "causal")
    nch = t // CMP_STRIDE
    xc = jnp.stack([kc, vc]).reshape(2, G_NSA, nch, CMP_STRIDE * HEAD_DIM)
    cmp_kv = _compress(xc, lw["cmp_pe"], lw["cmp_w1"], lw["cmp_w2"])
    o_cmp, sel = _cmp_select(qns, cmp_kv[0], cmp_kv[1], nch - 1)
    o_slc = _flash(qns, ks, vs, "select", sel)
    o_win = _flash(qns, kw, vw, "window")
    x1 = _merge(o_sb, o_df, o_cmp, o_slc, o_win, gates, lw["diff_lambda"], lw["diff_norm"], x, g_a, lw["n1"],
                lw["w_out"], layer)
    return x1, kv_rows, win_rows


def _layer_weights(layer, norms, w_in_p, w_out_b, diff_lambda, diff_norm, cmp_pe, cmp_w1, cmp_w2, wgu_b, wd_b):
    d = norms.shape[-1]
    return dict(
        n0=norms[layer, 0].reshape(1, d), n1=norms[layer, 1].reshape(1, d),
        n2=norms[layer, 2].reshape(1, d), n3=norms[layer, 3].reshape(1, d),
        w_in=w_in_p[layer], w_out=w_out_b[layer],
        diff_lambda=diff_lambda[layer], diff_norm=diff_norm[layer].reshape(1, HEAD_DIM),
        cmp_pe=cmp_pe[layer].reshape(2, 2, CMP_STRIDE * HEAD_DIM),
        cmp_w1=cmp_w1[layer].reshape(2, 2, CMP_STRIDE * HEAD_DIM, HEAD_DIM),
        cmp_w2=cmp_w2[layer], w_gate_up=wgu_b[layer], w_down=wd_b[layer])


def kernel(x_prompt, x_sample, cache_kv, state_win_kv, page_table, c_prompt, c_sample, ada_w, ada_b, norms, w_in,
           w_out, diff_lambda, diff_norm, cmp_pe, cmp_w1, cmp_w2, w_gate_up, w_down):
    depth = ada_w.shape[0]
    _, seq, d = x_prompt.shape
    nb, ns, _ = x_sample.shape
    past = page_table.shape[1] * PAGE_SIZE

    w_in_p = jnp.pad(w_in, ((0, 0), (0, 0), (0, N_IN_PAD - N_IN))).astype(BF16)
    w_out_b = w_out.astype(BF16)
    wgu_b = w_gate_up.astype(BF16)
    wd_b = w_down.astype(BF16)

    n_c = 1 + nb
    rows = -(-n_c // 8) * 8
    c_all = jnp.pad(jnp.concatenate([c_prompt, c_sample], axis=0), ((0, rows - n_c), (0, 0)))
    mod_all = _ada(c_all, ada_w, ada_b)

    tabs_p = _rope_tables(jnp.arange(seq))
    tabs_s = _rope_tables(past + jnp.arange(nb * ns) % ns)
    cache_rows = cache_kv.transpose(0, 1, 2, 4, 3, 5).reshape(depth, cache_kv.shape[1], PAGE_SIZE * ROWS_PER_TOK,
                                                               HEAD_DIM)
    wb_s = state_win_kv.shape[2]
    state_rows = state_win_kv.reshape(depth, nb, wb_s * 2 * G_NSA, HEAD_DIM)
    xp = x_prompt.reshape(seq, d)
    xs = x_sample.reshape(nb * ns, d)
    win_p, kv_s = [], []
    kv_stack = win_stack = None
    for layer in range(depth):
        lw = _layer_weights(layer, norms, w_in_p, w_out_b, diff_lambda, diff_norm, cmp_pe, cmp_w1, cmp_w2, wgu_b, wd_b)
        mod_p = _split_mod(mod_all[layer, 0:1])
        x1, kv_stack, win_rows = _prompt_mixer(xp, mod_p, lw, layer, tabs_p, depth, kv_stack)
        xp = _ffn(x1, lw["n2"], mod_p[4], mod_p[3], lw["w_gate_up"], lw["w_down"], mod_p[5], lw["n3"])
        wb = min(WINDOW, seq)
        win_p.append(win_rows[seq - wb:].reshape(1, wb, 2, G_NSA, HEAD_DIM))

        mod_s = _split_mod(jnp.repeat(mod_all[layer, 1:1 + nb], ns, axis=0))
        x1, kv_tok, win_stack = _sample_mixer(xs, mod_s, lw, layer, tabs_s, page_table, cache_rows, state_rows,
                                              nb, ns, past, win_stack)
        xs = _ffn(x1, lw["n2"], mod_s[4], mod_s[3], lw["w_gate_up"], lw["w_down"], mod_s[5], lw["n3"])
        kv_s.append(kv_tok.reshape(nb, ns, N_KV, 2, HEAD_DIM).transpose(0, 1, 3, 2, 4))
    kv_prompt = kv_stack.reshape(depth, 1, seq, N_KV, 2, HEAD_DIM).transpose(0, 1, 2, 4, 3, 5)
    win_sample = win_stack.reshape(depth, nb, wb_s, 2, G_NSA, HEAD_DIM)
    return (xp.reshape(1, seq, d), xs.reshape(nb, ns, d), kv_prompt, jnp.stack(kv_s), jnp.stack(win_p), win_sample)
```

```python
import functools
import math

import numpy as np
import jax
import jax.numpy as jnp
from jax import lax
from jax.experimental import pallas as pl
from jax.experimental.pallas import tpu as pltpu

F32 = jnp.float32
BF16 = jnp.bfloat16

HEAD_DIM = 128
H_SB = 4
H_DIFF = 4
H_NSA = 8
G_NSA = 2
HPG = H_NSA // G_NSA
N_KV = H_SB + H_DIFF + 2 * G_NSA
N_HEADS = H_SB + H_DIFF + H_NSA
DIFF_DIM = HEAD_DIM // 2
CMP_LEN = 32
CMP_STRIDE = 16
SLC_LEN = 64
SLC_TOPK = 16
WINDOW = 512
PAGE_SIZE = 128
ROPE_THETA = 10000.0
EPS = 1e-6
NEG = -1e30
LANES = 128
VMEM_LIMIT = 56 * 1024 * 1024

C_QA, C_KA, C_VA = 0, 512, 1024
C_QB, C_KB, C_VB = 1536, 2048, 2560
C_QC = 3072
C_KC, C_VC, C_KS, C_VS, C_KW, C_VW = 4096, 4352, 4608, 4864, 5120, 5376
C_GATE = 5632
N_IN = 5656
N_IN_PAD = 5760
ROWS_PER_TOK = 2 * N_KV


def _cparams(sem):
    return pltpu.CompilerParams(dimension_semantics=sem, vmem_limit_bytes=VMEM_LIMIT)


def _silu(x):
    return x * jax.nn.sigmoid(x)


def _rms_rows(x, g):
    return x * lax.rsqrt(jnp.mean(x * x, axis=-1, keepdims=True) + EPS) * g


def _dot(a, b):
    return jnp.dot(a, b, preferred_element_type=F32)


def _dot_nt(a, b):
    return lax.dot_general(a, b, (((1,), (1,)), ((), ())), preferred_element_type=F32)


def _split3(x):
    hi = x.astype(BF16)
    r1 = x - hi.astype(F32)
    mid = r1.astype(BF16)
    lo = (r1 - mid.astype(F32)).astype(BF16)
    return hi, mid, lo


def _dot_f32ish(x, m01):
    hi, mid, lo = _split3(x)
    return _dot(hi, m01) + _dot(mid, m01) + _dot(lo, m01)


def _ada_kernel(c_ref, w_ref, b_ref, o_ref):
    a = _silu(c_ref[...]).astype(BF16)
    o_ref[...] = _dot(a, w_ref[...].astype(BF16)) + b_ref[...]


def _ada(c_all, ada_w, ada_b):
    depth, d, n = ada_w.shape
    rows = c_all.shape[0]
    tn = 1024
    return pl.pallas_call(
        _ada_kernel,
        grid=(depth, n // tn),
        in_specs=[pl.BlockSpec((rows, d), lambda l, j: (0, 0)),
                  pl.BlockSpec((None, d, tn), lambda l, j: (l, 0, j)),
                  pl.BlockSpec((None, 1, tn), lambda l, j: (l, 0, j))],
        out_specs=pl.BlockSpec((None, rows, tn), lambda l, j: (l, 0, j)),
        out_shape=jax.ShapeDtypeStruct((depth, rows, n), F32),
        compiler_params=_cparams(("arbitrary", "arbitrary")),
        name="ada_mod",
    )(c_all, ada_w, ada_b.reshape(depth, 1, n))


def _row_spec(arr, tm, grid_rank):
    c = arr.shape[1]
    if arr.shape[0] == 1:
        if grid_rank == 1:
            return pl.BlockSpec((1, c), lambda i: (0, 0))
        return pl.BlockSpec((1, c), lambda i, j: (0, 0))
    if grid_rank == 1:
        return pl.BlockSpec((tm, c), lambda i: (i, 0))
    return pl.BlockSpec((tm, c), lambda i, j: (i, 0))


def _proj_kernel(x_ref, g_ref, sc_ref, sh_ref, w_ref, o_ref, h_scr):
    @pl.when(pl.program_id(1) == 0)
    def _():
        y = _rms_rows(x_ref[...], g_ref[...])
        h_scr[...] = (y * (1.0 + sc_ref[...]) + sh_ref[...]).astype(BF16)

    o_ref[...] = _dot(h_scr[...], w_ref[...])


def _proj(x, g, sc, sh, w, layer):
    t, d = x.shape
    n = w.shape[2]
    tm = min(512, t)
    tn = 1920
    return pl.pallas_call(
        _proj_kernel,
        grid=(t // tm, n // tn),
        in_specs=[pl.BlockSpec((tm, d), lambda i, j: (i, 0)),
                  pl.BlockSpec((1, d), lambda i, j: (0, 0)),
                  _row_spec(sc, tm, 2), _row_spec(sh, tm, 2),
                  pl.BlockSpec((None, d, tn), lambda i, j: (layer, 0, j))],
        out_specs=pl.BlockSpec((tm, tn), lambda i, j: (i, j)),
        out_shape=jax.ShapeDtypeStruct((t, n), F32),
        scratch_shapes=[pltpu.VMEM((tm, d), BF16)],
        compiler_params=_cparams(("arbitrary", "arbitrary")),
        name="in_proj",
    )(x, g, sc, sh, w)


def _postproj_kernel(p_ref, ca_ref, sa_ref, cb_ref, sb_ref, *rest):
    (qsb_ref, ksb_ref, vsb_ref, qdf_ref, kdf_ref, vdf_ref, qns_ref,
     kc_ref, vc_ref, ks_ref, vs_ref, kw_ref, vw_ref, gate_ref, kv_ref, win_ref) = rest[-16:]
    tm = p_ref.shape[0]
    ca, sa, cb, sb = ca_ref[...], sa_ref[...], cb_ref[...], sb_ref[...]
    lane = lax.broadcasted_iota(jnp.int32, ca.shape, 1)
    low_half = (lane % DIFF_DIM) < (DIFF_DIM // 2)
    first_chunk = lane < DIFF_DIM
    sm_scale = HEAD_DIM ** -0.5
    df_scale = DIFF_DIM ** -0.5

    def col(c0, h):
        return p_ref[:, c0 + h * HEAD_DIM:c0 + (h + 1) * HEAD_DIM]

    def rope_a(x):
        return x * ca + pltpu.roll(x, HEAD_DIM // 2, 1) * sa

    def rope_b(x):
        partner = jnp.where(low_half, pltpu.roll(x, HEAD_DIM - DIFF_DIM // 2, 1), pltpu.roll(x, DIFF_DIM // 2, 1))
        return x * cb + partner * sb

    def put_kv(is_v, head, val):
        kv_ref[pl.ds(2 * head + is_v, tm, stride=ROWS_PER_TOK), :] = val

    for h in range(H_SB):
        qsb_ref[h] = (col(C_QA, h) * sm_scale).astype(BF16)
        k, v = col(C_KA, h), col(C_VA, h)
        ksb_ref[h] = k.astype(BF16)
        vsb_ref[h] = v.astype(BF16)
        put_kv(0, h, k)
        put_kv(1, h, v)
    for h in range(H_DIFF):
        q = rope_b(col(C_QB, h)) * df_scale
        qdf_ref[2 * h] = jnp.where(first_chunk, q, 0.0).astype(BF16)
        qdf_ref[2 * h + 1] = jnp.where(first_chunk, 0.0, q).astype(BF16)
        k, v = rope_b(col(C_KB, h)), col(C_VB, h)
        kdf_ref[h] = k.astype(BF16)
        vdf_ref[h] = v.astype(BF16)
        put_kv(0, H_SB + h, k)
        put_kv(1, H_SB + h, v)
    for h in range(H_NSA):
        qns_ref[h] = (rope_a(col(C_QC, h)) * sm_scale).astype(BF16)
    for g in range(G_NSA):
        k, v = rope_a(col(C_KC, g)), col(C_VC, g)
        kc_ref[g] = k
        vc_ref[g] = v
        put_kv(0, H_SB + H_DIFF + g, k)
        put_kv(1, H_SB + H_DIFF + g, v)
        k, v = rope_a(col(C_KS, g)), col(C_VS, g)
        ks_ref[g] = k.astype(BF16)
        vs_ref[g] = v.astype(BF16)
        put_kv(0, H_SB + H_DIFF + G_NSA + g, k)
        put_kv(1, H_SB + H_DIFF + G_NSA + g, v)
        k, v = rope_a(col(C_KW, g)), col(C_VW, g)
        kw_ref[g] = k.astype(BF16)
        vw_ref[g] = v.astype(BF16)
        win_ref[:, g * HEAD_DIM:(g + 1) * HEAD_DIM] = k
        win_ref[:, (G_NSA + g) * HEAD_DIM:(G_NSA + g + 1) * HEAD_DIM] = v
    gate_ref[...] = jax.nn.sigmoid(p_ref[:, C_GATE:C_GATE + LANES])


def _postproj(proj, tabs, layer=0, depth=1, kv_stack=None):
    t = proj.shape[0]
    tm = min(256, t)
    hm = lambda n, dt: jax.ShapeDtypeStruct((n, t, HEAD_DIM), dt)
    hspec = lambda n: pl.BlockSpec((n, tm, HEAD_DIM), lambda i: (0, i, 0))
    tab_spec = pl.BlockSpec((tm, LANES), lambda i: (i, 0))
    out_shapes = [hm(H_SB, BF16), hm(H_SB, BF16), hm(H_SB, BF16),
                  hm(2 * H_DIFF, BF16), hm(H_DIFF, BF16), hm(H_DIFF, BF16), hm(H_NSA, BF16),
                  hm(G_NSA, F32), hm(G_NSA, F32), hm(G_NSA, BF16), hm(G_NSA, BF16),
                  hm(G_NSA, BF16), hm(G_NSA, BF16),
                  jax.ShapeDtypeStruct((t, LANES), F32),
                  jax.ShapeDtypeStruct((depth, t * ROWS_PER_TOK, HEAD_DIM), F32),
                  jax.ShapeDtypeStruct((t, 2 * G_NSA * HEAD_DIM), F32)]
    out_specs = [hspec(H_SB), hspec(H_SB), hspec(H_SB), hspec(2 * H_DIFF), hspec(H_DIFF), hspec(H_DIFF),
                 hspec(H_NSA), hspec(G_NSA), hspec(G_NSA), hspec(G_NSA), hspec(G_NSA), hspec(G_NSA), hspec(G_NSA),
                 pl.BlockSpec((tm, LANES), lambda i: (i, 0)),
                 pl.BlockSpec((None, tm * ROWS_PER_TOK, HEAD_DIM), lambda i: (layer, i, 0)),
                 pl.BlockSpec((tm, 2 * G_NSA * HEAD_DIM), lambda i: (i, 0))]
    in_specs = [pl.BlockSpec((tm, N_IN_PAD), lambda i: (i, 0)), tab_spec, tab_spec, tab_spec, tab_spec]
    args = [proj, *tabs]
    aliases = {}
    if kv_stack is not None:
        in_specs.append(pl.BlockSpec(memory_space=pl.ANY))
        args.append(kv_stack)
        aliases = {len(args) - 1: 14}
    return pl.pallas_call(
        _postproj_kernel,
        grid=(t // tm,),
        in_specs=in_specs,
        out_specs=out_specs,
        out_shape=out_shapes,
        input_output_aliases=aliases,
        compiler_params=_cparams(("arbitrary",)),
        name="rope_split",
    )(*args)


def _rope_tables(pos):
    posf = pos.astype(F32)[:, None]

    def tab(half, reps):
        inv = jnp.power(ROPE_THETA, -jnp.arange(half, dtype=F32) / half)
        ang = posf * inv[None, :]
        c, s = jnp.cos(ang), jnp.sin(ang)
        return jnp.tile(jnp.concatenate([c, c], 1), (1, reps)), jnp.tile(jnp.concatenate([-s, s], 1), (1, reps))

    ca, sa = tab(HEAD_DIM // 2, 1)
    cb, sb = tab(DIFF_DIM // 2, 2)
    return ca, sa, cb, sb


SB_UNDERFLOW = -104.0


def _sb_kernel(q_ref, k_ref, v_ref, u_ref, o_ref, carry_scr, acc_scr, *, tq, tk):
    qi = pl.program_id(1)
    carry_scr[...] = jnp.zeros_like(carry_scr)
    acc_scr[...] = jnp.zeros_like(acc_scr)
    q = q_ref[...]
    u = u_ref[...]
    row = lax.broadcasted_iota(jnp.int32, (tq, tk), 0)
    col = lax.broadcasted_iota(jnp.int32, (tq, tk), 1)

    def body(state):
        kj, _ = state
        off = pl.multiple_of(kj * tk, tk)
        z = _dot_nt(q, k_ref[pl.ds(off, tk), :])
        mask = (off + col) < (qi * tq + row)
        l1p = jnp.log(1.0 + jnp.exp(-jnp.abs(z)))
        ls_pos = jnp.minimum(z, 0.0) - l1p
        lk = jnp.where(mask, ls_pos - z, 0.0)
        hi = lk.astype(BF16)
        lo = (lk - hi.astype(F32)).astype(BF16)
        after = _dot(hi, u) + _dot(lo, u) + carry_scr[...]
        w = jnp.where(mask, jnp.exp(ls_pos + after), 0.0)
        acc_scr[...] += _dot(w.astype(BF16), v_ref[pl.ds(off, tk), :])
        carry = carry_scr[...] + jnp.sum(lk, axis=-1, keepdims=True)
        carry_scr[...] = carry
        go = (jnp.max(carry) > SB_UNDERFLOW).astype(jnp.int32)
        return kj - 1, go

    last_tile = (qi + 1) * (tq // tk) - 1
    lax.while_loop(lambda st: (st[0] >= 0) & (st[1] > 0), body, (last_tile, jnp.int32(1)))
    o_ref[...] = acc_scr[...]


def _suffix_matrix(n):
    j = np.arange(n)[:, None]
    s = np.arange(n)[None, :]
    return jnp.asarray((j > s).astype(np.float32), dtype=BF16)


def _sb_attention(q, k, v):
    h, t, d = q.shape
    tq = min(512, t)
    tk = min(256, t)
    return pl.pallas_call(
        functools.partial(_sb_kernel, tq=tq, tk=tk),
        grid=(h, t // tq),
        in_specs=[pl.BlockSpec((None, tq, d), lambda hh, i: (hh, i, 0)),
                  pl.BlockSpec((None, t, d), lambda hh, i: (hh, 0, 0)),
                  pl.BlockSpec((None, t, d), lambda hh, i: (hh, 0, 0)),
                  pl.BlockSpec((tk, tk), lambda hh, i: (0, 0))],
        out_specs=pl.BlockSpec((None, tq, d), lambda hh, i: (hh, i, 0)),
        out_shape=jax.ShapeDtypeStruct((h, t, d), F32),
        scratch_shapes=[pltpu.VMEM((tq, 1), F32), pltpu.VMEM((tq, d), F32)],
        compiler_params=_cparams(("arbitrary", "arbitrary")),
        name="sb_attn",
    )(q, k, v, _suffix_matrix(tk))


SEL_BIAS = 16384.0


def _flash_kernel(q_ref, k_ref, v_ref, *rest, mode, rep, tq, tk):
    if mode == "select":
        bias_ref, et_ref, o_ref, m_scr, acc_scr, s_scr, s2_scr = rest
    else:
        o_ref, m_scr, acc_scr, s_scr, s2_scr = rest
    qi = pl.program_id(1)
    rows = rep * tq
    m_scr[...] = jnp.full_like(m_scr, NEG)
    acc_scr[...] = jnp.zeros_like(acc_scr)
    q = q_ref[...].reshape(rows, HEAD_DIM)
    if mode == "select":
        q = jnp.concatenate([q, jnp.concatenate([bias_ref[...]] * rep, axis=0)], axis=1)
    ones = jnp.ones((tk, HEAD_DIM), BF16)
    qpos = qi * tq + (lax.broadcasted_iota(jnp.int32, (rows, tk), 0) & (tq - 1))
    col = lax.broadcasted_iota(jnp.int32, (rows, tk), 1)

    newest = ((qi + 1) * tq - 1) // tk
    if mode == "window":
        oldest = jnp.maximum(qi * tq - WINDOW, 0) // tk
        n_tiles = newest - oldest + 1
        tile_of = lambda i: newest - i
    else:
        n_tiles = newest + 1
        tile_of = lambda i: i
    n_full = (qi * tq + 1) // tk

    def scores(i):
        off = pl.multiple_of(tile_of(jnp.minimum(i, n_tiles - 1)) * tk, tk)
        k = k_ref[pl.ds(off, tk), :]
        if mode == "select":
            k = jnp.concatenate([k, et_ref[pl.ds(off, tk), :]], axis=1)
        return _dot_nt(q, k)

    def masked_scores(s, off):
        dist = qpos - (off + col)
        mask = (dist >= 0) & (dist <= WINDOW) if mode == "window" else dist >= 0
        return jnp.where(mask, s, NEG)

    def visit(src, dst, i):
        j = tile_of(i)
        off = pl.multiple_of(j * tk, tk)
        if mode == "window":
            src[...] = masked_scores(src[...], off)
        else:
            @pl.when(j >= n_full)
            def _():
                src[...] = masked_scores(src[...], off)
        dst[...] = scores(i + 1)
        s = src[...]
        m_old = m_scr[...]
        m_new = jnp.maximum(m_old, jnp.max(s, axis=-1, keepdims=True))
        e = jnp.exp(s - m_new)
        v_ext = jnp.concatenate([v_ref[pl.ds(off, tk), :], ones], axis=1)
        acc_scr[...] = jnp.exp(m_old - m_new) * acc_scr[...] + _dot(e.astype(BF16), v_ext)
        m_scr[...] = m_new

    def body(p, carry):
        visit(s_scr, s2_scr, 2 * p)
        pl.when(2 * p + 1 < n_tiles)(lambda: visit(s2_scr, s_scr, 2 * p + 1))
        return carry

    s_scr[...] = scores(0)
    lax.fori_loop(0, (n_tiles + 1) // 2, body, 0)
    acc = acc_scr[...]
    out = acc[:, :HEAD_DIM] / jnp.maximum(acc[:, HEAD_DIM:], 1e-30)
    o_ref[...] = out.reshape(rep, tq, HEAD_DIM)


def _block_onehot(t):
    key = np.arange(t)[:, None]
    b = np.arange(LANES)[None, :]
    return jnp.asarray((key // SLC_LEN == b).astype(np.float32), dtype=BF16)


def _flash(q, k, v, mode, sel_bias=None):
    hq, t, d = q.shape
    hk = k.shape[0]
    rep = hq // hk
    tq = min(1024 // rep, t)
    tk = min(256 if mode == "window" else 1024, t)
    in_specs = [pl.BlockSpec((rep, tq, d), lambda g, i: (g, i, 0)),
                pl.BlockSpec((None, t, d), lambda g, i: (g, 0, 0)),
                pl.BlockSpec((None, t, d), lambda g, i: (g, 0, 0))]
    args = [q, k, v]
    if mode == "select":
        in_specs += [pl.BlockSpec((None, tq, LANES), lambda g, i: (g, i, 0)),
                     pl.BlockSpec((t, LANES), lambda g, i: (0, 0))]
        args += [sel_bias, _block_onehot(t)]
    return pl.pallas_call(
        functools.partial(_flash_kernel, mode=mode, rep=rep, tq=tq, tk=tk),
        grid=(hk, t // tq),
        in_specs=in_specs,
        out_specs=pl.BlockSpec((rep, tq, d), lambda g, i: (g, i, 0)),
        out_shape=jax.ShapeDtypeStruct((hq, t, d), F32),
        scratch_shapes=[pltpu.VMEM((rep * tq, 1), F32), pltpu.VMEM((rep * tq, 2 * d), F32),
                        pltpu.VMEM((rep * tq, tk), F32), pltpu.VMEM((rep * tq, tk), F32)],
        compiler_params=_cparams(("arbitrary", "arbitrary")),
        name="attn_" + mode,
    )(*args)


def _compress_kernel(x_ref, pe_ref, w1_ref, w2_ref, o_ref):
    x = x_ref[...]
    n = x.shape[0]
    a = _dot((x + pe_ref[0:1, :]).astype(BF16), w1_ref[0].astype(BF16))
    b = _dot((x + pe_ref[1:2, :]).astype(BF16), w1_ref[1].astype(BF16))
    h = a + pltpu.roll(b, n - 1, 0)
    o_ref[...] = _dot(_silu(h).astype(BF16), w2_ref[...].astype(BF16)).astype(o_ref.dtype)


def _compress(xc, pe, w1, w2):
    _, g, n, wdt = xc.shape
    return pl.pallas_call(
        _compress_kernel,
        grid=(2, g),
        in_specs=[pl.BlockSpec((None, None, n, wdt), lambda a, b: (a, b, 0, 0)),
                  pl.BlockSpec((None, 2, wdt), lambda a, b: (a, 0, 0)),
                  pl.BlockSpec((None, 2, wdt, HEAD_DIM), lambda a, b: (a, 0, 0, 0)),
                  pl.BlockSpec((None, HEAD_DIM, HEAD_DIM), lambda a, b: (a, 0, 0))],
        out_specs=pl.BlockSpec((None, None, n, HEAD_DIM), lambda a, b: (a, b, 0, 0)),
        out_shape=jax.ShapeDtypeStruct((2, g, n, HEAD_DIM), BF16),
        compiler_params=_cparams(("arbitrary", "arbitrary")),
        name="nsa_compress",
    )(xc, pe, w1, w2)


def _importance_matrix(nc_pad, nblk_pad):
    r = SLC_LEN // CMP_STRIDE
    c = CMP_LEN // CMP_STRIDE
    i = np.arange(nc_pad)[:, None]
    b = np.arange(nblk_pad)[None, :]
    lo = r * b - (c - 1)
    return jnp.asarray(((i >= lo) & (i <= lo + r + c - 2)).astype(np.float32), dtype=BF16)


def _select_blocks(imp, cur, blk):
    forced = (blk == 0) | (blk == cur) | (blk == cur - 1)
    score = jnp.where(forced, 1e30, jnp.where(blk <= cur, imp, NEG))
    sel = jnp.zeros_like(score)
    blkf = blk.astype(F32)
    for _ in range(SLC_TOPK):
        mx = jnp.max(score, axis=-1, keepdims=True)
        idx = jnp.min(jnp.where(score == mx, blkf, 1e9), axis=-1, keepdims=True)
        pick = blkf == idx
        sel = jnp.where(pick, 1.0, sel)
        score = jnp.where(pick, -3e38, score)
    return sel


def _cmp_kernel(q_ref, kc_ref, vc_ref, imat_ref, o_ref, sel_ref, *, tq, nc):
    i = pl.program_id(0)
    rows = HPG * tq
    ncp = kc_ref.shape[1]
    row = lax.broadcasted_iota(jnp.int32, (rows, ncp), 0)
    qpos = i * tq + (row & (tq - 1))
    n = lax.broadcasted_iota(jnp.int32, (rows, ncp), 1)
    mask = (n * CMP_STRIDE + CMP_LEN - 1 <= qpos) & (n < nc)
    imps = []
    for g in range(G_NSA):
        q = q_ref[HPG * g:HPG * (g + 1)].reshape(rows, HEAD_DIM)
        s = jnp.where(mask, _dot_nt(q, kc_ref[g]), NEG)
        m = jnp.max(s, axis=-1, keepdims=True)
        e = jnp.where(mask, jnp.exp(s - m), 0.0)
        p = e / jnp.maximum(jnp.sum(e, axis=-1, keepdims=True), 1e-30)
        o_ref[HPG * g:HPG * (g + 1)] = _dot(p.astype(BF16), vc_ref[g]).reshape(HPG, tq, HEAD_DIM)
        psum = p[0:tq]
        for h in range(1, HPG):
            psum = psum + p[h * tq:(h + 1) * tq]
        imps.append(_dot_f32ish(psum, imat_ref[...]))
    imp = jnp.concatenate(imps, axis=0)
    blk = lax.broadcasted_iota(jnp.int32, imp.shape, 1)
    cur = (i * tq + (lax.broadcasted_iota(jnp.int32, imp.shape, 0) & (tq - 1))) // SLC_LEN
    bias = ((_select_blocks(imp, cur, blk) - 1.0) * SEL_BIAS).astype(BF16)
    for g in range(G_NSA):
        sel_ref[g] = bias[tq * g:tq * (g + 1)]


def _cmp_select(q, kcmp, vcmp, nc):
    _, t, d = q.shape
    g, ncp, _ = kcmp.shape
    tq = min(512, t)
    imat = _importance_matrix(ncp, LANES)
    return pl.pallas_call(
        functools.partial(_cmp_kernel, tq=tq, nc=nc),
        grid=(t // tq,),
        in_specs=[pl.BlockSpec((H_NSA, tq, d), lambda i: (0, i, 0)),
                  pl.BlockSpec((g, ncp, d), lambda i: (0, 0, 0)),
                  pl.BlockSpec((g, ncp, d), lambda i: (0, 0, 0)),
                  pl.BlockSpec((ncp, LANES), lambda i: (0, 0))],
        out_specs=[pl.BlockSpec((H_NSA, tq, d), lambda i: (0, i, 0)),
                   pl.BlockSpec((g, tq, LANES), lambda i: (0, i, 0))],
        out_shape=[jax.ShapeDtypeStruct((H_NSA, t, d), F32), jax.ShapeDtypeStruct((g, t, LANES), BF16)],
        compiler_params=_cparams(("arbitrary",)),
        name="nsa_cmp_select",
    )(q, kcmp, vcmp, imat)


def _merge_kernel(osb_ref, odf_ref, ocm_ref, osl_ref, owi_ref, gate_ref, lam_ref, dn_ref, x_ref, ga_ref, nrm_ref,
                  w_ref, o_ref, cat_scr, *, lam_init):
    lv = lam_ref[...]
    lam = (jnp.exp(jnp.sum(lv[0:1] * lv[1:2], axis=-1, keepdims=True))
           - jnp.exp(jnp.sum(lv[2:3] * lv[3:4], axis=-1, keepdims=True)) + lam_init)
    for h in range(H_SB):
        cat_scr[:, h * HEAD_DIM:(h + 1) * HEAD_DIM] = osb_ref[h].astype(BF16)
    dn = dn_ref[...]
    for h in range(H_DIFF):
        ob = odf_ref[2 * h] - lam * odf_ref[2 * h + 1]
        ob = _rms_rows(ob, dn) * (1.0 - lam_init)
        c0 = (H_SB + h) * HEAD_DIM
        cat_scr[:, c0:c0 + HEAD_DIM] = ob.astype(BF16)
    gates = gate_ref[...]
    for h in range(H_NSA):
        oc = (gates[:, 3 * h:3 * h + 1] * ocm_ref[h] + gates[:, 3 * h + 1:3 * h + 2] * osl_ref[h]
              + gates[:, 3 * h + 2:3 * h + 3] * owi_ref[h])
        c0 = (H_SB + H_DIFF + h) * HEAD_DIM
        cat_scr[:, c0:c0 + HEAD_DIM] = oc.astype(BF16)
    y = _dot(cat_scr[...], w_ref[...])
    o_ref[...] = x_ref[...] + ga_ref[...] * _rms_rows(y, nrm_ref[...])


def _merge(osb, odf, ocm, osl, owi, gates, lam_vecs, diff_norm, x, ga, nrm, w, layer):
    t, d = x.shape
    tm = min(256, t)
    lam_init = 0.8 - 0.6 * math.exp(-0.3 * layer)
    hspec = lambda n: pl.BlockSpec((n, tm, HEAD_DIM), lambda i: (0, i, 0))
    return pl.pallas_call(
        functools.partial(_merge_kernel, lam_init=lam_init),
        grid=(t // tm,),
        in_specs=[hspec(H_SB), hspec(2 * H_DIFF), hspec(H_NSA), hspec(H_NSA), hspec(H_NSA),
                  pl.BlockSpec((tm, LANES), lambda i: (i, 0)),
                  pl.BlockSpec(lam_vecs.shape, lambda i: (0, 0)),
                  pl.BlockSpec((1, HEAD_DIM), lambda i: (0, 0)),
                  pl.BlockSpec((tm, d), lambda i: (i, 0)),
                  _row_spec(ga, tm, 1),
                  pl.BlockSpec((1, d), lambda i: (0, 0)),
                  pl.BlockSpec((None,) + w.shape[1:], lambda i: (layer, 0, 0))],
        out_specs=pl.BlockSpec((tm, d), lambda i: (i, 0)),
        out_shape=jax.ShapeDtypeStruct((t, d), F32),
        scratch_shapes=[pltpu.VMEM((tm, N_HEADS * HEAD_DIM), BF16)],
        compiler_params=_cparams(("arbitrary",)),
        name="merge_out_proj",
    )(osb, odf, ocm, osl, owi, gates, lam_vecs, diff_norm, x, ga, nrm, w)


def _ffn_kernel(x_ref, g2_ref, sc_ref, sh_ref, wg_ref, wu_ref, wd_ref, gf_ref, g3_ref, o_ref, h_scr, acc_scr):
    f = pl.program_id(1)

    @pl.when(f == 0)
    def _():
        y = _rms_rows(x_ref[...], g2_ref[...])
        h_scr[...] = (y * (1.0 + sc_ref[...]) + sh_ref[...]).astype(BF16)
        acc_scr[...] = jnp.zeros_like(acc_scr)

    h = h_scr[...]
    a = _dot(h, wg_ref[...])
    b = _dot(h, wu_ref[...])
    acc_scr[...] += _dot((_silu(a) * b).astype(BF16), wd_ref[...])

    @pl.when(f == pl.num_programs(1) - 1)
    def _():
        o_ref[...] = x_ref[...] + gf_ref[...] * _rms_rows(acc_scr[...], g3_ref[...])


def _ffn(x, g2, sc, sh, wgu, wd, gf, g3, layer):
    t, d = x.shape
    dff = wd.shape[1]
    tm = min(512, t)
    tf = 512 if dff % 512 == 0 else dff
    nf = dff // tf
    return pl.pallas_call(
        _ffn_kernel,
        grid=(t // tm, nf),
        in_specs=[pl.BlockSpec((tm, d), lambda i, f: (i, 0)),
                  pl.BlockSpec((1, d), lambda i, f: (0, 0)),
                  _row_spec(sc, tm, 2), _row_spec(sh, tm, 2),
                  pl.BlockSpec((None, d, tf), lambda i, f: (layer, 0, f)),
                  pl.BlockSpec((None, d, tf), lambda i, f: (layer, 0, f + nf)),
                  pl.BlockSpec((None, tf, d), lambda i, f: (layer, f, 0)),
                  _row_spec(gf, tm, 2),
                  pl.BlockSpec((1, d), lambda i, f: (0, 0))],
        out_specs=pl.BlockSpec((tm, d), lambda i, f: (i, 0)),
        out_shape=jax.ShapeDtypeStruct((t, d), F32),
        scratch_shapes=[pltpu.VMEM((tm, d), BF16), pltpu.VMEM((tm, d), F32)],
        compiler_params=_cparams(("arbitrary", "arbitrary")),
        name="ffn_swiglu",
    )(x, g2, sc, sh, wgu, wgu, wd, gf, g3)


DEC_ROWS = 32
BLK_LANES = 256


def _dec_kernel(pt_ref, *refs, pg, nps, past, ns):
    page_refs = refs[:pg]
    (new_ref, qsb_ref, qdf_ref, qsl_ref, u_ref, w1_ref, bsel_ref,
     osb_ref, odf_ref, sacc_ref, sm_ref, sl_ref, cab_ref,
     carry_scr, asb_scr, mdf_scr, adf_scr) = refs[pg:]
    j = pl.program_id(1)
    chunks_per_page = PAGE_SIZE // CMP_STRIDE

    @pl.when(j == 0)
    def _():
        carry_scr[...] = jnp.zeros_like(carry_scr)
        asb_scr[...] = jnp.zeros_like(asb_scr)
        mdf_scr[...] = jnp.full_like(mdf_scr, NEG)
        adf_scr[...] = jnp.zeros_like(adf_scr)

    def step(prefs, page0, masked):
        n = len(prefs)
        nk = n * PAGE_SIZE
        lane = lax.broadcasted_iota(jnp.int32, (DEC_ROWS, nk), 1)
        row = lax.broadcasted_iota(jnp.int32, (DEC_ROWS, nk), 0)
        kpos = page0 * PAGE_SIZE + lane
        qpos = past + (row & (ns - 1))

        def tok_rows(r):
            return jnp.concatenate([p[pl.ds(r, PAGE_SIZE, stride=ROWS_PER_TOK), :] for p in prefs], axis=0).astype(BF16)

        def stick_breaking():
            z = jnp.concatenate([_dot_nt(qsb_ref[8 * h:8 * h + 8, :], tok_rows(2 * h)) for h in range(H_SB)], axis=0)
            l1p = jnp.log(1.0 + jnp.exp(-jnp.abs(z)))
            ls_pos = jnp.minimum(z, 0.0) - l1p
            lk = ls_pos - z
            if masked:
                valid = kpos < qpos
                lk = jnp.where(valid, lk, 0.0)
            hi = lk.astype(BF16)
            lo = (lk - hi.astype(F32)).astype(BF16)
            pieces = ([hi[:, PAGE_SIZE * i:PAGE_SIZE * (i + 1)] for i in range(n)]
                      + [lo[:, PAGE_SIZE * i:PAGE_SIZE * (i + 1)] for i in range(n)])
            local = _dot(jnp.concatenate(pieces, axis=0), u_ref[...])
            run = carry_scr[...]
            after = [None] * n
            for i in reversed(range(n)):
                after[i] = (local[DEC_ROWS * i:DEC_ROWS * (i + 1)] + local[DEC_ROWS * (n + i):DEC_ROWS * (n + i + 1)]
                            + run)
                run = run + jnp.sum(lk[:, PAGE_SIZE * i:PAGE_SIZE * (i + 1)], axis=-1, keepdims=True)
            w = jnp.exp(ls_pos + jnp.concatenate(after, axis=1))
            if masked:
                w = jnp.where(valid, w, 0.0)
            wb = w.astype(BF16)
            asb_scr[...] += jnp.concatenate(
                [_dot(wb[8 * h:8 * h + 8, :], tok_rows(2 * h + 1)) for h in range(H_SB)], axis=0)
            carry_scr[...] = run

        if masked:
            stick_breaking()
        else:
            pl.when(jnp.max(carry_scr[...]) > SB_UNDERFLOW)(stick_breaking)

        r0 = 2 * H_SB
        s = jnp.concatenate([_dot_nt(qdf_ref[8 * h:8 * h + 8, :], tok_rows(r0 + 2 * h)) for h in range(H_DIFF)], axis=0)
        if masked:
            s = jnp.where(kpos <= qpos, s, NEG)
        m_old = mdf_scr[...]
        m_new = jnp.maximum(m_old, jnp.max(s, axis=-1, keepdims=True))
        eb = jnp.exp(s - m_new).astype(BF16)
        ones = jnp.ones((nk, HEAD_DIM), BF16)
        adf_scr[...] = jnp.exp(m_old - m_new) * adf_scr[...] + jnp.concatenate(
            [_dot(eb[8 * h:8 * h + 8, :], jnp.concatenate([tok_rows(r0 + 2 * h + 1), ones], axis=1))
             for h in range(H_DIFF)], axis=0)
        mdf_scr[...] = m_new

        r0 = 2 * (H_SB + H_DIFF + G_NSA)
        s = jnp.concatenate([_dot_nt(qsl_ref[16 * g:16 * g + 16, :], tok_rows(r0 + 2 * g)) for g in range(G_NSA)], axis=0)
        if masked:
            valid = kpos <= qpos
            s = jnp.where(valid, s, NEG)
        blk = lane // SLC_LEN
        col = lax.broadcasted_iota(jnp.int32, (DEC_ROWS, LANES), 1)
        mref = jnp.zeros((DEC_ROWS, nk), F32)
        mcols = jnp.full((DEC_ROWS, LANES), NEG, F32)
        for b in range(2 * n):
            mb = jnp.max(jnp.where(blk == b, s, NEG), axis=-1, keepdims=True)
            mref = jnp.where(blk == b, mb, mref)
            mcols = jnp.where(col == b, mb, mcols)
        e = jnp.exp(s - mref)
        if masked:
            e = jnp.where(valid, e, 0.0)
        eb = e.astype(BF16)
        sm_ref[...] = mcols
        sl_ref[...] = _dot(eb, bsel_ref[0:nk, :])
        blk16 = lax.broadcasted_iota(jnp.int32, (16, nk), 1) // SLC_LEN
        for g in range(G_NSA):
            eg = eb[16 * g:16 * g + 16, :]
            per_blk = jnp.concatenate([jnp.where(blk16 == b, eg, jnp.zeros_like(eg)) for b in range(2 * n)], axis=0)
            res = _dot(per_blk, tok_rows(r0 + 2 * g + 1))
            for b in range(2 * n):
                sacc_ref[b, 16 * g:16 * g + 16, :] = res[16 * b:16 * b + 16]
        for b in range(2 * n, 2 * pg):
            sacc_ref[b] = jnp.zeros((DEC_ROWS, HEAD_DIM), F32)

        r0 = 2 * (H_SB + H_DIFF)
        lhs = jnp.concatenate(
            [jnp.concatenate([p[pl.ds((CMP_STRIDE * m + l) * ROWS_PER_TOK + r0, 8), :] for l in range(CMP_STRIDE)], axis=1)
             for p in prefs for m in range(chunks_per_page)], axis=0).astype(BF16)
        res = _dot(lhs, w1_ref[...])
        crow = lax.broadcasted_iota(jnp.int32, (n * 8 * chunks_per_page, 2 * HEAD_DIM), 0)
        keep = jnp.where((crow & 1) == 0, res[:, :2 * HEAD_DIM], res[:, 2 * HEAD_DIM:])
        live = n * 8 * chunks_per_page
        for half in range(2):
            cab_ref[half, 0:live, :] = keep[:, half * HEAD_DIM:(half + 1) * HEAD_DIM]
            if n < pg:
                cab_ref[half, live:, :] = jnp.zeros(((pg - n) * 8 * chunks_per_page, HEAD_DIM), F32)

    @pl.when(j == 0)
    def _():
        step([new_ref], nps * pg, True)

    @pl.when(j > 0)
    def _():
        step(list(page_refs), (nps - j) * pg, False)

    @pl.when(j == nps)
    def _():
        osb_ref[...] = asb_scr[...]
        adf = adf_scr[...]
        odf_ref[...] = adf[:, :HEAD_DIM] / jnp.maximum(adf[:, HEAD_DIM:], 1e-30)


def _dec_main(layer, page_table, cache_rows, new_page, qsb, qdf, qsl, w1all, past, ns):
    nb, npages = page_table.shape
    pg = next(c for c in (8, 4, 2, 1) if npages % c == 0 and npages >= 2 * c)
    nps = npages // pg
    rows_per_page = PAGE_SIZE * ROWS_PER_TOK

    def page_spec(i):
        def imap(b, j, pt):
            return (layer, pt[b, (nps - jnp.maximum(j, 1)) * pg + i], 0, 0)
        return pl.BlockSpec((None, None, rows_per_page, HEAD_DIM), imap)

    qspec = pl.BlockSpec((None, DEC_ROWS, HEAD_DIM), lambda b, j, pt: (b, 0, 0))
    nblk_pad = (nps + 1) * 2 * pg
    cab_rows = pg * 8 * (PAGE_SIZE // CMP_STRIDE)
    key = np.arange(pg * PAGE_SIZE)[:, None]
    bsel = jnp.asarray((key // SLC_LEN == np.arange(LANES)[None, :]).astype(np.float32), dtype=BF16)
    stat_spec = pl.BlockSpec((None, None, DEC_ROWS, LANES), lambda b, j, pt: (b, nps - j, 0, 0))
    grid_spec = pltpu.PrefetchScalarGridSpec(
        num_scalar_prefetch=1,
        grid=(nb, nps + 1),
        in_specs=[page_spec(i) for i in range(pg)] + [
            pl.BlockSpec((None, rows_per_page, HEAD_DIM), lambda b, j, pt: (b, 0, 0)),
            qspec, qspec, qspec,
            pl.BlockSpec((PAGE_SIZE, PAGE_SIZE), lambda b, j, pt: (0, 0)),
            pl.BlockSpec(w1all.shape, lambda b, j, pt: (0, 0)),
            pl.BlockSpec(bsel.shape, lambda b, j, pt: (0, 0))],
        out_specs=[
            qspec, qspec,
            pl.BlockSpec((None, 2 * pg, DEC_ROWS, HEAD_DIM), lambda b, j, pt: (b, nps - j, 0, 0)),
            stat_spec, stat_spec,
            pl.BlockSpec((None, 2, cab_rows, HEAD_DIM), lambda b, j, pt: (b, 0, nps - j, 0))],
        scratch_shapes=[pltpu.VMEM((DEC_ROWS, 1), F32), pltpu.VMEM((DEC_ROWS, HEAD_DIM), F32),
                        pltpu.VMEM((DEC_ROWS, 1), F32), pltpu.VMEM((DEC_ROWS, 2 * HEAD_DIM), F32)])
    out_shape = [jax.ShapeDtypeStruct((nb, DEC_ROWS, HEAD_DIM), F32),
                 jax.ShapeDtypeStruct((nb, DEC_ROWS, HEAD_DIM), F32),
                 jax.ShapeDtypeStruct((nb, nblk_pad, DEC_ROWS, HEAD_DIM), F32),
                 jax.ShapeDtypeStruct((nb, nps + 1, DEC_ROWS, LANES), F32),
                 jax.ShapeDtypeStruct((nb, nps + 1, DEC_ROWS, LANES), F32),
                 jax.ShapeDtypeStruct((nb, 2, (nps + 1) * cab_rows, HEAD_DIM), F32)]
    o_sb, o_df, sacc, sm, sl, cab = pl.pallas_call(
        functools.partial(_dec_kernel, pg=pg, nps=nps, past=past, ns=ns),
        grid_spec=grid_spec,
        out_shape=out_shape,
        compiler_params=_cparams(("arbitrary", "arbitrary")),
        name="decode_paged",
    )(page_table, *([cache_rows] * pg), new_page, qsb, qdf, qsl, _suffix_matrix(PAGE_SIZE), w1all, bsel)

    def by_block(stat, fill):
        stat = stat[..., :2 * pg].transpose(0, 2, 1, 3).reshape(nb, DEC_ROWS, nblk_pad)
        return jnp.pad(stat, ((0, 0), (0, 0), (0, BLK_LANES - nblk_pad)), constant_values=fill)

    return o_sb, o_df, sacc, by_block(sm, NEG), by_block(sl, 0.0), cab


def _cmp_bias_kernel(pe_ref, w1_ref, o_ref):
    for kv in range(2):
        for c in range(2):
            o_ref[kv, c] = _dot(jnp.broadcast_to(pe_ref[kv, c:c + 1, :], (8, pe_ref.shape[-1])).astype(BF16),
                                w1_ref[kv, c].astype(BF16))


def _cmp_bias(pe, w1):
    return pl.pallas_call(
        _cmp_bias_kernel,
        out_shape=jax.ShapeDtypeStruct((2, 2, 8, HEAD_DIM), F32),
        compiler_params=pltpu.CompilerParams(vmem_limit_bytes=VMEM_LIMIT),
        name="cmp_bias",
    )(pe, w1)


def _dec_final_kernel(cab_ref, bias_ref, w2_ref, q_ref, imat_ref, jmat_ref, sacc_ref, sm_ref, sl_ref,
                      swin_ref, nwin_ref, *rest, nseq, **static):
    outs = rest[-4:]
    for s in range(nseq):
        _dec_final_one(cab_ref.at[s], bias_ref, w2_ref, q_ref.at[s], imat_ref, jmat_ref, sacc_ref.at[s], sm_ref.at[s],
                       sl_ref.at[s], swin_ref.at[s], nwin_ref.at[s], *[o.at[s] for o in outs], **static)


def _dec_final_one(cab_ref, bias_ref, w2_ref, q_ref, imat_ref, jmat_ref, sacc_ref, sm_ref, sl_ref,
                   swin_ref, nwin_ref, ocm_ref, osl_ref, owi_ref, wout_ref, *, nc, past, ns, nblk_pad, wb):
    win_rows, new_rows = swin_ref.shape[0], ns * 2 * G_NSA
    wout_ref[0:win_rows - new_rows, :] = swin_ref[new_rows:win_rows, :]
    wout_ref[win_rows - new_rows:win_rows, :] = nwin_ref[0:new_rows, :]
    ncp = cab_ref.shape[1] // 8
    rows16 = lax.broadcasted_iota(jnp.int32, (16, 1), 0)
    qpos16 = past + (rows16 & (ns - 1))
    imps = []
    for g in range(G_NSA):
        q = q_ref[16 * g:16 * g + 16, :]
        cmp_kv = []
        for kv in range(2):
            a = cab_ref[0, pl.ds(2 * g + kv, ncp, stride=8), :]
            b = cab_ref[1, pl.ds(2 * g + kv, ncp, stride=8), :]
            h = a + pltpu.roll(b, ncp - 1, 0) + bias_ref[kv, 0, 0:1, :] + bias_ref[kv, 1, 0:1, :]
            cmp_kv.append(_dot(_silu(h).astype(BF16), w2_ref[kv].astype(BF16)).astype(BF16))
        kc, vc = cmp_kv
        s = _dot_nt(q, kc)
        n = lax.broadcasted_iota(jnp.int32, s.shape, 1)
        mask = (n * CMP_STRIDE + CMP_LEN - 1 <= qpos16) & (n < nc)
        s = jnp.where(mask, s, NEG)
        m = jnp.max(s, axis=-1, keepdims=True)
        e = jnp.where(mask, jnp.exp(s - m), 0.0)
        p = e / jnp.maximum(jnp.sum(e, axis=-1, keepdims=True), 1e-30)
        ocm_ref[16 * g:16 * g + 16, :] = _dot(p.astype(BF16), vc)
        hi, mid, lo = _split3(p)
        jm = jmat_ref[...]
        psum = _dot(jm, hi) + _dot(jm, mid) + _dot(jm, lo)
        imps.append(_dot_f32ish(psum, imat_ref[...]))
    imp = jnp.concatenate(imps, axis=0)
    blk = lax.broadcasted_iota(jnp.int32, imp.shape, 1)
    cur = (past + (lax.broadcasted_iota(jnp.int32, imp.shape, 0) & (ns - 1))) // SLC_LEN
    sel_all = _select_blocks(imp, cur, blk)
    for g in range(G_NSA):
        q = q_ref[16 * g:16 * g + 16, :]
        sel = sel_all[16 * g:16 * g + 16]
        mb = sm_ref[16 * g:16 * g + 16, :]
        lb = sl_ref[16 * g:16 * g + 16, :]
        picked = sel > 0.5
        mtot = jnp.max(jnp.where(picked, mb, NEG), axis=-1, keepdims=True)
        wgt = jnp.where(picked, jnp.exp(mb - mtot), 0.0)
        den = jnp.sum(wgt * lb, axis=-1, keepdims=True)
        num = jnp.zeros((16, HEAD_DIM), F32)
        for b in range(nblk_pad):
            r0 = b * DEC_ROWS + 16 * g
            num = num + wgt[:, b:b + 1] * sacc_ref[r0:r0 + 16, :]
        osl_ref[16 * g:16 * g + 16, :] = num / jnp.maximum(den, 1e-30)
        kw = swin_ref[pl.ds(g, wb, stride=2 * G_NSA), :].astype(BF16)
        vw = swin_ref[pl.ds(G_NSA + g, wb, stride=2 * G_NSA), :].astype(BF16)
        kn = nwin_ref[pl.ds(g, 8, stride=2 * G_NSA), :].astype(BF16)
        vn = nwin_ref[pl.ds(G_NSA + g, 8, stride=2 * G_NSA), :].astype(BF16)
        s1 = _dot_nt(q, kw)
        s2 = _dot_nt(q, kn)
        d1 = qpos16 - (past - wb + lax.broadcasted_iota(jnp.int32, s1.shape, 1))
        k1 = (d1 >= 0) & (d1 <= WINDOW)
        d2 = qpos16 - (past + lax.broadcasted_iota(jnp.int32, s2.shape, 1))
        k2 = (d2 >= 0) & (d2 <= WINDOW)
        s1 = jnp.where(k1, s1, NEG)
        s2 = jnp.where(k2, s2, NEG)
        m = jnp.maximum(jnp.max(s1, axis=-1, keepdims=True), jnp.max(s2, axis=-1, keepdims=True))
        e1 = jnp.where(k1, jnp.exp(s1 - m), 0.0)
        e2 = jnp.where(k2, jnp.exp(s2 - m), 0.0)
        den = jnp.sum(e1, axis=-1, keepdims=True) + jnp.sum(e2, axis=-1, keepdims=True)
        owi_ref[16 * g:16 * g + 16, :] = (_dot(e1.astype(BF16), vw) + _dot(e2.astype(BF16), vn)) / jnp.maximum(den, 1e-30)


def _dec_final(layer, cab, bias, w2, qsl, sacc, sm, sl, state_rows, new_win, nc, past, ns, win_stack):
    nb = cab.shape[0]
    depth = state_rows.shape[0]
    ncp = cab.shape[2] // 8
    nblk_pad = sacc.shape[1] // DEC_ROWS
    wb = state_rows.shape[2] // (2 * G_NSA)
    imat = _importance_matrix(ncp, BLK_LANES)
    t_of = np.arange(16) % ns
    jmat = jnp.asarray((t_of[:, None] == t_of[None, :]).astype(np.float32), dtype=BF16)
    nseq = 2 if nb % 2 == 0 else 1
    ospec = pl.BlockSpec((nseq, DEC_ROWS, HEAD_DIM), lambda b: (b, 0, 0))
    wspec = pl.BlockSpec((None, nseq, state_rows.shape[2], HEAD_DIM), lambda b: (layer, b, 0, 0))
    in_specs = [pl.BlockSpec((nseq, 2, ncp * 8, HEAD_DIM), lambda b: (b, 0, 0, 0)),
                pl.BlockSpec(bias.shape, lambda b: (0, 0, 0, 0)),
                pl.BlockSpec(w2.shape, lambda b: (0, 0, 0)),
                ospec,
                pl.BlockSpec(imat.shape, lambda b: (0, 0)),
                pl.BlockSpec(jmat.shape, lambda b: (0, 0)),
                pl.BlockSpec((nseq, nblk_pad * DEC_ROWS, HEAD_DIM), lambda b: (b, 0, 0)),
                pl.BlockSpec((nseq, DEC_ROWS, BLK_LANES), lambda b: (b, 0, 0)),
                pl.BlockSpec((nseq, DEC_ROWS, BLK_LANES), lambda b: (b, 0, 0)),
                wspec,
                pl.BlockSpec((nseq, new_win.shape[1], HEAD_DIM), lambda b: (b, 0, 0))]
    args = [cab, bias, w2, qsl, imat, jmat, sacc, sm, sl, state_rows, new_win]
    aliases = {}
    if win_stack is not None:
        in_specs.append(pl.BlockSpec(memory_space=pl.ANY))
        args.append(win_stack)
        aliases = {len(args) - 1: 3}
    return pl.pallas_call(
        functools.partial(_dec_final_kernel, nseq=nseq, nc=nc, past=past, ns=ns, nblk_pad=nblk_pad, wb=wb),
        grid=(nb // nseq,),
        in_specs=in_specs,
        out_specs=[ospec, ospec, ospec, wspec],
        out_shape=[jax.ShapeDtypeStruct((nb, DEC_ROWS, HEAD_DIM), F32)] * 3 + [
            jax.ShapeDtypeStruct((depth,) + state_rows.shape[1:], F32)],
        input_output_aliases=aliases,
        compiler_params=_cparams(("arbitrary",)),
        name="decode_nsa_finish",
    )(*args)


def _sample_mixer(x, mod, lw, layer, tabs, page_table, cache_rows, state_rows, nb, ns, past, win_stack):
    sh_a, sc_a, g_a = mod[0], mod[1], mod[2]
    proj = _proj(x, lw["n0"], sc_a, sh_a, lw["w_in"], layer)
    (qsb, _, _, qdf, _, _, qns, _, _, _, _, _, _, gates, kv_rows, win_rows) = _postproj(proj, tabs)

    def per_seq(a, heads, rows_per_head):
        a = a.reshape(heads, nb, ns, HEAD_DIM).transpose(1, 0, 2, 3)
        a = jnp.pad(a, ((0, 0), (0, 0), (0, rows_per_head - ns), (0, 0)))
        return a.reshape(nb, heads * rows_per_head, HEAD_DIM)

    qsb_d = per_seq(qsb, H_SB, 8)
    qdf_d = per_seq(qdf, 2 * H_DIFF, ns)
    qsl_d = per_seq(qns, H_NSA, ns)
    kv_tok = kv_rows.reshape(nb, ns * ROWS_PER_TOK, HEAD_DIM)
    new_page = jnp.pad(kv_tok, ((0, 0), (0, (PAGE_SIZE - ns) * ROWS_PER_TOK), (0, 0)))
    w1 = lw["cmp_w1"]
    w1all = jnp.concatenate([w1[0, 0], w1[0, 1], w1[1, 0], w1[1, 1]], axis=-1).astype(BF16)
    o_sb, o_df, sacc, sm, sl, cab = _dec_main(layer, page_table, cache_rows, new_page, qsb_d, qdf_d, qsl_d, w1all,
                                              past, ns)
    bias = _cmp_bias(lw["cmp_pe"], w1)
    new_win = jnp.pad(win_rows.reshape(nb, ns * 2 * G_NSA, HEAD_DIM), ((0, 0), (0, (8 - ns) * 2 * G_NSA), (0, 0)))
    padded = -(-(past + ns) // SLC_LEN) * SLC_LEN
    nc = padded // CMP_STRIDE - CMP_LEN // CMP_STRIDE + 1
    o_cmp, o_slc, o_win, win_stack = _dec_final(layer, cab, bias, lw["cmp_w2"], qsl_d, sacc.reshape(nb, -1, HEAD_DIM),
                                                sm, sl, state_rows, new_win, nc, past, ns, win_stack)

    def head_major(a, heads, rows_per_head):
        a = a.reshape(nb, heads, rows_per_head, HEAD_DIM)[:, :, :ns]
        return a.transpose(1, 0, 2, 3).reshape(heads, nb * ns, HEAD_DIM)

    x1 = _merge(head_major(o_sb, H_SB, 8), head_major(o_df, 2 * H_DIFF, ns), head_major(o_cmp, H_NSA, ns),
                head_major(o_slc, H_NSA, ns), head_major(o_win, H_NSA, ns), gates, lw["diff_lambda"],
                lw["diff_norm"], x, g_a, lw["n1"], lw["w_out"], layer)
    return x1, kv_tok, win_stack


def _split_mod(mod):
    return [mod[:, i * (mod.shape[1] // 6):(i + 1) * (mod.shape[1] // 6)] for i in range(6)]


def _prompt_mixer(x, mod, lw, layer, tabs, depth, kv_stack):
    t = x.shape[0]
    sh_a, sc_a, g_a = mod[0], mod[1], mod[2]
    proj = _proj(x, lw["n0"], sc_a, sh_a, lw["w_in"], layer)
    (qsb, ksb, vsb, qdf, kdf, vdf, qns, kc, vc, ks, vs, kw, vw, gates, kv_rows, win_rows) = _postproj(
        proj, tabs, layer, depth, kv_stack)
    o_sb = _sb_attention(qsb, ksb, vsb)
    o_df = _flash(qdf, kdf, vdf, "causal")
    nch = t // CMP_STRIDE
    xc = jnp.stack([kc, vc]).reshape(2, G_NSA, nch, CMP_STRIDE * HEAD_DIM)
    cmp_kv = _compress(xc, lw["cmp_pe"], lw["cmp_w1"], lw["cmp_w2"])
    o_cmp, sel = _cmp_select(qns, cmp_kv[0], cmp_kv[1], nch - 1)
    o_slc = _flash(qns, ks, vs, "select", sel)
    o_win = _flash(qns, kw, vw, "window")
    x1 = _merge(o_sb, o_df, o_cmp, o_slc, o_win, gates, lw["diff_lambda"], lw["diff_norm"], x, g_a, lw["n1"],
                lw["w_out"], layer)
    return x1, kv_rows, win_rows


def _layer_weights(layer, norms, w_in_p, w_out_b, diff_lambda, diff_norm, cmp_pe, cmp_w1, cmp_w2, wgu_b, wd_b):
    d = norms.shape[-1]
    return dict(
        n0=norms[layer, 0].reshape(1, d), n1=norms[layer, 1].reshape(1, d),
        n2=norms[layer, 2].reshape(1, d), n3=norms[layer, 3].reshape(1, d),
        w_in=w_in_p, w_out=w_out_b,
        diff_lambda=diff_lambda[layer], diff_norm=diff_norm[layer].reshape(1, HEAD_DIM),
        cmp_pe=cmp_pe[layer].reshape(2, 2, CMP_STRIDE * HEAD_DIM),
        cmp_w1=cmp_w1[layer].reshape(2, 2, CMP_STRIDE * HEAD_DIM, HEAD_DIM),
        cmp_w2=cmp_w2[layer], w_gate_up=wgu_b, w_down=wd_b)


def kernel(x_prompt, x_sample, cache_kv, state_win_kv, page_table, c_prompt, c_sample, ada_w, ada_b, norms, w_in,
           w_out, diff_lambda, diff_norm, cmp_pe, cmp_w1, cmp_w2, w_gate_up, w_down):
    depth = ada_w.shape[0]
    _, seq, d = x_prompt.shape
    nb, ns, _ = x_sample.shape
    past = page_table.shape[1] * PAGE_SIZE

    w_in_p = jnp.pad(w_in, ((0, 0), (0, 0), (0, N_IN_PAD - N_IN))).astype(BF16)
    w_out_b = w_out.astype(BF16)
    wgu_b = w_gate_up.astype(BF16)
    wd_b = w_down.astype(BF16)

    n_c = 1 + nb
    rows = -(-n_c // 8) * 8
    c_all = jnp.pad(jnp.concatenate([c_prompt, c_sample], axis=0), ((0, rows - n_c), (0, 0)))
    mod_all = _ada(c_all, ada_w, ada_b)

    tabs_p = _rope_tables(jnp.arange(seq))
    tabs_s = _rope_tables(past + jnp.arange(nb * ns) % ns)
    cache_rows = cache_kv.transpose(0, 1, 2, 4, 3, 5).reshape(depth, cache_kv.shape[1], PAGE_SIZE * ROWS_PER_TOK,
                                                               HEAD_DIM)
    wb_s = state_win_kv.shape[2]
    state_rows = state_win_kv.reshape(depth, nb, wb_s * 2 * G_NSA, HEAD_DIM)
    xp = x_prompt.reshape(seq, d)
    xs = x_sample.reshape(nb * ns, d)
    win_p, kv_s = [], []
    kv_stack = win_stack = None
    for layer in range(depth):
        lw = _layer_weights(layer, norms, w_in_p, w_out_b, diff_lambda, diff_norm, cmp_pe, cmp_w1, cmp_w2, wgu_b, wd_b)
        mod_p = _split_mod(mod_all[layer, 0:1])
        x1, kv_stack, win_rows = _prompt_mixer(xp, mod_p, lw, layer, tabs_p, depth, kv_stack)
        xp = _ffn(x1, lw["n2"], mod_p[4], mod_p[3], lw["w_gate_up"], lw["w_down"], mod_p[5], lw["n3"], layer)
        wb = min(WINDOW, seq)
        win_p.append(win_rows[seq - wb:].reshape(1, wb, 2, G_NSA, HEAD_DIM))

        mod_s = _split_mod(jnp.repeat(mod_all[layer, 1:1 + nb], ns, axis=0))
        x1, kv_tok, win_stack = _sample_mixer(xs, mod_s, lw, layer, tabs_s, page_table, cache_rows, state_rows,
                                              nb, ns, past, win_stack)
        xs = _ffn(x1, lw["n2"], mod_s[4], mod_s[3], lw["w_gate_up"], lw["w_down"], mod_s[5], lw["n3"], layer)
        kv_s.append(kv_tok.reshape(nb, ns, N_KV, 2, HEAD_DIM).transpose(0, 1, 3, 2, 4))
    kv_prompt = kv_stack.reshape(depth, 1, seq, N_KV, 2, HEAD_DIM).transpose(0, 1, 2, 4, 3, 5)
    win_sample = win_stack.reshape(depth, nb, wb_s, 2, G_NSA, HEAD_DIM)
    return (xp.reshape(1, seq, d), xs.reshape(nb, ns, d), kv_prompt, jnp.stack(kv_s), jnp.stack(win_p), win_sample)
```
